```python
import math
import jax, jax.numpy as jnp
from jax import lax
import numpy as np

D_MODEL = 1024
BATCH = 8
SEQ = 4096
DEPTH = 4

GRID_W = 64
CTX_LEN = 256
HEAD_DIM = 64
N_HEADS_NA = 8
NA_WIDTH = N_HEADS_NA * HEAD_DIM
HY_WIDTH = D_MODEL - NA_WIDTH
HY_ORDER = 2
HY_EMB = 33
HY_HIDDEN = 64
HY_FAST_DECAY = 0.3
HY_SLOW_DECAY = 1.5
HY_TARGET = 1e-2
SHORT_CONV = 3
WIN_R = 8
WIN_C = 16
N_HEADS_DIFF = D_MODEL // (2 * HEAD_DIM)
DIFF_QK = N_HEADS_DIFF * 2 * HEAD_DIM
GATE_WIDTH = D_MODEL
IN_WIDTH = 4 * D_MODEL
HY_LO = 3 * NA_WIDTH
HY_HI = HY_LO + (HY_ORDER + 1) * HY_WIDTH
GATE_LO = IN_WIDTH - GATE_WIDTH
Q_BLOCK = 128
N_EVEN = (DEPTH + 1) // 2
N_ODD = DEPTH // 2
EPS = 1e-6
ROPE_BASE = 10000.0

kernel_name = "hybrid_natten_hyena_diffattn_prefix_dit"

F32 = jnp.float32


def _rmsnorm(x, g):
    x32 = x.astype(F32)
    y = x32 * lax.rsqrt(jnp.mean(x32 * x32, axis=-1, keepdims=True) + EPS)
    return (y * g.astype(F32)).astype(x.dtype)


def _modulation(vec, w_mod, b_mod, n_parts):
    m = jax.nn.silu(vec) @ w_mod[:, :n_parts * D_MODEL] + b_mod[:n_parts * D_MODEL]
    return jnp.split(m, n_parts, axis=-1)


def _axial_rope(n_tokens):
    t = jnp.arange(n_tokens, dtype=jnp.int32)
    row = (t // GRID_W).astype(F32)
    col = (t % GRID_W).astype(F32)
    n_freq = HEAD_DIM // 4
    inv = ROPE_BASE ** (-jnp.arange(n_freq, dtype=F32) / n_freq)
    ang = jnp.concatenate([row[:, None] * inv, col[:, None] * inv], axis=-1)
    return jnp.cos(ang), jnp.sin(ang)


def _apply_rope(x, cos, sin):
    shape = (1, cos.shape[0]) + (1,) * (x.ndim - 3) + (cos.shape[1],)
    cos = cos.reshape(shape)
    sin = sin.reshape(shape)
    x32 = x.astype(F32)
    x1, x2 = x32[..., :HEAD_DIM // 2], x32[..., HEAD_DIM // 2:]
    return jnp.concatenate([x1 * cos - x2 * sin, x1 * sin + x2 * cos], axis=-1).astype(x.dtype)


def _softmax_attend(q, k, v):
    s = jnp.einsum('bqhd,bkhd->bhqk', q, k).astype(F32) * (q.shape[-1] ** -0.5)
    p = jax.nn.softmax(s, axis=-1).astype(v.dtype)
    return jnp.einsum('bhqk,bkhe->bqhe', p, v)


def _neighbourhood_attention(q, k, v, kc, vc, rpb):
    B, S, H, Dh = q.shape
    rows = S // GRID_W
    kr = min(WIN_R, rows)
    kw = min(WIN_C, GRID_W)
    qg = q.reshape(B, rows, GRID_W, H, Dh)
    kg = k.reshape(B, rows, GRID_W, H, Dh)
    vg = v.reshape(B, rows, GRID_W, H, Dh)
    row_start = jnp.clip(jnp.arange(rows) - kr // 2, 0, rows - kr)
    col = jnp.arange(GRID_W)
    col_start = jnp.clip(col - kw // 2, 0, GRID_W - kw)
    col_ok = (col[None, :] >= col_start[:, None]) & (col[None, :] < col_start[:, None] + kw)
    dcol_idx = jnp.clip(col[None, :] - col[:, None] + WIN_C - 1, 0, 2 * WIN_C - 2)
    rpb_cols = rpb.astype(F32)[:, :, dcol_idx]
    scale = Dh ** -0.5
    n_loc = kr * GRID_W

    def row_block(r):
        rs = row_start[r]
        q_r = lax.dynamic_index_in_dim(qg, r, axis=1, keepdims=False)
        k_b = lax.dynamic_slice_in_dim(kg, rs, kr, axis=1)
        v_b = lax.dynamic_slice_in_dim(vg, rs, kr, axis=1)
        bias = lax.dynamic_slice_in_dim(rpb_cols, rs - r + WIN_R - 1, kr, axis=1)
        bias = jnp.transpose(bias, (0, 2, 1, 3))
        s_loc = jnp.einsum('bqhd,brwhd->bhqrw', q_r, k_b).astype(F32) * scale + bias[None]
        s_loc = jnp.where(col_ok[None, None, :, None, :], s_loc, -jnp.inf)
        s_ctx = jnp.einsum('bqhd,bchd->bhqc', q_r, kc).astype(F32) * scale
        s = jnp.concatenate([s_loc.reshape(B, H, GRID_W, n_loc), s_ctx], axis=-1)
        p = jax.nn.softmax(s, axis=-1).astype(v.dtype)
        o = jnp.einsum('bhqk,bkhd->bqhd', p[..., :n_loc], v_b.reshape(B, n_loc, H, Dh))
        return o + jnp.einsum('bhqc,bchd->bqhd', p[..., n_loc:], vc)

    out = lax.map(row_block, jnp.arange(rows))
    return jnp.moveaxis(out, 0, 1).reshape(B, S, H, Dh)


def _short_conv(u, w, b):
    L = u.shape[1]
    pad = SHORT_CONV // 2
    up = jnp.pad(u, ((0, 0), (pad, SHORT_CONV - 1 - pad), (0, 0)))
    y = b
    for j in range(SHORT_CONV):
        y = y + up[:, j:j + L] * w[j]
    return y


def _hyena_filters(L, w1, b1, freq, w2, b2, w3, b3):
    t = jnp.linspace(0.0, 1.0, L, dtype=F32)[:, None]
    w = (2.0 * math.pi / L) * jnp.arange(L, dtype=F32)[:, None]
    bands = (HY_EMB - 1) // 2
    fb = jnp.linspace(1e-4, bands - 1, bands, dtype=F32)[None, :]
    z = jnp.concatenate([t, jnp.cos(fb * w), -jnp.sin(fb * w)], axis=-1)
    fr = freq.astype(F32)
    h = jnp.sin(fr * (z @ w1.astype(F32) + b1.astype(F32)))
    h = jnp.sin(fr * (h @ w2.astype(F32) + b2.astype(F32)))
    h = h @ w3.astype(F32) + b3.astype(F32)
    min_decay = math.log(HY_TARGET) / HY_SLOW_DECAY
    max_decay = math.log(HY_TARGET) / HY_FAST_DECAY
    deltas = jnp.abs(jnp.linspace(min_decay, max_decay, HY_WIDTH, dtype=F32))
    decay = jnp.exp(-t * deltas[None, :])
    h = h.reshape(L, HY_ORDER, 2, HY_WIDTH) * decay[:, None, None, :]
    return h * lax.rsqrt(jnp.sum(h * h, axis=(0, 2), keepdims=True) + EPS)


def _bidir_long_conv(z, h_fwd, h_bwd, skip):
    L = z.shape[1]
    n = 2 * L
    Zf = jnp.fft.rfft(z, n=n, axis=1)
    Hf = jnp.fft.rfft(h_fwd, n=n, axis=0) + jnp.conj(jnp.fft.rfft(h_bwd, n=n, axis=0))
    y = jnp.fft.irfft(Zf * Hf[None], n=n, axis=1)[:, :L]
    return y + skip * z


def _hyena(u, conv_w, conv_b, filt, skip):
    u = _short_conv(u, conv_w, conv_b).astype(F32)
    v, x1, x2 = jnp.split(u, 3, axis=-1)
    h = _hyena_filters(u.shape[1], *filt)
    skip = skip.astype(F32)
    z = x1 * _bidir_long_conv(v, h[:, 0, 0], h[:, 0, 1], skip[0])
    z = x2 * _bidir_long_conv(z, h[:, 1, 0], h[:, 1, 1], skip[1])
    return z


def _diff_attend(q, k, v, lam):
    s = jnp.einsum('bqhmd,bkhmd->bhmqk', q, k).astype(F32) * (q.shape[-1] ** -0.5)
    p = jax.nn.softmax(s, axis=-1)
    a = p[:, :, 0] - lam * p[:, :, 1]
    return jnp.einsum('bhqk,bkhe->bqhe', a.astype(v.dtype), v)


def _diff_attention_latent(q, k, v, kc, vc, lam):
    B, S = q.shape[:2]
    k_all = jnp.concatenate([k, kc], axis=1)
    v_all = jnp.concatenate([v, vc], axis=1)
    nb = S // Q_BLOCK
    qb = jnp.moveaxis(q.reshape((B, nb, Q_BLOCK) + q.shape[2:]), 1, 0)
    o = lax.map(lambda qq: _diff_attend(qq, k_all, v_all, lam), qb)
    return jnp.moveaxis(o, 0, 1).reshape((B, S) + o.shape[3:])


def setup_inputs(seed: int = 0) -> dict:
    key = jax.random.key(seed)
    ks = jax.random.split(key, 32)
    nrm = jax.random.normal
    D = D_MODEL
    return {
        "x": nrm(ks[0], (BATCH, SEQ, D), F32),
        "c": nrm(ks[1], (BATCH, D), F32),
        "ctx": nrm(ks[2], (BATCH, CTX_LEN, D), F32),
        "c_ctx": nrm(ks[3], (D,), F32),
        "norm_g": 1.0 + 0.02 * nrm(ks[4], (DEPTH, D), F32),
        "w_mod": 0.5 * D ** -0.5 * nrm(ks[5], (DEPTH, D, 3 * D), F32),
        "b_mod": 0.02 * nrm(ks[6], (DEPTH, 3 * D), F32),
        "w_in": D ** -0.5 * nrm(ks[7], (DEPTH, D, IN_WIDTH), F32),
        "w_out": D ** -0.5 * nrm(ks[8], (DEPTH, D, D), F32),
        "q_norm_g": 1.0 + 0.02 * nrm(ks[9], (DEPTH, HEAD_DIM), F32),
        "k_norm_g": 1.0 + 0.02 * nrm(ks[10], (DEPTH, HEAD_DIM), F32),
        "na_rpb": 0.1 * nrm(ks[11], (N_EVEN, N_HEADS_NA, 2 * WIN_R - 1, 2 * WIN_C - 1), F32),
        "hy_conv_w": SHORT_CONV ** -0.5 * nrm(ks[12], (N_EVEN, SHORT_CONV, 3 * HY_WIDTH), F32),
        "hy_conv_b": 0.02 * nrm(ks[13], (N_EVEN, 3 * HY_WIDTH), F32),
        "hy_filt_w1": HY_EMB ** -0.5 * nrm(ks[14], (N_EVEN, HY_EMB, HY_HIDDEN), F32),
        "hy_filt_b1": 0.02 * nrm(ks[15], (N_EVEN, HY_HIDDEN), F32),
        "hy_filt_freq": 1.0 + 0.1 * nrm(ks[16], (N_EVEN, HY_HIDDEN), F32),
        "hy_filt_w2": HY_HIDDEN ** -0.5 * nrm(ks[17], (N_EVEN, HY_HIDDEN, HY_HIDDEN), F32),
        "hy_filt_b2": 0.02 * nrm(ks[18], (N_EVEN, HY_HIDDEN), F32),
        "hy_filt_w3": HY_HIDDEN ** -0.5 * nrm(ks[19], (N_EVEN, HY_HIDDEN, HY_ORDER * 2 * HY_WIDTH), F32),
        "hy_filt_b3": 0.02 * nrm(ks[20], (N_EVEN, HY_ORDER * 2 * HY_WIDTH), F32),
        "hy_skip": 0.5 * nrm(ks[21], (N_EVEN, HY_ORDER, HY_WIDTH), F32),
        "diff_lam_q1": 0.1 * nrm(ks[22], (N_ODD, HEAD_DIM), F32),
        "diff_lam_k1": 0.1 * nrm(ks[23], (N_ODD, HEAD_DIM), F32),
        "diff_lam_q2": 0.1 * nrm(ks[24], (N_ODD, HEAD_DIM), F32),
        "diff_lam_k2": 0.1 * nrm(ks[25], (N_ODD, HEAD_DIM), F32),
        "diff_subln_g": 1.0 + 0.02 * nrm(ks[26], (N_ODD, 2 * HEAD_DIM), F32),
    }


def reference(x, c, ctx, c_ctx, norm_g, w_mod, b_mod, w_in, w_out, q_norm_g, k_norm_g,
              na_rpb, hy_conv_w, hy_conv_b, hy_filt_w1, hy_filt_b1, hy_filt_freq, hy_filt_w2,
              hy_filt_b2, hy_filt_w3, hy_filt_b3, hy_skip,
              diff_lam_q1, diff_lam_k1, diff_lam_q2, diff_lam_k2, diff_subln_g):
    B, S, _ = x.shape
    n_ctx = ctx.shape[1]
    cos, sin = _axial_rope(S)
    xc = ctx
    for l in range(DEPTH):
        ctx_out = l < DEPTH - 1
        shift, scale, gate = _modulation(c, w_mod[l], b_mod[l], 3)
        h = _rmsnorm(x, norm_g[l]) * (1.0 + scale[:, None]) + shift[:, None]
        p = h @ w_in[l]
        mods_c = _modulation(c_ctx, w_mod[l], b_mod[l], 3 if ctx_out else 2)
        hc = _rmsnorm(xc, norm_g[l]) * (1.0 + mods_c[1]) + mods_c[0]
        if l % 2 == 0:
            e = l // 2
            qa = _rmsnorm(p[..., :NA_WIDTH].reshape(B, S, N_HEADS_NA, HEAD_DIM), q_norm_g[l])
            ka = _rmsnorm(p[..., NA_WIDTH:2 * NA_WIDTH].reshape(B, S, N_HEADS_NA, HEAD_DIM), k_norm_g[l])
            va = p[..., 2 * NA_WIDTH:3 * NA_WIDTH].reshape(B, S, N_HEADS_NA, HEAD_DIM)
            kv_lo, kv_hi = NA_WIDTH, 3 * NA_WIDTH
            pc = hc @ (w_in[l] if ctx_out else w_in[l][:, kv_lo:kv_hi])
            off = 0 if ctx_out else kv_lo
            kc = _rmsnorm(pc[..., kv_lo - off:kv_lo - off + NA_WIDTH].reshape(B, n_ctx, N_HEADS_NA, HEAD_DIM), k_norm_g[l])
            vc = pc[..., kv_lo - off + NA_WIDTH:kv_hi - off].reshape(B, n_ctx, N_HEADS_NA, HEAD_DIM)
            filt = (hy_filt_w1[e], hy_filt_b1[e], hy_filt_freq[e], hy_filt_w2[e], hy_filt_b2[e],
                    hy_filt_w3[e], hy_filt_b3[e])
            o_na = _neighbourhood_attention(qa, ka, va, kc, vc, na_rpb[e]).reshape(B, S, NA_WIDTH)
            o_hy = _hyena(p[..., HY_LO:HY_HI], hy_conv_w[e], hy_conv_b[e], filt, hy_skip[e]).astype(x.dtype)
            y = jnp.concatenate([o_na, o_hy], axis=-1) * jax.nn.silu(p[..., GATE_LO:])
            if ctx_out:
                qc = _rmsnorm(pc[..., :NA_WIDTH].reshape(B, n_ctx, N_HEADS_NA, HEAD_DIM), q_norm_g[l])
                oc_na = _softmax_attend(qc, kc, vc).reshape(B, n_ctx, NA_WIDTH)
                oc_hy = _hyena(pc[..., HY_LO:HY_HI], hy_conv_w[e], hy_conv_b[e], filt, hy_skip[e]).astype(xc.dtype)
                yc = jnp.concatenate([oc_na, oc_hy], axis=-1) * jax.nn.silu(pc[..., GATE_LO:])
        else:
            o_i = l // 2
            lam_init = 0.8 - 0.6 * math.exp(-0.3 * l)
            lam = (jnp.exp(jnp.sum(diff_lam_q1[o_i].astype(F32) * diff_lam_k1[o_i].astype(F32)))
                   - jnp.exp(jnp.sum(diff_lam_q2[o_i].astype(F32) * diff_lam_k2[o_i].astype(F32)))
                   + lam_init)
            q = _rmsnorm(p[..., :DIFF_QK].reshape(B, S, N_HEADS_DIFF, 2, HEAD_DIM), q_norm_g[l])
            k = _rmsnorm(p[..., DIFF_QK:2 * DIFF_QK].reshape(B, S, N_HEADS_DIFF, 2, HEAD_DIM), k_norm_g[l])
            q = _apply_rope(q, cos, sin)
            k = _apply_rope(k, cos, sin)
            v = p[..., 2 * DIFF_QK:3 * DIFF_QK].reshape(B, S, N_HEADS_DIFF, 2 * HEAD_DIM)
            kv_lo, kv_hi = DIFF_QK, 3 * DIFF_QK
            pc = hc @ (w_in[l] if ctx_out else w_in[l][:, kv_lo:kv_hi])
            off = 0 if ctx_out else kv_lo
            kc = _rmsnorm(pc[..., kv_lo - off:kv_lo - off + DIFF_QK].reshape(B, n_ctx, N_HEADS_DIFF, 2, HEAD_DIM), k_norm_g[l])
            vc = pc[..., kv_lo - off + DIFF_QK:kv_hi - off].reshape(B, n_ctx, N_HEADS_DIFF, 2 * HEAD_DIM)
            o = _diff_attention_latent(q, k, v, kc, vc, lam)
            o = (_rmsnorm(o, diff_subln_g[o_i]) * (1.0 - lam_init)).reshape(B, S, D_MODEL)
            y = o * jax.nn.silu(p[..., GATE_LO:])
            if ctx_out:
                qc = _rmsnorm(pc[..., :DIFF_QK].reshape(B, n_ctx, N_HEADS_DIFF, 2, HEAD_DIM), q_norm_g[l])
                oc = _diff_attend(qc, kc, vc, lam)
                oc = (_rmsnorm(oc, diff_subln_g[o_i]) * (1.0 - lam_init)).reshape(B, n_ctx, D_MODEL)
                yc = oc * jax.nn.silu(pc[..., GATE_LO:])
        x = x + gate[:, None] * (y @ w_out[l])
        if ctx_out:
            xc = xc + mods_c[2] * (yc @ w_out[l])
    return x
```

```python
import functools
import math

import numpy as np
import jax
import jax.numpy as jnp
from jax import lax
from jax.experimental import pallas as pl
from jax.experimental.pallas import tpu as pltpu

F32 = jnp.float32
BF16 = jnp.bfloat16
HIGHEST = lax.Precision.HIGHEST

D_MODEL = 1024
DEPTH = 4
GRID_W = 64
HEAD_DIM = 64
N_HEADS_NA = 8
NA_WIDTH = N_HEADS_NA * HEAD_DIM
HY_WIDTH = D_MODEL - NA_WIDTH
HY_ORDER = 2
HY_EMB = 33
HY_HIDDEN = 64
HY_FAST_DECAY = 0.3
HY_SLOW_DECAY = 1.5
HY_TARGET = 1e-2
WIN_R = 8
WIN_C = 16
N_HEADS_DIFF = D_MODEL // (2 * HEAD_DIM)
DIFF_QK = N_HEADS_DIFF * 2 * HEAD_DIM
IN_WIDTH = 4 * D_MODEL
EPS = 1e-6
ROPE_BASE = 10000.0

LANES = 128
MXU_DIM = 256
VMEM_LIMIT = 56 * 1024 * 1024
NEG_BIG = -1e30

NA_Q_ROWS = 4
NA_WIN_ROWS = 12
HY_LANES = 2 * LANES


def _cparams(sem):
    return pltpu.CompilerParams(dimension_semantics=sem, vmem_limit_bytes=VMEM_LIMIT)


def _single(shape, index_map):
    return pl.BlockSpec(shape, index_map, pipeline_mode=pl.Buffered(1))


def _dot(a, b):
    return jnp.dot(a, b, preferred_element_type=F32)


def _dot_nt(a, b):
    return lax.dot_general(a, b, (((1,), (1,)), ((), ())), preferred_element_type=F32)


def _mod_kernel(v_ref, w_ref, b_ref, o_ref):
    v = v_ref[...]
    a = v * jax.nn.sigmoid(v)
    o_ref[0] = jnp.dot(a, w_ref[0], precision=HIGHEST, preferred_element_type=F32) + b_ref[0]


def _modulation(vecs, w_mod, b_mod):
    rows = vecs.shape[0]
    tn = 1024
    return pl.pallas_call(
        _mod_kernel,
        grid=(DEPTH, 3 * D_MODEL // tn),
        in_specs=[pl.BlockSpec((rows, D_MODEL), lambda l, j: (0, 0)),
                  pl.BlockSpec((1, D_MODEL, tn), lambda l, j: (l, 0, j)),
                  pl.BlockSpec((1, 1, tn), lambda l, j: (l, 0, j))],
        out_specs=pl.BlockSpec((1, rows, tn), lambda l, j: (l, 0, j)),
        out_shape=jax.ShapeDtypeStruct((DEPTH, rows, 3 * D_MODEL), F32),
        compiler_params=_cparams(("arbitrary", "arbitrary")),
        name="modulation",
    )(vecs, w_mod, b_mod.reshape(DEPTH, 1, 3 * D_MODEL))


_EVEN_SECTIONS = (("q", 0, NA_WIDTH, "qnorm"), ("k", NA_WIDTH, 2 * NA_WIDTH, "knorm"),
                  ("v", 2 * NA_WIDTH, 3 * NA_WIDTH, "copy"),
                  ("hy", 3 * NA_WIDTH, 3 * NA_WIDTH + 3 * HY_WIDTH, "copy"),
                  ("gate", IN_WIDTH - D_MODEL, IN_WIDTH, "silu"))
_ODD_SECTIONS = (("q", 0, DIFF_QK, "qnorm"), ("k", DIFF_QK, 2 * DIFF_QK, "knorm"),
                 ("v", 2 * DIFF_QK, 3 * DIFF_QK, "copy"),
                 ("gate", IN_WIDTH - D_MODEL, IN_WIDTH, "silu"))


def _rope_chunk(a, cos, sin_signed, low_half):
    up = pltpu.roll(a, LANES - HEAD_DIM // 2, axis=1)
    dn = pltpu.roll(a, HEAD_DIM // 2, axis=1)
    return a * cos + jnp.where(low_half, up, dn) * sin_signed


def _inproj_kernel(*refs, sections, rope):
    x_ref, shift_ref, scale_ref, g_ref, w_ref, qg_ref, kg_ref, e_ref = refs[:8]
    pos = 8
    if rope:
        cos_ref, sin_ref = refs[8:10]
        pos = 10
    out_refs = refs[pos:]
    x = x_ref[0]
    ms = jnp.mean(x * x, axis=-1, keepdims=True)
    h = x * lax.rsqrt(ms + EPS) * g_ref[...] * (1.0 + scale_ref[0]) + shift_ref[0]
    hb = h.astype(BF16)
    if rope:
        cos = cos_ref[...]
        sin_signed = sin_ref[...]
        lane = lax.broadcasted_iota(jnp.int32, cos.shape, 1)
        low_half = (lane % HEAD_DIM) < HEAD_DIM // 2
    for o_ref, (_, lo, hi, kind) in zip(out_refs, sections):
        for c0 in range(lo, hi, MXU_DIM):
            acc = _dot(hb, w_ref[:, c0:c0 + MXU_DIM])
            if kind in ("qnorm", "knorm"):
                gain = qg_ref[...] if kind == "qnorm" else kg_ref[...]
                ss = _dot((acc * acc).astype(BF16), e_ref[...])
                acc = acc * lax.rsqrt(ss * (1.0 / HEAD_DIM) + EPS) * gain
                if rope:
                    acc = jnp.concatenate(
                        [_rope_chunk(acc[:, j:j + LANES], cos, sin_signed, low_half)
                         for j in range(0, MXU_DIM, LANES)], axis=1)
            elif kind == "silu":
                acc = acc * jax.nn.sigmoid(acc)
            o_ref[0, :, c0 - lo:c0 - lo + MXU_DIM] = acc.astype(o_ref.dtype)


def _rope_tables(n_tokens):
    t = jnp.arange(n_tokens, dtype=jnp.int32)
    row = (t // GRID_W).astype(F32)
    col = (t % GRID_W).astype(F32)
    n_freq = HEAD_DIM // 4
    inv = ROPE_BASE ** (-jnp.arange(n_freq, dtype=F32) / n_freq)
    ang = jnp.concatenate([row[:, None] * inv, col[:, None] * inv], axis=-1)
    cos, sin = jnp.cos(ang), jnp.sin(ang)
    cos_t = jnp.concatenate([cos, cos, cos, cos], axis=-1)
    sin_t = jnp.concatenate([-sin, sin, -sin, sin], axis=-1)
    return cos_t, sin_t


def _inproj(x, shift, scale, g, w, qg, kg, *, odd, rope_tabs=None):
    B, S, _ = x.shape
    tm = min(S, 512)
    sections = _ODD_SECTIONS if odd else _EVEN_SECTIONS
    rope = rope_tabs is not None
    head = jnp.arange(MXU_DIM) // HEAD_DIM
    e = (head[:, None] == head[None, :]).astype(BF16)
    qg_t = jnp.tile(qg.astype(F32), MXU_DIM // HEAD_DIM)[None] * (HEAD_DIM ** -0.5)
    kg_t = jnp.tile(kg.astype(F32), MXU_DIM // HEAD_DIM)[None]
    const = lambda b, i: (0, 0)
    in_specs = [pl.BlockSpec((1, tm, D_MODEL), lambda b, i: (b, i, 0)),
                pl.BlockSpec((1, 1, D_MODEL), lambda b, i: (b, 0, 0)),
                pl.BlockSpec((1, 1, D_MODEL), lambda b, i: (b, 0, 0)),
                pl.BlockSpec((1, D_MODEL), const),
                _single((D_MODEL, IN_WIDTH), const),
                pl.BlockSpec((1, MXU_DIM), const),
                pl.BlockSpec((1, MXU_DIM), const),
                pl.BlockSpec((MXU_DIM, MXU_DIM), const)]
    args = [x, shift[:, None], scale[:, None], g[None], w, qg_t, kg_t, e]
    if rope:
        in_specs += [pl.BlockSpec((tm, LANES), lambda b, i: (i, 0))] * 2
        args += list(rope_tabs)
    out_shape = [jax.ShapeDtypeStruct((B, S, hi - lo), BF16) for _, lo, hi, _ in sections]
    out_specs = [pl.BlockSpec((1, tm, hi - lo), lambda b, i: (b, i, 0)) for _, lo, hi, _ in sections]
    return pl.pallas_call(
        functools.partial(_inproj_kernel, sections=sections, rope=rope),
        grid=(B, S // tm),
        in_specs=in_specs, out_specs=out_specs, out_shape=out_shape,
        compiler_params=_cparams(("arbitrary", "arbitrary")),
        name="inproj_odd" if odd else "inproj_even",
    )(*args)


def _outproj_kernel(*refs, n_parts):
    x_ref, gm_ref, gate_ref, w_ref = refs[:4]
    parts = refs[4:4 + n_parts]
    o_ref = refs[4 + n_parts]
    y = jnp.concatenate([p[0].astype(F32) for p in parts], axis=1) * gate_ref[0].astype(F32)
    o_ref[0] = x_ref[0] + gm_ref[0] * _dot(y.astype(BF16), w_ref[...])


def _outproj(x, gate_mod, gate, w, parts):
    B, S, _ = x.shape
    tm = min(S, 512)
    in_specs = [pl.BlockSpec((1, tm, D_MODEL), lambda b, i: (b, i, 0)),
                pl.BlockSpec((1, 1, D_MODEL), lambda b, i: (b, 0, 0)),
                pl.BlockSpec((1, tm, D_MODEL), lambda b, i: (b, i, 0)),
                _single((D_MODEL, D_MODEL), lambda b, i: (0, 0))]
    in_specs += [pl.BlockSpec((1, tm, p.shape[-1]), lambda b, i: (b, i, 0)) for p in parts]
    return pl.pallas_call(
        functools.partial(_outproj_kernel, n_parts=len(parts)),
        grid=(B, S // tm),
        in_specs=in_specs,
        out_specs=pl.BlockSpec((1, tm, D_MODEL), lambda b, i: (b, i, 0)),
        out_shape=jax.ShapeDtypeStruct((B, S, D_MODEL), F32),
        compiler_params=_cparams(("arbitrary", "arbitrary")),
        name="outproj",
    )(x, gate_mod[:, None], gate, w, *parts)


def _head_masks(shape):
    lane = lax.broadcasted_iota(jnp.int32, shape, 1)
    return lane < HEAD_DIM, lane >= HEAD_DIM


def _na_kernel(q_ref, k_ref, v_ref, kc_ref, vc_ref, bias_ref, o_ref, *, n_rows):
    i = pl.program_id(2)
    start = jnp.clip(i * NA_Q_ROWS - WIN_R // 2, 0, n_rows - NA_WIN_ROWS)
    off = pl.multiple_of(start * GRID_W, GRID_W)
    q = q_ref[0]
    kw = k_ref[0, pl.ds(off, NA_WIN_ROWS * GRID_W), :]
    vw = v_ref[0, pl.ds(off, NA_WIN_ROWS * GRID_W), :]
    kc = kc_ref[0]
    vc = vc_ref[0]
    outs = []
    for hh, sel in enumerate(_head_masks(q.shape)):
        qh = jnp.where(sel, q, jnp.zeros_like(q))
        s_loc = _dot_nt(qh, kw) + bias_ref[hh, 0]
        s_ctx = _dot_nt(qh, kc)
        m = jnp.maximum(jnp.max(s_loc, axis=-1, keepdims=True), jnp.max(s_ctx, axis=-1, keepdims=True))
        p_loc = jnp.exp(s_loc - m)
        p_ctx = jnp.exp(s_ctx - m)
        l = jnp.sum(p_loc, axis=-1, keepdims=True) + jnp.sum(p_ctx, axis=-1, keepdims=True)
        o = _dot(p_loc.astype(BF16), vw) + _dot(p_ctx.astype(BF16), vc)
        outs.append(o / l)
    lo_half, _ = _head_masks(outs[0].shape)
    o_ref[0] = jnp.where(lo_half, outs[0], outs[1]).astype(o_ref.dtype)


def _na_bias_table(rpb, n_rows):
    n_blk = n_rows // NA_Q_ROWS
    kr = min(WIN_R, n_rows)
    c = jnp.arange(GRID_W)
    cs = jnp.clip(c - WIN_C // 2, 0, GRID_W - WIN_C)
    col_ok = (c[None, :] >= cs[:, None]) & (c[None, :] < cs[:, None] + WIN_C)
    dcol = jnp.clip(c[None, :] - c[:, None] + WIN_C - 1, 0, 2 * WIN_C - 2)
    tabs = []
    for i in (0, 1, n_blk - 1):
        r0 = i * NA_Q_ROWS
        start = min(max(r0 - WIN_R // 2, 0), n_rows - NA_WIN_ROWS)
        r = r0 + jnp.arange(NA_Q_ROWS)
        rr = start + jnp.arange(NA_WIN_ROWS)
        rs = jnp.clip(r - kr // 2, 0, n_rows - kr)
        row_ok = (rr[None, :] >= rs[:, None]) & (rr[None, :] < rs[:, None] + kr)
        drow = jnp.clip(rr[None, :] - r[:, None] + WIN_R - 1, 0, 2 * WIN_R - 2)
        b = rpb.astype(F32)[:, drow[:, None, :, None], dcol[None, :, None, :]]
        ok = row_ok[:, None, :, None] & col_ok[None, :, None, :]
        b = jnp.where(ok[None], b, NEG_BIG)
        tabs.append(b.reshape(rpb.shape[0], NA_Q_ROWS * GRID_W, NA_WIN_ROWS * GRID_W))
    return jnp.stack(tabs, axis=1)


def _na_attention(q, k, v, kc, vc, rpb):
    B, S, _ = q.shape
    n_ctx = kc.shape[1]
    n_rows = S // GRID_W
    n_blk = n_rows // NA_Q_ROWS
    assert n_rows % NA_Q_ROWS == 0 and n_rows >= NA_WIN_ROWS + NA_Q_ROWS
    tq = NA_Q_ROWS * GRID_W
    tk = NA_WIN_ROWS * GRID_W
    bias = _na_bias_table(rpb, n_rows)

    def bias_map(b, hp, i):
        return (hp, jnp.where(i == 0, 0, jnp.where(i == n_blk - 1, 2, 1)), 0, 0)

    return pl.pallas_call(
        functools.partial(_na_kernel, n_rows=n_rows),
        grid=(B, NA_WIDTH // LANES, n_blk),
        in_specs=[pl.BlockSpec((1, tq, LANES), lambda b, hp, i: (b, i, hp)),
                  pl.BlockSpec((1, S, LANES), lambda b, hp, i: (b, 0, hp)),
                  pl.BlockSpec((1, S, LANES), lambda b, hp, i: (b, 0, hp)),
                  pl.BlockSpec((1, n_ctx, LANES), lambda b, hp, i: (b, 0, hp)),
                  pl.BlockSpec((1, n_ctx, LANES), lambda b, hp, i: (b, 0, hp)),
                  pl.BlockSpec((2, 1, tq, tk), bias_map)],
        out_specs=pl.BlockSpec((1, tq, LANES), lambda b, hp, i: (b, i, hp)),
        out_shape=jax.ShapeDtypeStruct((B, S, NA_WIDTH), BF16),
        compiler_params=_cparams(("arbitrary", "arbitrary", "arbitrary")),
        name="na_attention",
    )(q, k, v, kc, vc, bias)


def _ctx_attn_kernel(q_ref, k_ref, v_ref, o_ref):
    q = q_ref[0]
    k = k_ref[0]
    v = v_ref[0]
    outs = []
    for sel in _head_masks(q.shape):
        qh = jnp.where(sel, q, jnp.zeros_like(q))
        s = _dot_nt(qh, k)
        p = jnp.exp(s - jnp.max(s, axis=-1, keepdims=True))
        outs.append(_dot(p.astype(BF16), v) / jnp.sum(p, axis=-1, keepdims=True))
    lo_half, _ = _head_masks(outs[0].shape)
    o_ref[0] = jnp.where(lo_half, outs[0], outs[1]).astype(o_ref.dtype)


def _ctx_attention(q, k, v):
    B, n, width = q.shape
    spec = pl.BlockSpec((1, n, LANES), lambda b, hp: (b, 0, hp))
    return pl.pallas_call(
        _ctx_attn_kernel,
        grid=(B, width // LANES),
        in_specs=[spec, spec, spec], out_specs=spec,
        out_shape=jax.ShapeDtypeStruct((B, n, width), BF16),
        compiler_params=_cparams(("arbitrary", "arbitrary")),
        name="ctx_attention",
    )(q, k, v)


def _diff_kernel(*refs, tk, n_main, has_extra, lam_init):
    lq1_ref, lk1_ref, lq2_ref, lk2_ref, sg_ref, q_ref, k_ref, v_ref = refs[:8]
    pos = 8
    if has_extra:
        ke_ref, ve_ref = refs[8:10]
        pos = 10
    o_ref, m_sc, l_sc, acc_sc = refs[pos:]
    q = q_ref[0]
    qs = [jnp.where(sel, q, jnp.zeros_like(q)) for sel in _head_masks(q.shape)]
    m_sc[...] = jnp.full(m_sc.shape, NEG_BIG, F32)
    l_sc[...] = jnp.zeros(l_sc.shape, F32)
    acc_sc[...] = jnp.zeros(acc_sc.shape, F32)

    def chunk(kb, vb):
        for mi in range(2):
            s = _dot_nt(qs[mi], kb)
            m_old = m_sc[mi]
            m_new = jnp.maximum(m_old, jnp.max(s, axis=-1, keepdims=True))
            p = jnp.exp(s - m_new)
            alpha = jnp.exp(m_old - m_new)
            l_sc[mi] = alpha * l_sc[mi] + jnp.sum(p, axis=-1, keepdims=True)
            acc_sc[mi] = alpha * acc_sc[mi] + _dot(p.astype(BF16), vb)
            m_sc[mi] = m_new

    def body(j, carry):
        off = pl.multiple_of(j * tk, tk)
        chunk(k_ref[0, pl.ds(off, tk), :], v_ref[0, pl.ds(off, tk), :])
        return carry

    lax.fori_loop(0, n_main, body, 0)
    if has_extra:
        chunk(ke_ref[0], ve_ref[0])

    lam = (jnp.exp(jnp.sum(lq1_ref[...] * lk1_ref[...], axis=-1, keepdims=True))
           - jnp.exp(jnp.sum(lq2_ref[...] * lk2_ref[...], axis=-1, keepdims=True)) + lam_init)
    o = acc_sc[0] / l_sc[0] - lam * (acc_sc[1] / l_sc[1])
    o = o * lax.rsqrt(jnp.mean(o * o, axis=-1, keepdims=True) + EPS) * sg_ref[...]
    o_ref[0] = (o * (1.0 - lam_init)).astype(o_ref.dtype)


def _diff_attention(q, k, v, lam_vecs, subln_g, lam_init, k_extra=None, v_extra=None):
    B, sq, width = q.shape
    sk = k.shape[1]
    tq = min(sq, 256)
    tk = min(sk, 512)
    has_extra = k_extra is not None
    small = pl.BlockSpec((1, HEAD_DIM), lambda b, h, i: (0, 0))
    in_specs = [small] * 4 + [
        pl.BlockSpec((1, LANES), lambda b, h, i: (0, 0)),
        pl.BlockSpec((1, tq, LANES), lambda b, h, i: (b, i, h)),
        pl.BlockSpec((1, sk, LANES), lambda b, h, i: (b, 0, h)),
        pl.BlockSpec((1, sk, LANES), lambda b, h, i: (b, 0, h))]
    args = [a.astype(F32)[None] for a in lam_vecs] + [subln_g.astype(F32)[None], q, k, v]
    if has_extra:
        se = k_extra.shape[1]
        in_specs += [pl.BlockSpec((1, se, LANES), lambda b, h, i: (b, 0, h))] * 2
        args += [k_extra, v_extra]
    return pl.pallas_call(
        functools.partial(_diff_kernel, tk=tk, n_main=sk // tk, has_extra=has_extra, lam_init=lam_init),
        grid=(B, width // LANES, sq // tq),
        in_specs=in_specs,
        out_specs=pl.BlockSpec((1, tq, LANES), lambda b, h, i: (b, i, h)),
        out_shape=jax.ShapeDtypeStruct((B, sq, width), BF16),
        scratch_shapes=[pltpu.VMEM((2, tq, 1), F32), pltpu.VMEM((2, tq, 1), F32),
                        pltpu.VMEM((2, tq, LANES), F32)],
        compiler_params=_cparams(("arbitrary", "arbitrary", "arbitrary")),
        name="diff_attention",
    )(*args)


def _filter_kernel(z_ref, w1_ref, b1_ref, fr_ref, w2_ref, b2_ref, w3_ref, b3_ref, dec_ref, h_ref, ss_ref):
    hdot = functools.partial(jnp.dot, precision=HIGHEST, preferred_element_type=F32)
    fr = fr_ref[...]
    h = jnp.sin(fr * (hdot(z_ref[...], w1_ref[...]) + b1_ref[...]))
    h = jnp.sin(fr * (hdot(h, w2_ref[...]) + b2_ref[...]))
    h = hdot(h, w3_ref[...]) + b3_ref[...]
    dec = dec_ref[...]
    h = h * jnp.concatenate([dec] * (2 * HY_ORDER), axis=1)
    h_ref[...] = h

    @pl.when(pl.program_id(0) == 0)
    def _():
        ss_ref[...] = jnp.zeros(ss_ref.shape, F32)

    ss_ref[...] += jnp.sum(h * h, axis=0, keepdims=True)


def _hyena_filters(L, w1, b1, freq, w2, b2, w3, b3):
    t = jnp.linspace(0.0, 1.0, L, dtype=F32)[:, None]
    w = (2.0 * math.pi / L) * jnp.arange(L, dtype=F32)[:, None]
    bands = (HY_EMB - 1) // 2
    fb = jnp.linspace(1e-4, bands - 1, bands, dtype=F32)[None, :]
    z = jnp.concatenate([t, jnp.cos(fb * w), -jnp.sin(fb * w)], axis=-1)
    emb_pad = HY_HIDDEN - HY_EMB
    z = jnp.pad(z, ((0, 0), (0, emb_pad)))
    w1p = jnp.pad(w1.astype(F32), ((0, emb_pad), (0, 0)))
    min_decay = math.log(HY_TARGET) / HY_SLOW_DECAY
    max_decay = math.log(HY_TARGET) / HY_FAST_DECAY
    deltas = jnp.abs(jnp.linspace(min_decay, max_decay, HY_WIDTH, dtype=F32))
    decay = jnp.exp(-t * deltas[None, :])
    tl = min(L, 512)
    width = HY_ORDER * 2 * HY_WIDTH
    const = lambda i: (0, 0)
    return pl.pallas_call(
        _filter_kernel,
        grid=(L // tl,),
        in_specs=[pl.BlockSpec((tl, HY_HIDDEN), lambda i: (i, 0)),
                  pl.BlockSpec((HY_HIDDEN, HY_HIDDEN), const), pl.BlockSpec((1, HY_HIDDEN), const),
                  pl.BlockSpec((1, HY_HIDDEN), const),
                  pl.BlockSpec((HY_HIDDEN, HY_HIDDEN), const), pl.BlockSpec((1, HY_HIDDEN), const),
                  pl.BlockSpec((HY_HIDDEN, width), const), pl.BlockSpec((1, width), const),
                  pl.BlockSpec((tl, HY_WIDTH), lambda i: (i, 0))],
        out_specs=[pl.BlockSpec((tl, width), lambda i: (i, 0)), pl.BlockSpec((1, width), const)],
        out_shape=[jax.ShapeDtypeStruct((L, width), F32), jax.ShapeDtypeStruct((1, width), F32)],
        compiler_params=_cparams(("arbitrary",)),
        name="hyena_filters",
    )(z, w1p, b1.astype(F32)[None], freq.astype(F32)[None], w2.astype(F32), b2.astype(F32)[None],
      w3.astype(F32), b3.astype(F32)[None], decay)


def _fft_sizes(L):
    n2 = 128 if L >= 1024 else 32
    n1 = 2 * L // n2
    return n1, n2


@functools.lru_cache(maxsize=None)
def _dft_tables(L):
    n1s, n2s = _fft_sizes(L)
    N = 2 * L
    k1 = np.arange(n1s)
    n1 = np.arange(n1s // 2)
    n2 = np.arange(n2s)
    m = (k1[None, :, None] * (n2s * n1[None, None, :] + n2[:, None, None])) % N
    th = 2.0 * np.pi * m / N
    c, s = np.cos(th), np.sin(th)
    l1 = np.concatenate([np.concatenate([c, s], axis=2), np.concatenate([-s, c], axis=2)], axis=1)
    ct, st = np.swapaxes(c, 1, 2), np.swapaxes(s, 1, 2)
    l1i = np.concatenate([np.concatenate([ct, -st], axis=2), np.concatenate([st, ct], axis=2)], axis=1)
    th3 = 2.0 * np.pi * ((n2[:, None] * n2[None, :]) % n2s) / n2s
    c3, s3 = np.cos(th3), np.sin(th3)
    l3 = np.block([[c3, s3], [-s3, c3]])
    l3i = np.block([[c3, -s3], [s3, c3]])
    return tuple(np.asarray(a, np.float32) for a in (l1, l3, l3i, l1i))


def _ld2(buf, rows):
    return jnp.concatenate([buf[0, rows, :], buf[1, rows, :]], axis=1)


def _st2(buf, rows, val):
    buf[0, rows, :] = val[:, :LANES]
    buf[1, rows, :] = val[:, LANES:]


def _fft_stage1(zr, zi, ar, ai, l1_ref, n1s, n2s):
    half = n1s // 2

    def body(n2, carry):
        src = pl.ds(n2, half, stride=n2s)
        d = jnp.concatenate([_ld2(zr, src), _ld2(zi, src)], axis=0)
        out = _dot(l1_ref[n2], d.astype(BF16))
        dst = pl.ds(n2, n1s, stride=n2s)
        _st2(ar, dst, out[:n1s])
        _st2(ai, dst, out[n1s:])
        return carry

    lax.fori_loop(0, n2s, body, 0)


def _fft_stage3(ar, ai, l3_ref, k1, n2s):
    r = pl.multiple_of(k1 * n2s, n2s)
    d = jnp.concatenate([_ld2(ar, pl.ds(r, n2s)), _ld2(ai, pl.ds(r, n2s))], axis=0)
    return r, _dot(l3_ref[...], d.astype(BF16))


def _spectrum_kernel(hf_ref, hb_ref, ssf_ref, ssb_ref, l1_ref, l3_ref, gr_ref, gi_ref,
                     zr, zi, ar, ai, *, n1s, n2s):
    nrm = lax.rsqrt(ssf_ref[...] + ssb_ref[...] + EPS)
    hf = hf_ref[...] * nrm
    hb = hb_ref[...] * nrm
    zr[0] = hf + hb
    zr[1] = hf - hb
    zi[...] = jnp.zeros(zi.shape, F32)
    _fft_stage1(zr, zi, ar, ai, l1_ref, n1s, n2s)
    inv_n = 1.0 / (n1s * n2s)

    def body(k1, carry):
        r, z = _fft_stage3(ar, ai, l3_ref, k1, n2s)
        gr_ref[0, pl.ds(r, n2s), :] = z[:n2s, :LANES] * inv_n
        gi_ref[0, pl.ds(r, n2s), :] = z[n2s:, LANES:] * inv_n
        return carry

    lax.fori_loop(0, n1s, body, 0)


def _hyena_spectra(h, ss):
    L = h.shape[0]
    n1s, n2s = _fft_sizes(L)
    N = 2 * L
    l1, l3, _, _ = _dft_tables(L)
    n_ct = HY_WIDTH // LANES
    fwd = lambda o, ct: (0, o * 2 * n_ct + ct)
    bwd = lambda o, ct: (0, o * 2 * n_ct + n_ct + ct)
    out_spec = pl.BlockSpec((1, N, LANES), lambda o, ct: (o, 0, ct))
    return pl.pallas_call(
        functools.partial(_spectrum_kernel, n1s=n1s, n2s=n2s),
        grid=(HY_ORDER, n_ct),
        in_specs=[pl.BlockSpec((L, LANES), fwd), pl.BlockSpec((L, LANES), bwd),
                  pl.BlockSpec((1, LANES), fwd), pl.BlockSpec((1, LANES), bwd),
                  _single(l1.shape, lambda o, ct: (0, 0, 0)), _single(l3.shape, lambda o, ct: (0, 0))],
        out_specs=[out_spec, out_spec],
        out_shape=[jax.ShapeDtypeStruct((HY_ORDER, N, HY_WIDTH), F32)] * 2,
        scratch_shapes=[pltpu.VMEM((2, L, LANES), F32), pltpu.VMEM((2, L, LANES), F32),
                        pltpu.VMEM((2, N, LANES), F32), pltpu.VMEM((2, N, LANES), F32)],
        compiler_params=_cparams(("arbitrary", "arbitrary")),
        name="hyena_spectra",
    )(h, h, ss, ss, jnp.asarray(l1, BF16), jnp.asarray(l3, BF16))


def _short_conv(u, w_ref, b_ref):
    L = u.shape[0]
    row = lax.broadcasted_iota(jnp.int32, u.shape, 0)
    prev = jnp.where(row == 0, 0.0, pltpu.roll(u, 1, axis=0))
    nxt = jnp.where(row == L - 1, 0.0, pltpu.roll(u, L - 1, axis=0))
    return b_ref[...] + prev * w_ref[0:1, :] + u * w_ref[1:2, :] + nxt * w_ref[2:3, :]


def _conv_kernel(a_ref, x_ref, wa_ref, ba_ref, wx_ref, bx_ref, gr_ref, gi_ref, skip_ref,
                 l1_ref, l3_ref, l3i_ref, l1i_ref, o_ref, zr, zi, ar, ai, *, conv_a, n1s, n2s):
    half = n1s // 2
    slots = ((zr, 0), (zr, 1), (zi, 0), (zi, 1))
    for s, (buf, hi) in enumerate(slots):
        u = a_ref[s].astype(F32)
        buf[hi] = _short_conv(u, wa_ref, ba_ref) if conv_a else u
    _fft_stage1(zr, zi, ar, ai, l1_ref, n1s, n2s)

    def mid(k1, carry):
        r, z = _fft_stage3(ar, ai, l3_ref, k1, n2s)
        g_r = gr_ref[0, pl.ds(r, n2s), :]
        g_i = gi_ref[0, pl.ds(r, n2s), :]
        g_r = jnp.concatenate([g_r, g_r], axis=1)
        g_i = jnp.concatenate([g_i, g_i], axis=1)
        z_r, z_i = z[:n2s], z[n2s:]
        p = jnp.concatenate([z_r * g_r - z_i * g_i, z_r * g_i + z_i * g_r], axis=0)
        b = _dot(l3i_ref[...], p.astype(BF16))
        _st2(ar, pl.ds(r, n2s), b[:n2s])
        _st2(ai, pl.ds(r, n2s), b[n2s:])
        return carry

    lax.fori_loop(0, n1s, mid, 0)
    skip = skip_ref[0]
    skip = jnp.concatenate([skip, skip], axis=1)

    def last(n2, carry):
        src = pl.ds(n2, n1s, stride=n2s)
        d = jnp.concatenate([_ld2(ar, src), _ld2(ai, src)], axis=0)
        y = _dot(l1i_ref[n2], d.astype(BF16))
        rows = pl.ds(n2, half, stride=n2s)
        _st2(zr, rows, y[:half] + skip * _ld2(zr, rows))
        _st2(zi, rows, y[half:] + skip * _ld2(zi, rows))
        return carry

    lax.fori_loop(0, n2s, last, 0)
    for s, (buf, hi) in enumerate(slots):
        xg = _short_conv(x_ref[s].astype(F32), wx_ref, bx_ref)
        o_ref[s] = (xg * buf[hi]).astype(o_ref.dtype)


def _hyena_conv(a, a_col, x, x_col, conv_w, conv_b, wa_col, wx_col, g_r, g_i, skip, order, *, conv_a, out_dtype):
    B, L, _ = a.shape
    n1s, n2s = _fft_sizes(L)
    N = 2 * L
    n_ct = HY_WIDTH // LANES
    seqs = 4
    assert B % seqs == 0
    tables = _dft_tables(L)
    t_specs = [_single(t.shape, (lambda b, ct: (0, 0, 0)) if t.ndim == 3 else (lambda b, ct: (0, 0))) for t in tables]
    return pl.pallas_call(
        functools.partial(_conv_kernel, conv_a=conv_a, n1s=n1s, n2s=n2s),
        grid=(n_ct, B // seqs),
        in_specs=[_single((seqs, L, LANES), lambda ct, b: (b, 0, a_col + ct)),
                  _single((seqs, L, LANES), lambda ct, b: (b, 0, x_col + ct)),
                  pl.BlockSpec((3, LANES), lambda ct, b: (0, wa_col + ct)),
                  pl.BlockSpec((1, LANES), lambda ct, b: (0, wa_col + ct)),
                  pl.BlockSpec((3, LANES), lambda ct, b: (0, wx_col + ct)),
                  pl.BlockSpec((1, LANES), lambda ct, b: (0, wx_col + ct)),
                  _single((1, N, LANES), lambda ct, b: (order, 0, ct)),
                  _single((1, N, LANES), lambda ct, b: (order, 0, ct)),
                  pl.BlockSpec((1, 1, LANES), lambda ct, b: (order, 0, ct))] + t_specs,
        out_specs=_single((seqs, L, LANES), lambda ct, b: (b, 0, ct)),
        out_shape=jax.ShapeDtypeStruct((B, L, HY_WIDTH), out_dtype),
        scratch_shapes=[pltpu.VMEM((2, L, LANES), F32), pltpu.VMEM((2, L, LANES), F32),
                        pltpu.VMEM((2, N, LANES), F32), pltpu.VMEM((2, N, LANES), F32)],
        compiler_params=_cparams(("arbitrary", "arbitrary")),
        name="hyena_conv",
    )(a, x, conv_w, conv_b[None], conv_w, conv_b[None], g_r, g_i, skip[:, None],
      *[jnp.asarray(t, BF16) for t in tables])


def _hyena(hy, conv_w, conv_b, filt, skip):
    L = hy.shape[1]
    n_ct = HY_WIDTH // LANES
    h, ss = _hyena_filters(L, *filt)
    g_r, g_i = _hyena_spectra(h, ss)
    conv_w = conv_w.astype(F32)
    conv_b = conv_b.astype(F32)
    skip = skip.astype(F32)
    z1 = _hyena_conv(hy, 0, hy, n_ct, conv_w, conv_b, 0, n_ct, g_r, g_i, skip, 0, conv_a=True, out_dtype=F32)
    return _hyena_conv(z1, 0, hy, 2 * n_ct, conv_w, conv_b, 0, 2 * n_ct, g_r, g_i, skip, 1,
                       conv_a=False, out_dtype=BF16)


def kernel(x, c, ctx, c_ctx, norm_g, w_mod, b_mod, w_in, w_out, q_norm_g, k_norm_g, na_rpb, hy_conv_w, hy_conv_b, hy_filt_w1, hy_filt_b1, hy_filt_freq, hy_filt_w2, hy_filt_b2, hy_filt_w3, hy_filt_b3, hy_skip, diff_lam_q1, diff_lam_k1, diff_lam_q2, diff_lam_k2, diff_subln_g):
    B, S, D = x.shape
    mod_rows = -(-(B + 1) // 8) * 8
    vecs = jnp.concatenate([c, c_ctx[None], jnp.zeros((mod_rows - B - 1, D), F32)], axis=0)
    mods = _modulation(vecs, w_mod, b_mod)
    w_in_b = w_in.astype(BF16)
    w_out_b = w_out.astype(BF16)
    rope_tabs = _rope_tables(S)
    xc = ctx
    for l in range(DEPTH):
        ctx_out = l < DEPTH - 1
        shift, scale, gate = (mods[l, :B, j * D:(j + 1) * D] for j in range(3))
        c_shift, c_scale, c_gate = (jnp.broadcast_to(mods[l, B:B + 1, j * D:(j + 1) * D], (B, D)) for j in range(3))
        odd = l % 2 == 1
        lat = _inproj(x, shift, scale, norm_g[l], w_in_b[l], q_norm_g[l], k_norm_g[l], odd=odd,
                      rope_tabs=rope_tabs if odd else None)
        cx = _inproj(xc, c_shift, c_scale, norm_g[l], w_in_b[l], q_norm_g[l], k_norm_g[l], odd=odd)
        if not odd:
            e = l // 2
            q, k, v, hy, g = lat
            qc, kc, vc, hyc, gc = cx
            filt = (hy_filt_w1[e], hy_filt_b1[e], hy_filt_freq[e], hy_filt_w2[e], hy_filt_b2[e],
                    hy_filt_w3[e], hy_filt_b3[e])
            o_na = _na_attention(q, k, v, kc, vc, na_rpb[e])
            o_hy = _hyena(hy, hy_conv_w[e], hy_conv_b[e], filt, hy_skip[e])
            parts = [o_na, o_hy]
            if ctx_out:
                c_parts = [_ctx_attention(qc, kc, vc), _hyena(hyc, hy_conv_w[e], hy_conv_b[e], filt, hy_skip[e])]
        else:
            o_i = l // 2
            lam_init = 0.8 - 0.6 * math.exp(-0.3 * l)
            lam_vecs = (diff_lam_q1[o_i], diff_lam_k1[o_i], diff_lam_q2[o_i], diff_lam_k2[o_i])
            q, k, v, g = lat
            qc, kc, vc, gc = cx
            parts = [_diff_attention(q, k, v, lam_vecs, diff_subln_g[o_i], lam_init, kc, vc)]
            if ctx_out:
                c_parts = [_diff_attention(qc, kc, vc, lam_vecs, diff_subln_g[o_i], lam_init)]
        x = _outproj(x, gate, g, w_out_b[l], parts)
        if ctx_out:
            xc = _outproj(xc, c_gate, gc, w_out_b[l], c_parts)
    return x
```

```python
import functools
import math

import numpy as np
import jax
import jax.numpy as jnp
from jax import lax
from jax.experimental import pallas as pl
from jax.experimental.pallas import tpu as pltpu

F32 = jnp.float32
BF16 = jnp.bfloat16
HIGHEST = lax.Precision.HIGHEST

D_MODEL = 1024
DEPTH = 4
GRID_W = 64
HEAD_DIM = 64
N_HEADS_NA = 8
NA_WIDTH = N_HEADS_NA * HEAD_DIM
HY_WIDTH = D_MODEL - NA_WIDTH
HY_ORDER = 2
HY_EMB = 33
HY_HIDDEN = 64
HY_FAST_DECAY = 0.3
HY_SLOW_DECAY = 1.5
HY_TARGET = 1e-2
WIN_R = 8
WIN_C = 16
N_HEADS_DIFF = D_MODEL // (2 * HEAD_DIM)
DIFF_QK = N_HEADS_DIFF * 2 * HEAD_DIM
IN_WIDTH = 4 * D_MODEL
EPS = 1e-6
ROPE_BASE = 10000.0

LANES = 128
MXU_DIM = 256
VMEM_LIMIT = 56 * 1024 * 1024
NEG_BIG = -1e30

NA_Q_ROWS = 4
NA_WIN_ROWS = 12
HY_LANES = 2 * LANES


def _cparams(sem):
    return pltpu.CompilerParams(dimension_semantics=sem, vmem_limit_bytes=VMEM_LIMIT)


def _single(shape, index_map):
    return pl.BlockSpec(shape, index_map, pipeline_mode=pl.Buffered(1))


def _dot(a, b):
    return jnp.dot(a, b, preferred_element_type=F32)


def _dot_nt(a, b):
    return lax.dot_general(a, b, (((1,), (1,)), ((), ())), preferred_element_type=F32)


def _mod_kernel(v_ref, w_ref, b_ref, o_ref):
    v = v_ref[...]
    a = v * jax.nn.sigmoid(v)
    o_ref[0] = jnp.dot(a, w_ref[0], precision=HIGHEST, preferred_element_type=F32) + b_ref[0]


def _modulation(vecs, w_mod, b_mod):
    rows = vecs.shape[0]
    tn = 1024
    return pl.pallas_call(
        _mod_kernel,
        grid=(DEPTH, 3 * D_MODEL // tn),
        in_specs=[pl.BlockSpec((rows, D_MODEL), lambda l, j: (0, 0)),
                  pl.BlockSpec((1, D_MODEL, tn), lambda l, j: (l, 0, j)),
                  pl.BlockSpec((1, 1, tn), lambda l, j: (l, 0, j))],
        out_specs=pl.BlockSpec((1, rows, tn), lambda l, j: (l, 0, j)),
        out_shape=jax.ShapeDtypeStruct((DEPTH, rows, 3 * D_MODEL), F32),
        compiler_params=_cparams(("arbitrary", "arbitrary")),
        name="modulation",
    )(vecs, w_mod, b_mod.reshape(DEPTH, 1, 3 * D_MODEL))


_EVEN_SECTIONS = (("q", 0, NA_WIDTH, "qnorm"), ("k", NA_WIDTH, 2 * NA_WIDTH, "knorm"),
                  ("v", 2 * NA_WIDTH, 3 * NA_WIDTH, "copy"),
                  ("hy", 3 * NA_WIDTH, 3 * NA_WIDTH + 3 * HY_WIDTH, "copy"),
                  ("gate", IN_WIDTH - D_MODEL, IN_WIDTH, "silu"))
_ODD_SECTIONS = (("q", 0, DIFF_QK, "qnorm"), ("k", DIFF_QK, 2 * DIFF_QK, "knorm"),
                 ("v", 2 * DIFF_QK, 3 * DIFF_QK, "copy"),
                 ("gate", IN_WIDTH - D_MODEL, IN_WIDTH, "silu"))


def _rope_chunk(a, cos, sin_signed, low_half):
    up = pltpu.roll(a, LANES - HEAD_DIM // 2, axis=1)
    dn = pltpu.roll(a, HEAD_DIM // 2, axis=1)
    return a * cos + jnp.where(low_half, up, dn) * sin_signed


def _inproj_kernel(*refs, sections, rope):
    x_ref, shift_ref, scale_ref, g_ref, w_ref, qg_ref, kg_ref, e_ref = refs[:8]
    pos = 8
    if rope:
        cos_ref, sin_ref = refs[8:10]
        pos = 10
    out_refs = refs[pos:]
    x = x_ref[0]
    ms = jnp.mean(x * x, axis=-1, keepdims=True)
    h = x * lax.rsqrt(ms + EPS) * g_ref[...] * (1.0 + scale_ref[0]) + shift_ref[0]
    hb = h.astype(BF16)
    if rope:
        cos = cos_ref[...]
        sin_signed = sin_ref[...]
        lane = lax.broadcasted_iota(jnp.int32, cos.shape, 1)
        low_half = (lane % HEAD_DIM) < HEAD_DIM // 2
    for o_ref, (_, lo, hi, kind) in zip(out_refs, sections):
        for c0 in range(lo, hi, MXU_DIM):
            acc = _dot(hb, w_ref[:, c0:c0 + MXU_DIM])
            if kind in ("qnorm", "knorm"):
                gain = qg_ref[...] if kind == "qnorm" else kg_ref[...]
                ss = _dot((acc * acc).astype(BF16), e_ref[...])
                acc = acc * lax.rsqrt(ss * (1.0 / HEAD_DIM) + EPS) * gain
                if rope:
                    acc = jnp.concatenate(
                        [_rope_chunk(acc[:, j:j + LANES], cos, sin_signed, low_half)
                         for j in range(0, MXU_DIM, LANES)], axis=1)
            elif kind == "silu":
                acc = acc * jax.nn.sigmoid(acc)
            o_ref[0, :, c0 - lo:c0 - lo + MXU_DIM] = acc.astype(o_ref.dtype)


def _rope_tables(n_tokens):
    t = jnp.arange(n_tokens, dtype=jnp.int32)
    row = (t // GRID_W).astype(F32)
    col = (t % GRID_W).astype(F32)
    n_freq = HEAD_DIM // 4
    inv = ROPE_BASE ** (-jnp.arange(n_freq, dtype=F32) / n_freq)
    ang = jnp.concatenate([row[:, None] * inv, col[:, None] * inv], axis=-1)
    cos, sin = jnp.cos(ang), jnp.sin(ang)
    cos_t = jnp.concatenate([cos, cos, cos, cos], axis=-1)
    sin_t = jnp.concatenate([-sin, sin, -sin, sin], axis=-1)
    return cos_t, sin_t


def _inproj(x, shift, scale, g, w, qg, kg, *, odd, rope_tabs=None):
    B, S, _ = x.shape
    tm = min(S, 512)
    sections = _ODD_SECTIONS if odd else _EVEN_SECTIONS
    rope = rope_tabs is not None
    head = jnp.arange(MXU_DIM) // HEAD_DIM
    e = (head[:, None] == head[None, :]).astype(BF16)
    q_scale = HEAD_DIM ** -0.5 * (math.log2(math.e) if odd else 1.0)
    qg_t = jnp.tile(qg.astype(F32), MXU_DIM // HEAD_DIM)[None] * q_scale
    kg_t = jnp.tile(kg.astype(F32), MXU_DIM // HEAD_DIM)[None]
    const = lambda b, i: (0, 0)
    in_specs = [pl.BlockSpec((1, tm, D_MODEL), lambda b, i: (b, i, 0)),
                pl.BlockSpec((1, 1, D_MODEL), lambda b, i: (b, 0, 0)),
                pl.BlockSpec((1, 1, D_MODEL), lambda b, i: (b, 0, 0)),
                pl.BlockSpec((1, D_MODEL), const),
                _single((D_MODEL, IN_WIDTH), const),
                pl.BlockSpec((1, MXU_DIM), const),
                pl.BlockSpec((1, MXU_DIM), const),
                pl.BlockSpec((MXU_DIM, MXU_DIM), const)]
    args = [x, shift[:, None], scale[:, None], g[None], w, qg_t, kg_t, e]
    if rope:
        in_specs += [pl.BlockSpec((tm, LANES), lambda b, i: (i, 0))] * 2
        args += list(rope_tabs)
    out_shape = [jax.ShapeDtypeStruct((B, S, hi - lo), BF16) for _, lo, hi, _ in sections]
    out_specs = [pl.BlockSpec((1, tm, hi - lo), lambda b, i: (b, i, 0)) for _, lo, hi, _ in sections]
    return pl.pallas_call(
        functools.partial(_inproj_kernel, sections=sections, rope=rope),
        grid=(B, S // tm),
        in_specs=in_specs, out_specs=out_specs, out_shape=out_shape,
        compiler_params=_cparams(("arbitrary", "arbitrary")),
        name="inproj_odd" if odd else "inproj_even",
    )(*args)


def _outproj_kernel(*refs, n_parts):
    x_ref, gm_ref, gate_ref, w_ref = refs[:4]
    parts = refs[4:4 + n_parts]
    o_ref = refs[4 + n_parts]
    y = jnp.concatenate([p[0].astype(F32) for p in parts], axis=1) * gate_ref[0].astype(F32)
    o_ref[0] = x_ref[0] + gm_ref[0] * _dot(y.astype(BF16), w_ref[...])


def _outproj(x, gate_mod, gate, w, parts):
    B, S, _ = x.shape
    tm = min(S, 512)
    in_specs = [pl.BlockSpec((1, tm, D_MODEL), lambda b, i: (b, i, 0)),
                pl.BlockSpec((1, 1, D_MODEL), lambda b, i: (b, 0, 0)),
                pl.BlockSpec((1, tm, D_MODEL), lambda b, i: (b, i, 0)),
                _single((D_MODEL, D_MODEL), lambda b, i: (0, 0))]
    in_specs += [pl.BlockSpec((1, tm, p.shape[-1]), lambda b, i: (b, i, 0)) for p in parts]
    return pl.pallas_call(
        functools.partial(_outproj_kernel, n_parts=len(parts)),
        grid=(B, S // tm),
        in_specs=in_specs,
        out_specs=pl.BlockSpec((1, tm, D_MODEL), lambda b, i: (b, i, 0)),
        out_shape=jax.ShapeDtypeStruct((B, S, D_MODEL), F32),
        compiler_params=_cparams(("arbitrary", "arbitrary")),
        name="outproj",
    )(x, gate_mod[:, None], gate, w, *parts)


def _head_masks(shape):
    lane = lax.broadcasted_iota(jnp.int32, shape, 1)
    return lane < HEAD_DIM, lane >= HEAD_DIM


def _na_kernel(q_ref, k_ref, v_ref, kc_ref, vc_ref, bias_ref, o_ref, *, n_rows):
    i = pl.program_id(2)
    start = jnp.clip(i * NA_Q_ROWS - WIN_R // 2, 0, n_rows - NA_WIN_ROWS)
    off = pl.multiple_of(start * GRID_W, GRID_W)
    q = q_ref[0]
    kw = k_ref[0, pl.ds(off, NA_WIN_ROWS * GRID_W), :]
    vw = v_ref[0, pl.ds(off, NA_WIN_ROWS * GRID_W), :]
    kc = kc_ref[0]
    vc = vc_ref[0]
    outs = []
    for hh, sel in enumerate(_head_masks(q.shape)):
        qh = jnp.where(sel, q, jnp.zeros_like(q))
        s_loc = _dot_nt(qh, kw) + bias_ref[hh, 0]
        s_ctx = _dot_nt(qh, kc)
        m = jnp.maximum(jnp.max(s_loc, axis=-1, keepdims=True), jnp.max(s_ctx, axis=-1, keepdims=True))
        p_loc = jnp.exp(s_loc - m)
        p_ctx = jnp.exp(s_ctx - m)
        l = jnp.sum(p_loc, axis=-1, keepdims=True) + jnp.sum(p_ctx, axis=-1, keepdims=True)
        o = _dot(p_loc.astype(BF16), vw) + _dot(p_ctx.astype(BF16), vc)
        outs.append(o / l)
    lo_half, _ = _head_masks(outs[0].shape)
    o_ref[0] = jnp.where(lo_half, outs[0], outs[1]).astype(o_ref.dtype)


def _rpb_cols_kernel(r_ref, p_ref, m_ref, o_ref):
    o_ref[...] = jnp.dot(r_ref[...], p_ref[...], precision=HIGHEST, preferred_element_type=F32) + m_ref[...]


def _na_bias_table(rpb, n_rows):
    n_heads, n_drow, n_dcol = rpb.shape
    n_blk = n_rows // NA_Q_ROWS
    kr = min(WIN_R, n_rows)
    c = np.arange(GRID_W)
    cs = np.clip(c - WIN_C // 2, 0, GRID_W - WIN_C)
    col_ok = (c[None, :] >= cs[:, None]) & (c[None, :] < cs[:, None] + WIN_C)
    dcol = np.clip(c[None, :] - c[:, None] + WIN_C - 1, 0, 2 * WIN_C - 2)
    pad = -n_dcol % 8
    onehot = (np.arange(n_dcol + pad)[:, None] == dcol.reshape(1, -1)).astype(np.float32)
    col_mask = np.where(col_ok, 0.0, NEG_BIG).astype(np.float32).reshape(1, -1)
    r2 = jnp.pad(rpb.astype(F32).reshape(n_heads * n_drow, n_dcol), ((0, 0), (0, pad)))
    cols = pl.pallas_call(
        _rpb_cols_kernel,
        out_shape=jax.ShapeDtypeStruct((n_heads * n_drow, GRID_W * GRID_W), F32),
        name="rpb_cols",
    )(r2, jnp.asarray(onehot), jnp.asarray(col_mask))
    cols = cols.reshape(n_heads, n_drow, GRID_W, GRID_W)
    masked = jnp.full((n_heads, GRID_W, GRID_W), NEG_BIG, F32)
    tabs = []
    for i in (0, 1, n_blk - 1):
        r0 = i * NA_Q_ROWS
        start = min(max(r0 - WIN_R // 2, 0), n_rows - NA_WIN_ROWS)
        slabs = []
        for a in range(NA_Q_ROWS):
            r = r0 + a
            rs = min(max(r - kr // 2, 0), n_rows - kr)
            for j in range(NA_WIN_ROWS):
                rr = start + j
                slabs.append(cols[:, rr - r + WIN_R - 1] if rs <= rr < rs + kr else masked)
        b = jnp.stack(slabs, axis=1).reshape(n_heads, NA_Q_ROWS, NA_WIN_ROWS, GRID_W, GRID_W)
        b = b.transpose(0, 1, 3, 2, 4)
        tabs.append(b.reshape(n_heads, NA_Q_ROWS * GRID_W, NA_WIN_ROWS * GRID_W))
    return jnp.stack(tabs, axis=1)


def _na_attention(q, k, v, kc, vc, rpb):
    B, S, _ = q.shape
    n_ctx = kc.shape[1]
    n_rows = S // GRID_W
    n_blk = n_rows // NA_Q_ROWS
    assert n_rows % NA_Q_ROWS == 0 and n_rows >= NA_WIN_ROWS + NA_Q_ROWS
    tq = NA_Q_ROWS * GRID_W
    tk = NA_WIN_ROWS * GRID_W
    bias = _na_bias_table(rpb, n_rows)

    def bias_map(b, hp, i):
        return (hp, jnp.where(i == 0, 0, jnp.where(i == n_blk - 1, 2, 1)), 0, 0)

    return pl.pallas_call(
        functools.partial(_na_kernel, n_rows=n_rows),
        grid=(B, NA_WIDTH // LANES, n_blk),
        in_specs=[pl.BlockSpec((1, tq, LANES), lambda b, hp, i: (b, i, hp)),
                  pl.BlockSpec((1, S, LANES), lambda b, hp, i: (b, 0, hp)),
                  pl.BlockSpec((1, S, LANES), lambda b, hp, i: (b, 0, hp)),
                  pl.BlockSpec((1, n_ctx, LANES), lambda b, hp, i: (b, 0, hp)),
                  pl.BlockSpec((1, n_ctx, LANES), lambda b, hp, i: (b, 0, hp)),
                  pl.BlockSpec((2, 1, tq, tk), bias_map)],
        out_specs=pl.BlockSpec((1, tq, LANES), lambda b, hp, i: (b, i, hp)),
        out_shape=jax.ShapeDtypeStruct((B, S, NA_WIDTH), BF16),
        compiler_params=_cparams(("arbitrary", "arbitrary", "arbitrary")),
        name="na_attention",
    )(q, k, v, kc, vc, bias)


def _ctx_attn_kernel(q_ref, k_ref, v_ref, o_ref):
    q = q_ref[0]
    k = k_ref[0]
    v = v_ref[0]
    outs = []
    for sel in _head_masks(q.shape):
        qh = jnp.where(sel, q, jnp.zeros_like(q))
        s = _dot_nt(qh, k)
        p = jnp.exp(s - jnp.max(s, axis=-1, keepdims=True))
        outs.append(_dot(p.astype(BF16), v) / jnp.sum(p, axis=-1, keepdims=True))
    lo_half, _ = _head_masks(outs[0].shape)
    o_ref[0] = jnp.where(lo_half, outs[0], outs[1]).astype(o_ref.dtype)


def _ctx_attention(q, k, v):
    B, n, width = q.shape
    spec = pl.BlockSpec((1, n, LANES), lambda b, hp: (b, 0, hp))
    return pl.pallas_call(
        _ctx_attn_kernel,
        grid=(B, width // LANES),
        in_specs=[spec, spec, spec], out_specs=spec,
        out_shape=jax.ShapeDtypeStruct((B, n, width), BF16),
        compiler_params=_cparams(("arbitrary", "arbitrary")),
        name="ctx_attention",
    )(q, k, v)


def _diff_kernel(lq1_ref, lk1_ref, lq2_ref, lk2_ref, sg_ref, q_ref, k_ref, vt_ref, o_ref, *, tk, n_chunks, lam_init):
    qt = q_ref[0]
    tq = qt.shape[1]
    row = lax.broadcasted_iota(jnp.int32, qt.shape, 0)
    zero = jnp.zeros_like(qt)
    q2 = jnp.concatenate([jnp.where(row < HEAD_DIM, qt, zero), jnp.where(row >= HEAD_DIM, qt, zero)], axis=1)

    def scores(j):
        return _dot(k_ref[0, pl.ds(pl.multiple_of(j * tk, tk), tk), :], q2)

    def update(j, m, l, acc, s):
        m_new = jnp.maximum(m, jnp.max(s, axis=0, keepdims=True))
        p = jnp.exp2(s - m_new)
        alpha = jnp.exp2(m - m_new)
        l = alpha * l + jnp.sum(p, axis=0, keepdims=True)
        acc = alpha * acc + _dot(vt_ref[0, 0, j], p.astype(BF16))
        return m_new, l, acc

    def body(j, carry):
        m, l, acc, s = carry
        s_next = scores(j + 1)
        m, l, acc = update(j, m, l, acc, s)
        return m, l, acc, s_next

    init = (jnp.full((1, 2 * tq), NEG_BIG, F32), jnp.zeros((1, 2 * tq), F32),
            jnp.zeros((LANES, 2 * tq), F32), scores(0))
    m, l, acc, s = lax.fori_loop(0, n_chunks - 1, body, init)
    m, l, acc = update(n_chunks - 1, m, l, acc, s)

    lam = (jnp.exp(jnp.sum(lq1_ref[...] * lk1_ref[...], axis=-1, keepdims=True))
           - jnp.exp(jnp.sum(lq2_ref[...] * lk2_ref[...], axis=-1, keepdims=True)) + lam_init)
    o = acc / l
    o = o[:, :tq] - lam * o[:, tq:]
    o = o * lax.rsqrt(jnp.mean(o * o, axis=0, keepdims=True) + EPS) * sg_ref[...]
    o_ref[0] = (o * (1.0 - lam_init)).T.astype(o_ref.dtype)


def _diff_attention(q, k, v, lam_vecs, subln_g, lam_init):
    B, sq, width = q.shape
    sk = k.shape[1]
    n_heads = width // LANES
    tq = min(sq, 256)
    tk = min(sk, 256)
    vt = v.reshape(B, sk // tk, tk, n_heads, LANES).transpose(0, 3, 1, 4, 2)
    small = pl.BlockSpec((1, HEAD_DIM), lambda b, h, i: (0, 0))
    in_specs = [small] * 4 + [
        pl.BlockSpec((LANES, 1), lambda b, h, i: (0, 0)),
        pl.BlockSpec((1, LANES, tq), lambda b, h, i: (b, h, i)),
        pl.BlockSpec((1, sk, LANES), lambda b, h, i: (b, 0, h)),
        pl.BlockSpec((1, 1, sk // tk, LANES, tk), lambda b, h, i: (b, h, 0, 0, 0))]
    args = [a.astype(F32)[None] for a in lam_vecs] + [subln_g.astype(F32)[:, None], jnp.swapaxes(q, 1, 2), k, vt]
    return pl.pallas_call(
        functools.partial(_diff_kernel, tk=tk, n_chunks=sk // tk, lam_init=lam_init),
        grid=(B, n_heads, sq // tq),
        in_specs=in_specs,
        out_specs=pl.BlockSpec((1, tq, LANES), lambda b, h, i: (b, i, h)),
        out_shape=jax.ShapeDtypeStruct((B, sq, width), BF16),
        compiler_params=_cparams(("arbitrary", "arbitrary", "arbitrary")),
        name="diff_attention",
    )(*args)


def _filter_kernel(z_ref, w1_ref, b1_ref, fr_ref, w2_ref, b2_ref, w3_ref, b3_ref, dec_ref, h_ref, ss_ref):
    hdot = functools.partial(jnp.dot, precision=HIGHEST, preferred_element_type=F32)
    fr = fr_ref[...]
    h = jnp.sin(fr * (hdot(z_ref[...], w1_ref[...]) + b1_ref[...]))
    h = jnp.sin(fr * (hdot(h, w2_ref[...]) + b2_ref[...]))
    h = hdot(h, w3_ref[...]) + b3_ref[...]
    dec = dec_ref[...]
    h = h * jnp.concatenate([dec] * (2 * HY_ORDER), axis=1)
    h_ref[...] = h

    @pl.when(pl.program_id(0) == 0)
    def _():
        ss_ref[...] = jnp.zeros(ss_ref.shape, F32)

    ss_ref[...] += jnp.sum(h * h, axis=0, keepdims=True)


def _hyena_filters(L, w1, b1, freq, w2, b2, w3, b3):
    t = jnp.linspace(0.0, 1.0, L, dtype=F32)[:, None]
    w = (2.0 * math.pi / L) * jnp.arange(L, dtype=F32)[:, None]
    bands = (HY_EMB - 1) // 2
    fb = jnp.linspace(1e-4, bands - 1, bands, dtype=F32)[None, :]
    z = jnp.concatenate([t, jnp.cos(fb * w), -jnp.sin(fb * w)], axis=-1)
    emb_pad = HY_HIDDEN - HY_EMB
    z = jnp.pad(z, ((0, 0), (0, emb_pad)))
    w1p = jnp.pad(w1.astype(F32), ((0, emb_pad), (0, 0)))
    min_decay = math.log(HY_TARGET) / HY_SLOW_DECAY
    max_decay = math.log(HY_TARGET) / HY_FAST_DECAY
    deltas = jnp.abs(jnp.linspace(min_decay, max_decay, HY_WIDTH, dtype=F32))
    decay = jnp.exp(-t * deltas[None, :])
    tl = min(L, 512)
    width = HY_ORDER * 2 * HY_WIDTH
    const = lambda i: (0, 0)
    return pl.pallas_call(
        _filter_kernel,
        grid=(L // tl,),
        in_specs=[pl.BlockSpec((tl, HY_HIDDEN), lambda i: (i, 0)),
                  pl.BlockSpec((HY_HIDDEN, HY_HIDDEN), const), pl.BlockSpec((1, HY_HIDDEN), const),
                  pl.BlockSpec((1, HY_HIDDEN), const),
                  pl.BlockSpec((HY_HIDDEN, HY_HIDDEN), const), pl.BlockSpec((1, HY_HIDDEN), const),
                  pl.BlockSpec((HY_HIDDEN, width), const), pl.BlockSpec((1, width), const),
                  pl.BlockSpec((tl, HY_WIDTH), lambda i: (i, 0))],
        out_specs=[pl.BlockSpec((tl, width), lambda i: (i, 0)), pl.BlockSpec((1, width), const)],
        out_shape=[jax.ShapeDtypeStruct((L, width), F32), jax.ShapeDtypeStruct((1, width), F32)],
        compiler_params=_cparams(("arbitrary",)),
        name="hyena_filters",
    )(z, w1p, b1.astype(F32)[None], freq.astype(F32)[None], w2.astype(F32), b2.astype(F32)[None],
      w3.astype(F32), b3.astype(F32)[None], decay)


def _fft_sizes(L):
    n2 = 128 if L >= 1024 else 32
    n1 = 2 * L // n2
    return n1, n2


@functools.lru_cache(maxsize=None)
def _dft_tables(L):
    n1s, n2s = _fft_sizes(L)
    N = 2 * L
    k1 = np.arange(n1s)
    n1 = np.arange(n1s // 2)
    n2 = np.arange(n2s)
    m = (k1[None, :, None] * (n2s * n1[None, None, :] + n2[:, None, None])) % N
    th = 2.0 * np.pi * m / N
    c, s = np.cos(th), np.sin(th)
    l1 = np.concatenate([np.concatenate([c, s], axis=2), np.concatenate([-s, c], axis=2)], axis=1)
    ct, st = np.swapaxes(c, 1, 2), np.swapaxes(s, 1, 2)
    l1i = np.concatenate([np.concatenate([ct, -st], axis=2), np.concatenate([st, ct], axis=2)], axis=1)
    th3 = 2.0 * np.pi * ((n2[:, None] * n2[None, :]) % n2s) / n2s
    c3, s3 = np.cos(th3), np.sin(th3)
    l3 = np.block([[c3, s3], [-s3, c3]])
    l3i = np.block([[c3, -s3], [s3, c3]])
    return tuple(np.asarray(a, np.float32) for a in (l1, l3, l3i, l1i))


def _ld2(buf, rows):
    return jnp.concatenate([buf[0, rows, :], buf[1, rows, :]], axis=1)


def _st2(buf, rows, val):
    buf[0, rows, :] = val[:, :LANES]
    buf[1, rows, :] = val[:, LANES:]


def _fft_stage1(zr, zi, ar, ai, l1_ref, n1s, n2s):
    half = n1s // 2

    def body(n2, carry):
        src = pl.ds(n2, half, stride=n2s)
        d = jnp.concatenate([_ld2(zr, src), _ld2(zi, src)], axis=0)
        out = _dot(l1_ref[n2], d.astype(BF16))
        dst = pl.ds(n2, n1s, stride=n2s)
        _st2(ar, dst, out[:n1s])
        _st2(ai, dst, out[n1s:])
        return carry

    lax.fori_loop(0, n2s, body, 0)


def _fft_stage3(ar, ai, l3_ref, k1, n2s):
    r = pl.multiple_of(k1 * n2s, n2s)
    d = jnp.concatenate([_ld2(ar, pl.ds(r, n2s)), _ld2(ai, pl.ds(r, n2s))], axis=0)
    return r, _dot(l3_ref[...], d.astype(BF16))


def _spectrum_kernel(hf_ref, hb_ref, ssf_ref, ssb_ref, l1_ref, l3_ref, gr_ref, gi_ref,
                     zr, zi, ar, ai, *, n1s, n2s):
    nrm = lax.rsqrt(ssf_ref[...] + ssb_ref[...] + EPS)
    hf = hf_ref[...] * nrm
    hb = hb_ref[...] * nrm
    zr[0] = hf + hb
    zr[1] = hf - hb
    zi[...] = jnp.zeros(zi.shape, F32)
    _fft_stage1(zr, zi, ar, ai, l1_ref, n1s, n2s)
    inv_n = 1.0 / (n1s * n2s)

    def body(k1, carry):
        r, z = _fft_stage3(ar, ai, l3_ref, k1, n2s)
        gr_ref[0, pl.ds(r, n2s), :] = z[:n2s, :LANES] * inv_n
        gi_ref[0, pl.ds(r, n2s), :] = z[n2s:, LANES:] * inv_n
        return carry

    lax.fori_loop(0, n1s, body, 0)


def _hyena_spectra(h, ss):
    L = h.shape[0]
    n1s, n2s = _fft_sizes(L)
    N = 2 * L
    l1, l3, _, _ = _dft_tables(L)
    n_ct = HY_WIDTH // LANES
    fwd = lambda o, ct: (0, o * 2 * n_ct + ct)
    bwd = lambda o, ct: (0, o * 2 * n_ct + n_ct + ct)
    out_spec = pl.BlockSpec((1, N, LANES), lambda o, ct: (o, 0, ct))
    return pl.pallas_call(
        functools.partial(_spectrum_kernel, n1s=n1s, n2s=n2s),
        grid=(HY_ORDER, n_ct),
        in_specs=[pl.BlockSpec((L, LANES), fwd), pl.BlockSpec((L, LANES), bwd),
                  pl.BlockSpec((1, LANES), fwd), pl.BlockSpec((1, LANES), bwd),
                  _single(l1.shape, lambda o, ct: (0, 0, 0)), _single(l3.shape, lambda o, ct: (0, 0))],
        out_specs=[out_spec, out_spec],
        out_shape=[jax.ShapeDtypeStruct((HY_ORDER, N, HY_WIDTH), F32)] * 2,
        scratch_shapes=[pltpu.VMEM((2, L, LANES), F32), pltpu.VMEM((2, L, LANES), F32),
                        pltpu.VMEM((2, N, LANES), F32), pltpu.VMEM((2, N, LANES), F32)],
        compiler_params=_cparams(("arbitrary", "arbitrary")),
        name="hyena_spectra",
    )(h, h, ss, ss, jnp.asarray(l1, BF16), jnp.asarray(l3, BF16))


def _short_conv(u, w_ref, b_ref):
    L = u.shape[0]
    row = lax.broadcasted_iota(jnp.int32, u.shape, 0)
    prev = jnp.where(row == 0, 0.0, pltpu.roll(u, 1, axis=0))
    nxt = jnp.where(row == L - 1, 0.0, pltpu.roll(u, L - 1, axis=0))
    return b_ref[...] + prev * w_ref[0:1, :] + u * w_ref[1:2, :] + nxt * w_ref[2:3, :]


def _conv_kernel(a_ref, x_ref, wa_ref, ba_ref, wx_ref, bx_ref, gr_ref, gi_ref, skip_ref,
                 l1_ref, l3_ref, l3i_ref, l1i_ref, o_ref, zr, zi, ar, ai, *, conv_a, n1s, n2s):
    half = n1s // 2
    slots = ((zr, 0), (zr, 1), (zi, 0), (zi, 1))
    for s, (buf, hi) in enumerate(slots):
        u = a_ref[s].astype(F32)
        buf[hi] = _short_conv(u, wa_ref, ba_ref) if conv_a else u
    _fft_stage1(zr, zi, ar, ai, l1_ref, n1s, n2s)

    def mid(k1, carry):
        r, z = _fft_stage3(ar, ai, l3_ref, k1, n2s)
        g_r = gr_ref[0, pl.ds(r, n2s), :]
        g_i = gi_ref[0, pl.ds(r, n2s), :]
        g_r = jnp.concatenate([g_r, g_r], axis=1)
        g_i = jnp.concatenate([g_i, g_i], axis=1)
        z_r, z_i = z[:n2s], z[n2s:]
        p = jnp.concatenate([z_r * g_r - z_i * g_i, z_r * g_i + z_i * g_r], axis=0)
        b = _dot(l3i_ref[...], p.astype(BF16))
        _st2(ar, pl.ds(r, n2s), b[:n2s])
        _st2(ai, pl.ds(r, n2s), b[n2s:])
        return carry

    lax.fori_loop(0, n1s, mid, 0)
    skip = skip_ref[0]
    skip = jnp.concatenate([skip, skip], axis=1)

    def last(n2, carry):
        src = pl.ds(n2, n1s, stride=n2s)
        d = jnp.concatenate([_ld2(ar, src), _ld2(ai, src)], axis=0)
        y = _dot(l1i_ref[n2], d.astype(BF16))
        rows = pl.ds(n2, half, stride=n2s)
        _st2(zr, rows, y[:half] + skip * _ld2(zr, rows))
        _st2(zi, rows, y[half:] + skip * _ld2(zi, rows))
        return carry

    lax.fori_loop(0, n2s, last, 0)
    for s, (buf, hi) in enumerate(slots):
        xg = _short_conv(x_ref[s].astype(F32), wx_ref, bx_ref)
        o_ref[s] = (xg * buf[hi]).astype(o_ref.dtype)


def _hyena_conv(a, a_col, x, x_col, conv_w, conv_b, wa_col, wx_col, g_r, g_i, skip, order, *, conv_a, out_dtype):
    B, L, _ = a.shape
    n1s, n2s = _fft_sizes(L)
    N = 2 * L
    n_ct = HY_WIDTH // LANES
    seqs = 4
    assert B % seqs == 0
    tables = _dft_tables(L)
    t_specs = [_single(t.shape, (lambda b, ct: (0, 0, 0)) if t.ndim == 3 else (lambda b, ct: (0, 0))) for t in tables]
    return pl.pallas_call(
        functools.partial(_conv_kernel, conv_a=conv_a, n1s=n1s, n2s=n2s),
        grid=(n_ct, B // seqs),
        in_specs=[_single((seqs, L, LANES), lambda ct, b: (b, 0, a_col + ct)),
                  _single((seqs, L, LANES), lambda ct, b: (b, 0, x_col + ct)),
                  pl.BlockSpec((3, LANES), lambda ct, b: (0, wa_col + ct)),
                  pl.BlockSpec((1, LANES), lambda ct, b: (0, wa_col + ct)),
                  pl.BlockSpec((3, LANES), lambda ct, b: (0, wx_col + ct)),
                  pl.BlockSpec((1, LANES), lambda ct, b: (0, wx_col + ct)),
                  _single((1, N, LANES), lambda ct, b: (order, 0, ct)),
                  _single((1, N, LANES), lambda ct, b: (order, 0, ct)),
                  pl.BlockSpec((1, 1, LANES), lambda ct, b: (order, 0, ct))] + t_specs,
        out_specs=_single((seqs, L, LANES), lambda ct, b: (b, 0, ct)),
        out_shape=jax.ShapeDtypeStruct((B, L, HY_WIDTH), out_dtype),
        scratch_shapes=[pltpu.VMEM((2, L, LANES), F32), pltpu.VMEM((2, L, LANES), F32),
                        pltpu.VMEM((2, N, LANES), F32), pltpu.VMEM((2, N, LANES), F32)],
        compiler_params=_cparams(("arbitrary", "arbitrary")),
        name="hyena_conv",
    )(a, x, conv_w, conv_b[None], conv_w, conv_b[None], g_r, g_i, skip[:, None],
      *[jnp.asarray(t, BF16) for t in tables])


def _hyena(hy, conv_w, conv_b, filt, skip):
    L = hy.shape[1]
    n_ct = HY_WIDTH // LANES
    h, ss = _hyena_filters(L, *filt)
    g_r, g_i = _hyena_spectra(h, ss)
    conv_w = conv_w.astype(F32)
    conv_b = conv_b.astype(F32)
    skip = skip.astype(F32)
    z1 = _hyena_conv(hy, 0, hy, n_ct, conv_w, conv_b, 0, n_ct, g_r, g_i, skip, 0, conv_a=True, out_dtype=F32)
    return _hyena_conv(z1, 0, hy, 2 * n_ct, conv_w, conv_b, 0, 2 * n_ct, g_r, g_i, skip, 1,
                       conv_a=False, out_dtype=BF16)


def kernel(x, c, ctx, c_ctx, norm_g, w_mod, b_mod, w_in, w_out, q_norm_g, k_norm_g, na_rpb, hy_conv_w, hy_conv_b, hy_filt_w1, hy_filt_b1, hy_filt_freq, hy_filt_w2, hy_filt_b2, hy_filt_w3, hy_filt_b3, hy_skip, diff_lam_q1, diff_lam_k1, diff_lam_q2, diff_lam_k2, diff_subln_g):
    B, S, D = x.shape
    mod_rows = -(-(B + 1) // 8) * 8
    vecs = jnp.concatenate([c, c_ctx[None], jnp.zeros((mod_rows - B - 1, D), F32)], axis=0)
    mods = _modulation(vecs, w_mod, b_mod)
    w_in_b = w_in.astype(BF16)
    w_out_b = w_out.astype(BF16)
    rope_tabs = _rope_tables(S)
    xc = ctx
    for l in range(DEPTH):
        ctx_out = l < DEPTH - 1
        shift, scale, gate = (mods[l, :B, j * D:(j + 1) * D] for j in range(3))
        c_shift, c_scale, c_gate = (jnp.broadcast_to(mods[l, B:B + 1, j * D:(j + 1) * D], (B, D)) for j in range(3))
        odd = l % 2 == 1
        lat = _inproj(x, shift, scale, norm_g[l], w_in_b[l], q_norm_g[l], k_norm_g[l], odd=odd,
                      rope_tabs=rope_tabs if odd else None)
        cx = _inproj(xc, c_shift, c_scale, norm_g[l], w_in_b[l], q_norm_g[l], k_norm_g[l], odd=odd)
        if not odd:
            e = l // 2
            q, k, v, hy, g = lat
            qc, kc, vc, hyc, gc = cx
            filt = (hy_filt_w1[e], hy_filt_b1[e], hy_filt_freq[e], hy_filt_w2[e], hy_filt_b2[e],
                    hy_filt_w3[e], hy_filt_b3[e])
            o_na = _na_attention(q, k, v, kc, vc, na_rpb[e])
            o_hy = _hyena(hy, hy_conv_w[e], hy_conv_b[e], filt, hy_skip[e])
            parts = [o_na, o_hy]
            if ctx_out:
                c_parts = [_ctx_attention(qc, kc, vc), _hyena(hyc, hy_conv_w[e], hy_conv_b[e], filt, hy_skip[e])]
        else:
            o_i = l // 2
            lam_init = 0.8 - 0.6 * math.exp(-0.3 * l)
            lam_vecs = (diff_lam_q1[o_i], diff_lam_k1[o_i], diff_lam_q2[o_i], diff_lam_k2[o_i])
            q, k, v, g = lat
            qc, kc, vc, gc = cx
            k_all = jnp.concatenate([k, kc], axis=1)
            v_all = jnp.concatenate([v, vc], axis=1)
            parts = [_diff_attention(q, k_all, v_all, lam_vecs, diff_subln_g[o_i], lam_init)]
            if ctx_out:
                c_parts = [_diff_attention(qc, kc, vc, lam_vecs, diff_subln_g[o_i], lam_init)]
        x = _outproj(x, gate, g, w_out_b[l], parts)
        if ctx_out:
            xc = _outproj(xc, c_gate, gc, w_out_b[l], c_parts)
    return x
```

```python
import functools
import math

import numpy as np
import jax
import jax.numpy as jnp
from jax import lax
from jax.experimental import pallas as pl
from jax.experimental.pallas import tpu as pltpu

F32 = jnp.float32
BF16 = jnp.bfloat16
HIGHEST = lax.Precision.HIGHEST

D_MODEL = 1024
DEPTH = 4
GRID_W = 64
HEAD_DIM = 64
N_HEADS_NA = 8
NA_WIDTH = N_HEADS_NA * HEAD_DIM
HY_WIDTH = D_MODEL - NA_WIDTH
HY_ORDER = 2
HY_EMB = 33
HY_HIDDEN = 64
HY_FAST_DECAY = 0.3
HY_SLOW_DECAY = 1.5
HY_TARGET = 1e-2
WIN_R = 8
WIN_C = 16
N_HEADS_DIFF = D_MODEL // (2 * HEAD_DIM)
DIFF_QK = N_HEADS_DIFF * 2 * HEAD_DIM
IN_WIDTH = 4 * D_MODEL
EPS = 1e-6
ROPE_BASE = 10000.0

LANES = 128
MXU_DIM = 256
VMEM_LIMIT = 56 * 1024 * 1024
NEG_BIG = -1e30

NA_Q_ROWS = 4
NA_WIN_ROWS = 12
HY_LANES = 2 * LANES
HY_UNROLL = 4


def _cparams(sem):
    return pltpu.CompilerParams(dimension_semantics=sem, vmem_limit_bytes=VMEM_LIMIT)


def _single(shape, index_map):
    return pl.BlockSpec(shape, index_map, pipeline_mode=pl.Buffered(1))


def _dot(a, b):
    return jnp.dot(a, b, preferred_element_type=F32)


def _dot_nt(a, b):
    return lax.dot_general(a, b, (((1,), (1,)), ((), ())), preferred_element_type=F32)


def _mod_kernel(v_ref, w_ref, b_ref, o_ref):
    v = v_ref[...]
    a = v * jax.nn.sigmoid(v)
    o_ref[0] = jnp.dot(a, w_ref[0], precision=HIGHEST, preferred_element_type=F32) + b_ref[0]


def _modulation(vecs, w_mod, b_mod):
    rows = vecs.shape[0]
    tn = 1024
    return pl.pallas_call(
        _mod_kernel,
        grid=(DEPTH, 3 * D_MODEL // tn),
        in_specs=[pl.BlockSpec((rows, D_MODEL), lambda l, j: (0, 0)),
                  pl.BlockSpec((1, D_MODEL, tn), lambda l, j: (l, 0, j)),
                  pl.BlockSpec((1, 1, tn), lambda l, j: (l, 0, j))],
        out_specs=pl.BlockSpec((1, rows, tn), lambda l, j: (l, 0, j)),
        out_shape=jax.ShapeDtypeStruct((DEPTH, rows, 3 * D_MODEL), F32),
        compiler_params=_cparams(("arbitrary", "arbitrary")),
        name="modulation",
    )(vecs, w_mod, b_mod.reshape(DEPTH, 1, 3 * D_MODEL))


_EVEN_SECTIONS = (("q", 0, NA_WIDTH, "qnorm"), ("k", NA_WIDTH, 2 * NA_WIDTH, "knorm"),
                  ("v", 2 * NA_WIDTH, 3 * NA_WIDTH, "copy"),
                  ("hy", 3 * NA_WIDTH, 3 * NA_WIDTH + 3 * HY_WIDTH, "copy"),
                  ("gate", IN_WIDTH - D_MODEL, IN_WIDTH, "silu"))
_ODD_SECTIONS = (("q", 0, DIFF_QK, "qnorm"), ("k", DIFF_QK, 2 * DIFF_QK, "knorm"),
                 ("v", 2 * DIFF_QK, 3 * DIFF_QK, "copy"),
                 ("gate", IN_WIDTH - D_MODEL, IN_WIDTH, "silu"))


def _rope_chunk(a, cos, sin_signed, low_half):
    up = pltpu.roll(a, LANES - HEAD_DIM // 2, axis=1)
    dn = pltpu.roll(a, HEAD_DIM // 2, axis=1)
    return a * cos + jnp.where(low_half, up, dn) * sin_signed


def _inproj_kernel(*refs, sections, rope):
    x_ref, shift_ref, scale_ref, g_ref, w_ref, qg_ref, kg_ref, e_ref = refs[:8]
    pos = 8
    if rope:
        cos_ref, sin_ref = refs[8:10]
        pos = 10
    out_refs = refs[pos:]
    x = x_ref[0]
    ms = jnp.mean(x * x, axis=-1, keepdims=True)
    h = x * lax.rsqrt(ms + EPS) * g_ref[...] * (1.0 + scale_ref[0]) + shift_ref[0]
    hb = h.astype(BF16)
    if rope:
        cos = cos_ref[...]
        sin_signed = sin_ref[...]
        lane = lax.broadcasted_iota(jnp.int32, cos.shape, 1)
        low_half = (lane % HEAD_DIM) < HEAD_DIM // 2
    for o_ref, (_, lo, hi, kind) in zip(out_refs, sections):
        for c0 in range(lo, hi, MXU_DIM):
            acc = _dot(hb, w_ref[:, c0:c0 + MXU_DIM])
            if kind in ("qnorm", "knorm"):
                gain = qg_ref[...] if kind == "qnorm" else kg_ref[...]
                ss = _dot((acc * acc).astype(BF16), e_ref[...])
                acc = acc * lax.rsqrt(ss * (1.0 / HEAD_DIM) + EPS) * gain
                if rope:
                    acc = jnp.concatenate(
                        [_rope_chunk(acc[:, j:j + LANES], cos, sin_signed, low_half)
                         for j in range(0, MXU_DIM, LANES)], axis=1)
            elif kind == "silu":
                acc = acc * jax.nn.sigmoid(acc)
            o_ref[0, :, c0 - lo:c0 - lo + MXU_DIM] = acc.astype(o_ref.dtype)


def _rope_tables(n_tokens):
    t = jnp.arange(n_tokens, dtype=jnp.int32)
    row = (t // GRID_W).astype(F32)
    col = (t % GRID_W).astype(F32)
    n_freq = HEAD_DIM // 4
    inv = ROPE_BASE ** (-jnp.arange(n_freq, dtype=F32) / n_freq)
    ang = jnp.concatenate([row[:, None] * inv, col[:, None] * inv], axis=-1)
    cos, sin = jnp.cos(ang), jnp.sin(ang)
    cos_t = jnp.concatenate([cos, cos, cos, cos], axis=-1)
    sin_t = jnp.concatenate([-sin, sin, -sin, sin], axis=-1)
    return cos_t, sin_t


def _inproj(x, shift, scale, g, w, qg, kg, *, odd, rope_tabs=None):
    B, S, _ = x.shape
    tm = min(S, 512)
    sections = _ODD_SECTIONS if odd else _EVEN_SECTIONS
    rope = rope_tabs is not None
    head = jnp.arange(MXU_DIM) // HEAD_DIM
    e = (head[:, None] == head[None, :]).astype(BF16)
    q_scale = HEAD_DIM ** -0.5 * (math.log2(math.e) if odd else 1.0)
    qg_t = jnp.tile(qg.astype(F32), MXU_DIM // HEAD_DIM)[None] * q_scale
    kg_t = jnp.tile(kg.astype(F32), MXU_DIM // HEAD_DIM)[None]
    const = lambda b, i: (0, 0)
    in_specs = [pl.BlockSpec((1, tm, D_MODEL), lambda b, i: (b, i, 0)),
                pl.BlockSpec((1, 1, D_MODEL), lambda b, i: (b, 0, 0)),
                pl.BlockSpec((1, 1, D_MODEL), lambda b, i: (b, 0, 0)),
                pl.BlockSpec((1, D_MODEL), const),
                _single((D_MODEL, IN_WIDTH), const),
                pl.BlockSpec((1, MXU_DIM), const),
                pl.BlockSpec((1, MXU_DIM), const),
                pl.BlockSpec((MXU_DIM, MXU_DIM), const)]
    args = [x, shift[:, None], scale[:, None], g[None], w, qg_t, kg_t, e]
    if rope:
        in_specs += [pl.BlockSpec((tm, LANES), lambda b, i: (i, 0))] * 2
        args += list(rope_tabs)
    out_shape = [jax.ShapeDtypeStruct((B, S, hi - lo), BF16) for _, lo, hi, _ in sections]
    out_specs = [pl.BlockSpec((1, tm, hi - lo), lambda b, i: (b, i, 0)) for _, lo, hi, _ in sections]
    return pl.pallas_call(
        functools.partial(_inproj_kernel, sections=sections, rope=rope),
        grid=(B, S // tm),
        in_specs=in_specs, out_specs=out_specs, out_shape=out_shape,
        compiler_params=_cparams(("arbitrary", "arbitrary")),
        name="inproj_odd" if odd else "inproj_even",
    )(*args)


def _outproj_kernel(*refs, n_parts):
    x_ref, gm_ref, gate_ref, w_ref = refs[:4]
    parts = refs[4:4 + n_parts]
    o_ref = refs[4 + n_parts]
    y = jnp.concatenate([p[0].astype(F32) for p in parts], axis=1) * gate_ref[0].astype(F32)
    o_ref[0] = x_ref[0] + gm_ref[0] * _dot(y.astype(BF16), w_ref[...])


def _outproj(x, gate_mod, gate, w, parts):
    B, S, _ = x.shape
    tm = min(S, 512)
    in_specs = [pl.BlockSpec((1, tm, D_MODEL), lambda b, i: (b, i, 0)),
                pl.BlockSpec((1, 1, D_MODEL), lambda b, i: (b, 0, 0)),
                pl.BlockSpec((1, tm, D_MODEL), lambda b, i: (b, i, 0)),
                _single((D_MODEL, D_MODEL), lambda b, i: (0, 0))]
    in_specs += [pl.BlockSpec((1, tm, p.shape[-1]), lambda b, i: (b, i, 0)) for p in parts]
    return pl.pallas_call(
        functools.partial(_outproj_kernel, n_parts=len(parts)),
        grid=(B, S // tm),
        in_specs=in_specs,
        out_specs=pl.BlockSpec((1, tm, D_MODEL), lambda b, i: (b, i, 0)),
        out_shape=jax.ShapeDtypeStruct((B, S, D_MODEL), F32),
        compiler_params=_cparams(("arbitrary", "arbitrary")),
        name="outproj",
    )(x, gate_mod[:, None], gate, w, *parts)


def _head_masks(shape):
    lane = lax.broadcasted_iota(jnp.int32, shape, 1)
    return lane < HEAD_DIM, lane >= HEAD_DIM


def _na_kernel(q_ref, k_ref, v_ref, kc_ref, vc_ref, bias_ref, o_ref, *, n_rows):
    i = pl.program_id(2)
    start = jnp.clip(i * NA_Q_ROWS - WIN_R // 2, 0, n_rows - NA_WIN_ROWS)
    off = pl.multiple_of(start * GRID_W, GRID_W)
    q = q_ref[0]
    kw = k_ref[0, pl.ds(off, NA_WIN_ROWS * GRID_W), :]
    vw = v_ref[0, pl.ds(off, NA_WIN_ROWS * GRID_W), :]
    kc = kc_ref[0]
    vc = vc_ref[0]
    outs = []
    for hh, sel in enumerate(_head_masks(q.shape)):
        qh = jnp.where(sel, q, jnp.zeros_like(q))
        s_loc = _dot_nt(qh, kw) + bias_ref[hh, 0]
        s_ctx = _dot_nt(qh, kc)
        m = jnp.maximum(jnp.max(s_loc, axis=-1, keepdims=True), jnp.max(s_ctx, axis=-1, keepdims=True))
        p_loc = jnp.exp(s_loc - m)
        p_ctx = jnp.exp(s_ctx - m)
        l = jnp.sum(p_loc, axis=-1, keepdims=True) + jnp.sum(p_ctx, axis=-1, keepdims=True)
        o = _dot(p_loc.astype(BF16), vw) + _dot(p_ctx.astype(BF16), vc)
        outs.append(o / l)
    lo_half, _ = _head_masks(outs[0].shape)
    o_ref[0] = jnp.where(lo_half, outs[0], outs[1]).astype(o_ref.dtype)


def _rpb_cols_kernel(r_ref, p_ref, m_ref, o_ref):
    o_ref[...] = jnp.dot(r_ref[...], p_ref[...], precision=HIGHEST, preferred_element_type=F32) + m_ref[...]


def _na_bias_table(rpb, n_rows):
    n_heads, n_drow, n_dcol = rpb.shape
    n_blk = n_rows // NA_Q_ROWS
    kr = min(WIN_R, n_rows)
    c = np.arange(GRID_W)
    cs = np.clip(c - WIN_C // 2, 0, GRID_W - WIN_C)
    col_ok = (c[None, :] >= cs[:, None]) & (c[None, :] < cs[:, None] + WIN_C)
    dcol = np.clip(c[None, :] - c[:, None] + WIN_C - 1, 0, 2 * WIN_C - 2)
    pad = -n_dcol % 8
    onehot = (np.arange(n_dcol + pad)[:, None] == dcol.reshape(1, -1)).astype(np.float32)
    col_mask = np.where(col_ok, 0.0, NEG_BIG).astype(np.float32).reshape(1, -1)
    r2 = jnp.pad(rpb.astype(F32).reshape(n_heads * n_drow, n_dcol), ((0, 0), (0, pad)))
    cols = pl.pallas_call(
        _rpb_cols_kernel,
        out_shape=jax.ShapeDtypeStruct((n_heads * n_drow, GRID_W * GRID_W), F32),
        name="rpb_cols",
    )(r2, jnp.asarray(onehot), jnp.asarray(col_mask))
    cols = cols.reshape(n_heads, n_drow, GRID_W, GRID_W)
    masked = jnp.full((n_heads, GRID_W, GRID_W), NEG_BIG, F32)
    tabs = []
    for i in (0, 1, n_blk - 1):
        r0 = i * NA_Q_ROWS
        start = min(max(r0 - WIN_R // 2, 0), n_rows - NA_WIN_ROWS)
        slabs = []
        for a in range(NA_Q_ROWS):
            r = r0 + a
            rs = min(max(r - kr // 2, 0), n_rows - kr)
            for j in range(NA_WIN_ROWS):
                rr = start + j
                slabs.append(cols[:, rr - r + WIN_R - 1] if rs <= rr < rs + kr else masked)
        b = jnp.stack(slabs, axis=1).reshape(n_heads, NA_Q_ROWS, NA_WIN_ROWS, GRID_W, GRID_W)
        b = b.transpose(0, 1, 3, 2, 4)
        tabs.append(b.reshape(n_heads, NA_Q_ROWS * GRID_W, NA_WIN_ROWS * GRID_W))
    return jnp.stack(tabs, axis=1)


def _na_attention(q, k, v, kc, vc, rpb):
    B, S, _ = q.shape
    n_ctx = kc.shape[1]
    n_rows = S // GRID_W
    n_blk = n_rows // NA_Q_ROWS
    assert n_rows % NA_Q_ROWS == 0 and n_rows >= NA_WIN_ROWS + NA_Q_ROWS
    tq = NA_Q_ROWS * GRID_W
    tk = NA_WIN_ROWS * GRID_W
    bias = _na_bias_table(rpb, n_rows)

    def bias_map(b, hp, i):
        return (hp, jnp.where(i == 0, 0, jnp.where(i == n_blk - 1, 2, 1)), 0, 0)

    return pl.pallas_call(
        functools.partial(_na_kernel, n_rows=n_rows),
        grid=(B, NA_WIDTH // LANES, n_blk),
        in_specs=[pl.BlockSpec((1, tq, LANES), lambda b, hp, i: (b, i, hp)),
                  pl.BlockSpec((1, S, LANES), lambda b, hp, i: (b, 0, hp)),
                  pl.BlockSpec((1, S, LANES), lambda b, hp, i: (b, 0, hp)),
                  pl.BlockSpec((1, n_ctx, LANES), lambda b, hp, i: (b, 0, hp)),
                  pl.BlockSpec((1, n_ctx, LANES), lambda b, hp, i: (b, 0, hp)),
                  pl.BlockSpec((2, 1, tq, tk), bias_map)],
        out_specs=pl.BlockSpec((1, tq, LANES), lambda b, hp, i: (b, i, hp)),
        out_shape=jax.ShapeDtypeStruct((B, S, NA_WIDTH), BF16),
        compiler_params=_cparams(("arbitrary", "arbitrary", "arbitrary")),
        name="na_attention",
    )(q, k, v, kc, vc, bias)


def _ctx_attn_kernel(q_ref, k_ref, v_ref, o_ref):
    q = q_ref[0]
    k = k_ref[0]
    v = v_ref[0]
    outs = []
    for sel in _head_masks(q.shape):
        qh = jnp.where(sel, q, jnp.zeros_like(q))
        s = _dot_nt(qh, k)
        p = jnp.exp(s - jnp.max(s, axis=-1, keepdims=True))
        outs.append(_dot(p.astype(BF16), v) / jnp.sum(p, axis=-1, keepdims=True))
    lo_half, _ = _head_masks(outs[0].shape)
    o_ref[0] = jnp.where(lo_half, outs[0], outs[1]).astype(o_ref.dtype)


def _ctx_attention(q, k, v):
    B, n, width = q.shape
    spec = pl.BlockSpec((1, n, LANES), lambda b, hp: (b, 0, hp))
    return pl.pallas_call(
        _ctx_attn_kernel,
        grid=(B, width // LANES),
        in_specs=[spec, spec, spec], out_specs=spec,
        out_shape=jax.ShapeDtypeStruct((B, n, width), BF16),
        compiler_params=_cparams(("arbitrary", "arbitrary")),
        name="ctx_attention",
    )(q, k, v)


def _diff_kernel(lq1_ref, lk1_ref, lq2_ref, lk2_ref, sg_ref, q_ref, k_ref, vt_ref, o_ref,
                 s0_ref, s1_ref, p0_ref, p1_ref, acc_ref, *, tk, n_chunks, lam_init):
    s_bufs = (s0_ref, s1_ref)
    p_bufs = (p0_ref, p1_ref)
    qt = q_ref[0]
    tq = qt.shape[1]
    row = lax.broadcasted_iota(jnp.int32, qt.shape, 0)
    zero = jnp.zeros_like(qt)
    q2 = jnp.concatenate([jnp.where(row < HEAD_DIM, qt, zero), jnp.where(row >= HEAD_DIM, qt, zero)], axis=1)

    def scores(j):
        s = _dot(k_ref[0, j * tk:(j + 1) * tk, :], q2)
        s_bufs[j % 2][...] = s
        return jnp.max(s, axis=0, keepdims=True)

    cmax = scores(0)
    m = None
    for j in range(n_chunks):
        cmax_next = scores(j + 1) if j + 1 < n_chunks else None
        m_new = cmax if m is None else jnp.maximum(m, cmax)
        p_bufs[j % 2][...] = jnp.exp2((s_bufs[j % 2][...] - m_new).astype(BF16))
        pv = _dot(vt_ref[0, 0, j], p_bufs[j % 2][...])
        acc_ref[...] = pv if m is None else jnp.exp2(m - m_new) * acc_ref[...] + pv
        m, cmax = m_new, cmax_next
    l = acc_ref[LANES:LANES + 1, :]
    acc = acc_ref[:LANES, :]

    lam = (jnp.exp(jnp.sum(lq1_ref[...] * lk1_ref[...], axis=-1, keepdims=True))
           - jnp.exp(jnp.sum(lq2_ref[...] * lk2_ref[...], axis=-1, keepdims=True)) + lam_init)
    o = acc / l
    o = o[:, :tq] - lam * o[:, tq:]
    o = o * lax.rsqrt(jnp.mean(o * o, axis=0, keepdims=True) + EPS) * sg_ref[...]
    o_ref[0] = (o * (1.0 - lam_init)).T.astype(o_ref.dtype)


def _diff_attention(q, k, v, lam_vecs, subln_g, lam_init):
    B, sq, width = q.shape
    sk = k.shape[1]
    n_heads = width // LANES
    tq = min(sq, 256)
    tk = min(sk, 256)
    vt = v.reshape(B, sk // tk, tk, n_heads, LANES).transpose(0, 3, 1, 4, 2)
    ones_rows = 16
    vt = jnp.concatenate([vt, jnp.ones(vt.shape[:3] + (ones_rows, tk), BF16)], axis=3)
    small = pl.BlockSpec((1, HEAD_DIM), lambda b, h, i: (0, 0))
    in_specs = [small] * 4 + [
        pl.BlockSpec((LANES, 1), lambda b, h, i: (0, 0)),
        pl.BlockSpec((1, LANES, tq), lambda b, h, i: (b, h, i)),
        pl.BlockSpec((1, sk, LANES), lambda b, h, i: (b, 0, h)),
        pl.BlockSpec((1, 1, sk // tk, LANES + ones_rows, tk), lambda b, h, i: (b, h, 0, 0, 0))]
    args = [a.astype(F32)[None] for a in lam_vecs] + [subln_g.astype(F32)[:, None], jnp.swapaxes(q, 1, 2), k, vt]
    return pl.pallas_call(
        functools.partial(_diff_kernel, tk=tk, n_chunks=sk // tk, lam_init=lam_init),
        grid=(B, n_heads, sq // tq),
        in_specs=in_specs,
        out_specs=pl.BlockSpec((1, tq, LANES), lambda b, h, i: (b, i, h)),
        out_shape=jax.ShapeDtypeStruct((B, sq, width), BF16),
        scratch_shapes=[pltpu.VMEM((tk, 2 * tq), F32), pltpu.VMEM((tk, 2 * tq), F32),
                        pltpu.VMEM((tk, 2 * tq), BF16), pltpu.VMEM((tk, 2 * tq), BF16),
                        pltpu.VMEM((LANES + ones_rows, 2 * tq), F32)],
        compiler_params=_cparams(("arbitrary", "arbitrary", "arbitrary")),
        name="diff_attention",
    )(*args)


def _filter_kernel(z_ref, w1_ref, b1_ref, fr_ref, w2_ref, b2_ref, w3_ref, b3_ref, dec_ref, h_ref, ss_ref):
    hdot = functools.partial(jnp.dot, precision=HIGHEST, preferred_element_type=F32)
    fr = fr_ref[...]
    h = jnp.sin(fr * (hdot(z_ref[...], w1_ref[...]) + b1_ref[...]))
    h = jnp.sin(fr * (hdot(h, w2_ref[...]) + b2_ref[...]))
    h = hdot(h, w3_ref[...]) + b3_ref[...]
    dec = dec_ref[...]
    h = h * jnp.concatenate([dec] * (2 * HY_ORDER), axis=1)
    h_ref[...] = h

    @pl.when(pl.program_id(0) == 0)
    def _():
        ss_ref[...] = jnp.zeros(ss_ref.shape, F32)

    ss_ref[...] += jnp.sum(h * h, axis=0, keepdims=True)


def _hyena_filters(L, w1, b1, freq, w2, b2, w3, b3):
    t = jnp.linspace(0.0, 1.0, L, dtype=F32)[:, None]
    w = (2.0 * math.pi / L) * jnp.arange(L, dtype=F32)[:, None]
    bands = (HY_EMB - 1) // 2
    fb = jnp.linspace(1e-4, bands - 1, bands, dtype=F32)[None, :]
    z = jnp.concatenate([t, jnp.cos(fb * w), -jnp.sin(fb * w)], axis=-1)
    emb_pad = HY_HIDDEN - HY_EMB
    z = jnp.pad(z, ((0, 0), (0, emb_pad)))
    w1p = jnp.pad(w1.astype(F32), ((0, emb_pad), (0, 0)))
    min_decay = math.log(HY_TARGET) / HY_SLOW_DECAY
    max_decay = math.log(HY_TARGET) / HY_FAST_DECAY
    deltas = jnp.abs(jnp.linspace(min_decay, max_decay, HY_WIDTH, dtype=F32))
    decay = jnp.exp(-t * deltas[None, :])
    tl = min(L, 512)
    width = HY_ORDER * 2 * HY_WIDTH
    const = lambda i: (0, 0)
    return pl.pallas_call(
        _filter_kernel,
        grid=(L // tl,),
        in_specs=[pl.BlockSpec((tl, HY_HIDDEN), lambda i: (i, 0)),
                  pl.BlockSpec((HY_HIDDEN, HY_HIDDEN), const), pl.BlockSpec((1, HY_HIDDEN), const),
                  pl.BlockSpec((1, HY_HIDDEN), const),
                  pl.BlockSpec((HY_HIDDEN, HY_HIDDEN), const), pl.BlockSpec((1, HY_HIDDEN), const),
                  pl.BlockSpec((HY_HIDDEN, width), const), pl.BlockSpec((1, width), const),
                  pl.BlockSpec((tl, HY_WIDTH), lambda i: (i, 0))],
        out_specs=[pl.BlockSpec((tl, width), lambda i: (i, 0)), pl.BlockSpec((1, width), const)],
        out_shape=[jax.ShapeDtypeStruct((L, width), F32), jax.ShapeDtypeStruct((1, width), F32)],
        compiler_params=_cparams(("arbitrary",)),
        name="hyena_filters",
    )(z, w1p, b1.astype(F32)[None], freq.astype(F32)[None], w2.astype(F32), b2.astype(F32)[None],
      w3.astype(F32), b3.astype(F32)[None], decay)


def _fft_sizes(L):
    n2 = 128 if L >= 1024 else 32
    n1 = 2 * L // n2
    return n1, n2


@functools.lru_cache(maxsize=None)
def _dft_tables(L):
    n1s, n2s = _fft_sizes(L)
    N = 2 * L
    k1 = np.arange(n1s)
    n1 = np.arange(n1s // 2)
    n2 = np.arange(n2s)
    m = (k1[None, :, None] * (n2s * n1[None, None, :] + n2[:, None, None])) % N
    th = 2.0 * np.pi * m / N
    c, s = np.cos(th), np.sin(th)
    l1 = np.concatenate([np.concatenate([c, s], axis=2), np.concatenate([-s, c], axis=2)], axis=1)
    ct, st = np.swapaxes(c, 1, 2), np.swapaxes(s, 1, 2)
    l1i = np.concatenate([np.concatenate([ct, -st], axis=2), np.concatenate([st, ct], axis=2)], axis=1)
    th3 = 2.0 * np.pi * ((n2[:, None] * n2[None, :]) % n2s) / n2s
    c3, s3 = np.cos(th3), np.sin(th3)
    l3 = np.block([[c3, s3], [-s3, c3]])
    l3i = np.block([[c3, -s3], [s3, c3]])
    return tuple(np.asarray(a, np.float32) for a in (l1, l3, l3i, l1i))


def _ld2(buf, rows):
    return jnp.concatenate([buf[0, rows, :], buf[1, rows, :]], axis=1)


def _st2(buf, rows, val):
    buf[0, rows, :] = val[:, :LANES]
    buf[1, rows, :] = val[:, LANES:]


def _fft_stage1(zr, zi, ar, ai, l1_ref, n1s, n2s):
    half = n1s // 2

    def body(n2, carry):
        src = pl.ds(n2, half, stride=n2s)
        d = jnp.concatenate([_ld2(zr, src), _ld2(zi, src)], axis=0)
        out = _dot(l1_ref[n2], d.astype(BF16))
        dst = pl.ds(n2, n1s, stride=n2s)
        _st2(ar, dst, out[:n1s])
        _st2(ai, dst, out[n1s:])
        return carry

    lax.fori_loop(0, n2s, body, 0, unroll=HY_UNROLL)


def _fft_stage3(ar, ai, l3_ref, k1, n2s):
    r = pl.multiple_of(k1 * n2s, n2s)
    d = jnp.concatenate([_ld2(ar, pl.ds(r, n2s)), _ld2(ai, pl.ds(r, n2s))], axis=0)
    return r, _dot(l3_ref[...], d.astype(BF16))


def _spectrum_kernel(hf_ref, hb_ref, ssf_ref, ssb_ref, l1_ref, l3_ref, gr_ref, gi_ref,
                     zr, zi, ar, ai, *, n1s, n2s):
    nrm = lax.rsqrt(ssf_ref[...] + ssb_ref[...] + EPS)
    hf = hf_ref[...] * nrm
    hb = hb_ref[...] * nrm
    zr[0] = hf + hb
    zr[1] = hf - hb
    zi[...] = jnp.zeros(zi.shape, F32)
    _fft_stage1(zr, zi, ar, ai, l1_ref, n1s, n2s)
    inv_n = 1.0 / (n1s * n2s)

    def body(k1, carry):
        r, z = _fft_stage3(ar, ai, l3_ref, k1, n2s)
        gr_ref[0, pl.ds(r, n2s), :] = z[:n2s, :LANES] * inv_n
        gi_ref[0, pl.ds(r, n2s), :] = z[n2s:, LANES:] * inv_n
        return carry

    lax.fori_loop(0, n1s, body, 0, unroll=HY_UNROLL)


def _hyena_spectra(h, ss):
    L = h.shape[0]
    n1s, n2s = _fft_sizes(L)
    N = 2 * L
    l1, l3, _, _ = _dft_tables(L)
    n_ct = HY_WIDTH // LANES
    fwd = lambda o, ct: (0, o * 2 * n_ct + ct)
    bwd = lambda o, ct: (0, o * 2 * n_ct + n_ct + ct)
    out_spec = pl.BlockSpec((1, N, LANES), lambda o, ct: (o, 0, ct))
    return pl.pallas_call(
        functools.partial(_spectrum_kernel, n1s=n1s, n2s=n2s),
        grid=(HY_ORDER, n_ct),
        in_specs=[pl.BlockSpec((L, LANES), fwd), pl.BlockSpec((L, LANES), bwd),
                  pl.BlockSpec((1, LANES), fwd), pl.BlockSpec((1, LANES), bwd),
                  _single(l1.shape, lambda o, ct: (0, 0, 0)), _single(l3.shape, lambda o, ct: (0, 0))],
        out_specs=[out_spec, out_spec],
        out_shape=[jax.ShapeDtypeStruct((HY_ORDER, N, HY_WIDTH), F32)] * 2,
        scratch_shapes=[pltpu.VMEM((2, L, LANES), F32), pltpu.VMEM((2, L, LANES), F32),
                        pltpu.VMEM((2, N, LANES), F32), pltpu.VMEM((2, N, LANES), F32)],
        compiler_params=_cparams(("arbitrary", "arbitrary")),
        name="hyena_spectra",
    )(h, h, ss, ss, jnp.asarray(l1, BF16), jnp.asarray(l3, BF16))


def _short_conv(u, w_ref, b_ref):
    L = u.shape[0]
    row = lax.broadcasted_iota(jnp.int32, u.shape, 0)
    prev = jnp.where(row == 0, 0.0, pltpu.roll(u, 1, axis=0))
    nxt = jnp.where(row == L - 1, 0.0, pltpu.roll(u, L - 1, axis=0))
    return b_ref[...] + prev * w_ref[0:1, :] + u * w_ref[1:2, :] + nxt * w_ref[2:3, :]


def _conv_kernel(a_ref, x_ref, wa_ref, ba_ref, wx_ref, bx_ref, gr_ref, gi_ref, skip_ref,
                 l1_ref, l3_ref, l3i_ref, l1i_ref, o_ref, zr, zi, ar, ai, *, conv_a, n1s, n2s):
    half = n1s // 2
    slots = ((zr, 0), (zr, 1), (zi, 0), (zi, 1))
    for s, (buf, hi) in enumerate(slots):
        u = a_ref[s].astype(F32)
        buf[hi] = _short_conv(u, wa_ref, ba_ref) if conv_a else u
    _fft_stage1(zr, zi, ar, ai, l1_ref, n1s, n2s)

    def mid(k1, carry):
        r, z = _fft_stage3(ar, ai, l3_ref, k1, n2s)
        g_r = gr_ref[0, pl.ds(r, n2s), :]
        g_i = gi_ref[0, pl.ds(r, n2s), :]
        g_r = jnp.concatenate([g_r, g_r], axis=1)
        g_i = jnp.concatenate([g_i, g_i], axis=1)
        z_r, z_i = z[:n2s], z[n2s:]
        p = jnp.concatenate([z_r * g_r - z_i * g_i, z_r * g_i + z_i * g_r], axis=0)
        b = _dot(l3i_ref[...], p.astype(BF16))
        _st2(ar, pl.ds(r, n2s), b[:n2s])
        _st2(ai, pl.ds(r, n2s), b[n2s:])
        return carry

    lax.fori_loop(0, n1s, mid, 0, unroll=HY_UNROLL)
    skip = skip_ref[0]
    skip = jnp.concatenate([skip, skip], axis=1)

    def last(n2, carry):
        src = pl.ds(n2, n1s, stride=n2s)
        d = jnp.concatenate([_ld2(ar, src), _ld2(ai, src)], axis=0)
        y = _dot(l1i_ref[n2], d.astype(BF16))
        rows = pl.ds(n2, half, stride=n2s)
        _st2(zr, rows, y[:half] + skip * _ld2(zr, rows))
        _st2(zi, rows, y[half:] + skip * _ld2(zi, rows))
        return carry

    lax.fori_loop(0, n2s, last, 0, unroll=HY_UNROLL)
    for s, (buf, hi) in enumerate(slots):
        xg = _short_conv(x_ref[s].astype(F32), wx_ref, bx_ref)
        o_ref[s] = (xg * buf[hi]).astype(o_ref.dtype)


def _hyena_conv(a, a_col, x, x_col, conv_w, conv_b, wa_col, wx_col, g_r, g_i, skip, order, *, conv_a, out_dtype):
    B, L, _ = a.shape
    n1s, n2s = _fft_sizes(L)
    N = 2 * L
    n_ct = HY_WIDTH // LANES
    seqs = 4
    assert B % seqs == 0
    tables = _dft_tables(L)
    t_specs = [_single(t.shape, (lambda b, ct: (0, 0, 0)) if t.ndim == 3 else (lambda b, ct: (0, 0))) for t in tables]
    return pl.pallas_call(
        functools.partial(_conv_kernel, conv_a=conv_a, n1s=n1s, n2s=n2s),
        grid=(n_ct, B // seqs),
        in_specs=[_single((seqs, L, LANES), lambda ct, b: (b, 0, a_col + ct)),
                  _single((seqs, L, LANES), lambda ct, b: (b, 0, x_col + ct)),
                  pl.BlockSpec((3, LANES), lambda ct, b: (0, wa_col + ct)),
                  pl.BlockSpec((1, LANES), lambda ct, b: (0, wa_col + ct)),
                  pl.BlockSpec((3, LANES), lambda ct, b: (0, wx_col + ct)),
                  pl.BlockSpec((1, LANES), lambda ct, b: (0, wx_col + ct)),
                  _single((1, N, LANES), lambda ct, b: (order, 0, ct)),
                  _single((1, N, LANES), lambda ct, b: (order, 0, ct)),
                  pl.BlockSpec((1, 1, LANES), lambda ct, b: (order, 0, ct))] + t_specs,
        out_specs=_single((seqs, L, LANES), lambda ct, b: (b, 0, ct)),
        out_shape=jax.ShapeDtypeStruct((B, L, HY_WIDTH), out_dtype),
        scratch_shapes=[pltpu.VMEM((2, L, LANES), F32), pltpu.VMEM((2, L, LANES), F32),
                        pltpu.VMEM((2, N, LANES), F32), pltpu.VMEM((2, N, LANES), F32)],
        compiler_params=_cparams(("arbitrary", "arbitrary")),
        name="hyena_conv",
    )(a, x, conv_w, conv_b[None], conv_w, conv_b[None], g_r, g_i, skip[:, None],
      *[jnp.asarray(t, BF16) for t in tables])


def _hyena(hy, conv_w, conv_b, filt, skip):
    L = hy.shape[1]
    n_ct = HY_WIDTH // LANES
    h, ss = _hyena_filters(L, *filt)
    g_r, g_i = _hyena_spectra(h, ss)
    conv_w = conv_w.astype(F32)
    conv_b = conv_b.astype(F32)
    skip = skip.astype(F32)
    z1 = _hyena_conv(hy, 0, hy, n_ct, conv_w, conv_b, 0, n_ct, g_r, g_i, skip, 0, conv_a=True, out_dtype=F32)
    return _hyena_conv(z1, 0, hy, 2 * n_ct, conv_w, conv_b, 0, 2 * n_ct, g_r, g_i, skip, 1,
                       conv_a=False, out_dtype=BF16)


def kernel(x, c, ctx, c_ctx, norm_g, w_mod, b_mod, w_in, w_out, q_norm_g, k_norm_g, na_rpb, hy_conv_w, hy_conv_b, hy_filt_w1, hy_filt_b1, hy_filt_freq, hy_filt_w2, hy_filt_b2, hy_filt_w3, hy_filt_b3, hy_skip, diff_lam_q1, diff_lam_k1, diff_lam_q2, diff_lam_k2, diff_subln_g):
    B, S, D = x.shape
    mod_rows = -(-(B + 1) // 8) * 8
    vecs = jnp.concatenate([c, c_ctx[None], jnp.zeros((mod_rows - B - 1, D), F32)], axis=0)
    mods = _modulation(vecs, w_mod, b_mod)
    w_in_b = w_in.astype(BF16)
    w_out_b = w_out.astype(BF16)
    rope_tabs = _rope_tables(S)
    xc = ctx
    for l in range(DEPTH):
        ctx_out = l < DEPTH - 1
        shift, scale, gate = (mods[l, :B, j * D:(j + 1) * D] for j in range(3))
        c_shift, c_scale, c_gate = (jnp.broadcast_to(mods[l, B:B + 1, j * D:(j + 1) * D], (B, D)) for j in range(3))
        odd = l % 2 == 1
        lat = _inproj(x, shift, scale, norm_g[l], w_in_b[l], q_norm_g[l], k_norm_g[l], odd=odd,
                      rope_tabs=rope_tabs if odd else None)
        cx = _inproj(xc, c_shift, c_scale, norm_g[l], w_in_b[l], q_norm_g[l], k_norm_g[l], odd=odd)
        if not odd:
            e = l // 2
            q, k, v, hy, g = lat
            qc, kc, vc, hyc, gc = cx
            filt = (hy_filt_w1[e], hy_filt_b1[e], hy_filt_freq[e], hy_filt_w2[e], hy_filt_b2[e],
                    hy_filt_w3[e], hy_filt_b3[e])
            o_na = _na_attention(q, k, v, kc, vc, na_rpb[e])
            o_hy = _hyena(hy, hy_conv_w[e], hy_conv_b[e], filt, hy_skip[e])
            parts = [o_na, o_hy]
            if ctx_out:
                c_parts = [_ctx_attention(qc, kc, vc), _hyena(hyc, hy_conv_w[e], hy_conv_b[e], filt, hy_skip[e])]
        else:
            o_i = l // 2
            lam_init = 0.8 - 0.6 * math.exp(-0.3 * l)
            lam_vecs = (diff_lam_q1[o_i], diff_lam_k1[o_i], diff_lam_q2[o_i], diff_lam_k2[o_i])
            q, k, v, g = lat
            qc, kc, vc, gc = cx
            k_all = jnp.concatenate([k, kc], axis=1)
            v_all = jnp.concatenate([v, vc], axis=1)
            parts = [_diff_attention(q, k_all, v_all, lam_vecs, diff_subln_g[o_i], lam_init)]
            if ctx_out:
                c_parts = [_diff_attention(qc, kc, vc, lam_vecs, diff_subln_g[o_i], lam_init)]
        x = _outproj(x, gate, g, w_out_b[l], parts)
        if ctx_out:
            xc = _outproj(xc, c_gate, gc, w_out_b[l], c_parts)
    return x
```

```python
import functools
import math

import numpy as np
import jax
import jax.numpy as jnp
from jax import lax
from jax.experimental import pallas as pl
from jax.experimental.pallas import tpu as pltpu

F32 = jnp.float32
BF16 = jnp.bfloat16
HIGHEST = lax.Precision.HIGHEST

D_MODEL = 1024
DEPTH = 4
GRID_W = 64
HEAD_DIM = 64
N_HEADS_NA = 8
NA_WIDTH = N_HEADS_NA * HEAD_DIM
HY_WIDTH = D_MODEL - NA_WIDTH
HY_ORDER = 2
HY_EMB = 33
HY_HIDDEN = 64
HY_FAST_DECAY = 0.3
HY_SLOW_DECAY = 1.5
HY_TARGET = 1e-2
WIN_R = 8
WIN_C = 16
N_HEADS_DIFF = D_MODEL // (2 * HEAD_DIM)
DIFF_QK = N_HEADS_DIFF * 2 * HEAD_DIM
IN_WIDTH = 4 * D_MODEL
EPS = 1e-6
ROPE_BASE = 10000.0

LANES = 128
MXU_DIM = 256
VMEM_LIMIT = 56 * 1024 * 1024
NEG_BIG = -1e30
LOG2E = math.log2(math.e)
ONES_ROWS = 16

NA_Q_ROWS = 8
NA_WIN_ROWS = 16
NA_CHUNK_ROWS = 4
HY_LANES = 2 * LANES
DIFF_BUFS = 4
HY_UNROLL = 8


def _cparams(sem):
    return pltpu.CompilerParams(dimension_semantics=sem, vmem_limit_bytes=VMEM_LIMIT)


def _single(shape, index_map):
    return pl.BlockSpec(shape, index_map, pipeline_mode=pl.Buffered(1))


def _dot(a, b):
    return jnp.dot(a, b, preferred_element_type=F32)


def _dot_nt(a, b):
    return lax.dot_general(a, b, (((1,), (1,)), ((), ())), preferred_element_type=F32)


def _mod_kernel(v_ref, w_ref, b_ref, o_ref):
    v = v_ref[...]
    a = v * jax.nn.sigmoid(v)
    o_ref[0] = jnp.dot(a, w_ref[0], precision=HIGHEST, preferred_element_type=F32) + b_ref[0]


def _modulation(vecs, w_mod, b_mod):
    rows = vecs.shape[0]
    tn = 1024
    return pl.pallas_call(
        _mod_kernel,
        grid=(DEPTH, 3 * D_MODEL // tn),
        in_specs=[pl.BlockSpec((rows, D_MODEL), lambda l, j: (0, 0)),
                  pl.BlockSpec((1, D_MODEL, tn), lambda l, j: (l, 0, j)),
                  pl.BlockSpec((1, 1, tn), lambda l, j: (l, 0, j))],
        out_specs=pl.BlockSpec((1, rows, tn), lambda l, j: (l, 0, j)),
        out_shape=jax.ShapeDtypeStruct((DEPTH, rows, 3 * D_MODEL), F32),
        compiler_params=_cparams(("arbitrary", "arbitrary")),
        name="modulation",
    )(vecs, w_mod, b_mod.reshape(DEPTH, 1, 3 * D_MODEL))


_EVEN_SECTIONS = (("q", 0, NA_WIDTH, "qnorm"), ("k", NA_WIDTH, 2 * NA_WIDTH, "knorm"),
                  ("v", 2 * NA_WIDTH, 3 * NA_WIDTH, "copy"),
                  ("hy", 3 * NA_WIDTH, 3 * NA_WIDTH + 3 * HY_WIDTH, "copy"),
                  ("gate", IN_WIDTH - D_MODEL, IN_WIDTH, "silu"))
_ODD_SECTIONS = (("q", 0, DIFF_QK, "qnorm"), ("k", DIFF_QK, 2 * DIFF_QK, "knorm"),
                 ("v", 2 * DIFF_QK, 3 * DIFF_QK, "copy"),
                 ("gate", IN_WIDTH - D_MODEL, IN_WIDTH, "silu"))


def _rope_chunk(a, cos, sin_signed, low_half):
    up = pltpu.roll(a, LANES - HEAD_DIM // 2, axis=1)
    dn = pltpu.roll(a, HEAD_DIM // 2, axis=1)
    return a * cos + jnp.where(low_half, up, dn) * sin_signed


def _inproj_kernel(*refs, sections, rope):
    x_ref, shift_ref, scale_ref, g_ref, w_ref, qg_ref, kg_ref, e_ref = refs[:8]
    pos = 8
    if rope:
        cos_ref, sin_ref = refs[8:10]
        pos = 10
    out_refs = refs[pos:]
    x = x_ref[0]
    ms = jnp.mean(x * x, axis=-1, keepdims=True)
    h = x * lax.rsqrt(ms + EPS) * g_ref[...] * (1.0 + scale_ref[0]) + shift_ref[0]
    hb = h.astype(BF16)
    if rope:
        cos = cos_ref[...]
        sin_signed = sin_ref[...]
        lane = lax.broadcasted_iota(jnp.int32, cos.shape, 1)
        low_half = (lane % HEAD_DIM) < HEAD_DIM // 2
    for o_ref, (_, lo, hi, kind) in zip(out_refs, sections):
        for c0 in range(lo, hi, MXU_DIM):
            acc = _dot(hb, w_ref[:, c0:c0 + MXU_DIM])
            if kind in ("qnorm", "knorm"):
                gain = qg_ref[...] if kind == "qnorm" else kg_ref[...]
                ss = _dot((acc * acc).astype(BF16), e_ref[...])
                acc = acc * lax.rsqrt(ss * (1.0 / HEAD_DIM) + EPS) * gain
                if rope:
                    acc = jnp.concatenate(
                        [_rope_chunk(acc[:, j:j + LANES], cos, sin_signed, low_half)
                         for j in range(0, MXU_DIM, LANES)], axis=1)
            elif kind == "silu":
                acc = acc * jax.nn.sigmoid(acc)
            o_ref[0, :, c0 - lo:c0 - lo + MXU_DIM] = acc.astype(o_ref.dtype)


def _rope_tables(n_tokens):
    t = jnp.arange(n_tokens, dtype=jnp.int32)
    row = (t // GRID_W).astype(F32)
    col = (t % GRID_W).astype(F32)
    n_freq = HEAD_DIM // 4
    inv = ROPE_BASE ** (-jnp.arange(n_freq, dtype=F32) / n_freq)
    ang = jnp.concatenate([row[:, None] * inv, col[:, None] * inv], axis=-1)
    cos, sin = jnp.cos(ang), jnp.sin(ang)
    cos_t = jnp.concatenate([cos, cos, cos, cos], axis=-1)
    sin_t = jnp.concatenate([-sin, sin, -sin, sin], axis=-1)
    return cos_t, sin_t


def _inproj(x, shift, scale, g, w, qg, kg, *, odd, rope_tabs=None):
    B, S, _ = x.shape
    tm = min(S, 512)
    sections = _ODD_SECTIONS if odd else _EVEN_SECTIONS
    rope = rope_tabs is not None
    head = jnp.arange(MXU_DIM) // HEAD_DIM
    e = (head[:, None] == head[None, :]).astype(BF16)
    q_scale = HEAD_DIM ** -0.5 * LOG2E
    qg_t = jnp.tile(qg.astype(F32), MXU_DIM // HEAD_DIM)[None] * q_scale
    kg_t = jnp.tile(kg.astype(F32), MXU_DIM // HEAD_DIM)[None]
    const = lambda b, i: (0, 0)
    in_specs = [pl.BlockSpec((1, tm, D_MODEL), lambda b, i: (b, i, 0)),
                pl.BlockSpec((1, 1, D_MODEL), lambda b, i: (b, 0, 0)),
                pl.BlockSpec((1, 1, D_MODEL), lambda b, i: (b, 0, 0)),
                pl.BlockSpec((1, D_MODEL), const),
                _single((D_MODEL, IN_WIDTH), const),
                pl.BlockSpec((1, MXU_DIM), const),
                pl.BlockSpec((1, MXU_DIM), const),
                pl.BlockSpec((MXU_DIM, MXU_DIM), const)]
    args = [x, shift[:, None], scale[:, None], g[None], w, qg_t, kg_t, e]
    if rope:
        in_specs += [pl.BlockSpec((tm, LANES), lambda b, i: (i, 0))] * 2
        args += list(rope_tabs)
    out_shape = [jax.ShapeDtypeStruct((B, S, hi - lo), BF16) for _, lo, hi, _ in sections]
    out_specs = [pl.BlockSpec((1, tm, hi - lo), lambda b, i: (b, i, 0)) for _, lo, hi, _ in sections]
    return pl.pallas_call(
        functools.partial(_inproj_kernel, sections=sections, rope=rope),
        grid=(B, S // tm),
        in_specs=in_specs, out_specs=out_specs, out_shape=out_shape,
        compiler_params=_cparams(("arbitrary", "arbitrary")),
        name="inproj_odd" if odd else "inproj_even",
    )(*args)


def _outproj_kernel(*refs, n_parts):
    x_ref, gm_ref, gate_ref, w_ref = refs[:4]
    parts = refs[4:4 + n_parts]
    o_ref = refs[4 + n_parts]
    y = jnp.concatenate([p[0].astype(F32) for p in parts], axis=1) * gate_ref[0].astype(F32)
    o_ref[0] = x_ref[0] + gm_ref[0] * _dot(y.astype(BF16), w_ref[...])


def _outproj(x, gate_mod, gate, w, parts):
    B, S, _ = x.shape
    tm = min(S, 512)
    in_specs = [pl.BlockSpec((1, tm, D_MODEL), lambda b, i: (b, i, 0)),
                pl.BlockSpec((1, 1, D_MODEL), lambda b, i: (b, 0, 0)),
                pl.BlockSpec((1, tm, D_MODEL), lambda b, i: (b, i, 0)),
                _single((D_MODEL, D_MODEL), lambda b, i: (0, 0))]
    in_specs += [pl.BlockSpec((1, tm, p.shape[-1]), lambda b, i: (b, i, 0)) for p in parts]
    return pl.pallas_call(
        functools.partial(_outproj_kernel, n_parts=len(parts)),
        grid=(B, S // tm),
        in_specs=in_specs,
        out_specs=pl.BlockSpec((1, tm, D_MODEL), lambda b, i: (b, i, 0)),
        out_shape=jax.ShapeDtypeStruct((B, S, D_MODEL), F32),
        compiler_params=_cparams(("arbitrary", "arbitrary")),
        name="outproj",
    )(x, gate_mod[:, None], gate, w, *parts)


def _head_masks(shape):
    lane = lax.broadcasted_iota(jnp.int32, shape, 1)
    return lane < HEAD_DIM, lane >= HEAD_DIM


def _split_heads_t(qt):
    row = lax.broadcasted_iota(jnp.int32, qt.shape, 0)
    zero = jnp.zeros_like(qt)
    return jnp.concatenate([jnp.where(row < HEAD_DIM, qt, zero), jnp.where(row >= HEAD_DIM, qt, zero)], axis=1)


def _na_kernel(q_ref, k_ref, vt_ref, kc_ref, vct_ref, bias_ref, o_ref, *scratch, n_rows):
    s_bufs = scratch[:len(scratch) // 2]
    p_bufs = scratch[len(scratch) // 2:]
    i = pl.program_id(2)
    start = jnp.clip(i * NA_Q_ROWS - WIN_R // 2, 0, n_rows - NA_WIN_ROWS)
    chunk = NA_CHUNK_ROWS * GRID_W
    off = pl.multiple_of(start * GRID_W, chunk)
    tq = q_ref.shape[2]
    q2 = _split_heads_t(q_ref[0])
    n_loc = NA_WIN_ROWS // NA_CHUNK_ROWS
    cmax = []
    for t in range(n_loc):
        s = _dot(k_ref[0, pl.ds(off + t * chunk, chunk), :], q2) + bias_ref[0, 0, t * chunk:(t + 1) * chunk, :]
        s_bufs[t][...] = s
        cmax.append(jnp.max(s, axis=0, keepdims=True))
    s = _dot(kc_ref[0], q2)
    s_bufs[n_loc][...] = s
    cmax.append(jnp.max(s, axis=0, keepdims=True))
    m = acc = None
    for t in range(n_loc + 1):
        m_new = cmax[t] if m is None else jnp.maximum(m, cmax[t])
        p_bufs[t][...] = jnp.exp2((s_bufs[t][...] - m_new).astype(BF16))
        vt_t = vt_ref[0, 0, start // NA_CHUNK_ROWS + t] if t < n_loc else vct_ref[0, 0, 0]
        pv = _dot(vt_t, p_bufs[t][...])
        acc = pv if acc is None else jnp.exp2(m - m_new) * acc + pv
        m = m_new
    o = jnp.concatenate([acc[:HEAD_DIM, :tq] / acc[LANES:LANES + 1, :tq],
                         acc[HEAD_DIM:LANES, tq:] / acc[LANES:LANES + 1, tq:]], axis=0)
    o_ref[0] = o.T.astype(o_ref.dtype)


def _rpb_cols_kernel(r_ref, p_ref, m_ref, o_ref):
    gathered = jnp.dot(r_ref[...], p_ref[...], precision=HIGHEST, preferred_element_type=F32)
    o_ref[...] = gathered * LOG2E + m_ref[...]


def _na_bias_table(rpb, n_rows):
    n_heads, n_drow, n_dcol = rpb.shape
    n_blk = n_rows // NA_Q_ROWS
    kr = min(WIN_R, n_rows)
    c = np.arange(GRID_W)
    cs = np.clip(c - WIN_C // 2, 0, GRID_W - WIN_C)
    col_ok = (c[None, :] >= cs[:, None]) & (c[None, :] < cs[:, None] + WIN_C)
    dcol = np.clip(c[None, :] - c[:, None] + WIN_C - 1, 0, 2 * WIN_C - 2)
    pad = -n_dcol % 8
    onehot = (np.arange(n_dcol + pad)[:, None] == dcol.reshape(1, -1)).astype(np.float32)
    col_mask = np.where(col_ok, 0.0, NEG_BIG).astype(np.float32).reshape(1, -1)
    r2 = jnp.pad(rpb.astype(F32).reshape(n_heads * n_drow, n_dcol), ((0, 0), (0, pad)))
    cols = pl.pallas_call(
        _rpb_cols_kernel,
        out_shape=jax.ShapeDtypeStruct((n_heads * n_drow, GRID_W * GRID_W), F32),
        name="rpb_cols",
    )(r2, jnp.asarray(onehot), jnp.asarray(col_mask))
    cols = cols.reshape(n_heads, n_drow, GRID_W, GRID_W)
    masked = jnp.full((n_heads, GRID_W, GRID_W), NEG_BIG, F32)
    tabs = []
    for i in (0, 1, n_blk - 1):
        r0 = i * NA_Q_ROWS
        start = min(max(r0 - WIN_R // 2, 0), n_rows - NA_WIN_ROWS)
        slabs = []
        for a in range(NA_Q_ROWS):
            r = r0 + a
            rs = min(max(r - kr // 2, 0), n_rows - kr)
            for j in range(NA_WIN_ROWS):
                rr = start + j
                slabs.append(cols[:, rr - r + WIN_R - 1] if rs <= rr < rs + kr else masked)
        b = jnp.stack(slabs, axis=1).reshape(n_heads // 2, 2, NA_Q_ROWS, NA_WIN_ROWS, GRID_W, GRID_W)
        b = b.transpose(0, 3, 5, 1, 2, 4)
        tabs.append(b.reshape(n_heads // 2, NA_WIN_ROWS * GRID_W, 2 * NA_Q_ROWS * GRID_W))
    return jnp.stack(tabs, axis=1)


def _values_t(v, chunk):
    B, n, width = v.shape
    vt = v.reshape(B, n // chunk, chunk, width // LANES, LANES).transpose(0, 3, 1, 4, 2)
    return jnp.concatenate([vt, jnp.ones(vt.shape[:3] + (ONES_ROWS, chunk), v.dtype)], axis=3)


def _na_attention(q, k, v, kc, vc, rpb):
    B, S, _ = q.shape
    n_ctx = kc.shape[1]
    n_rows = S // GRID_W
    n_blk = n_rows // NA_Q_ROWS
    assert n_rows % NA_Q_ROWS == 0 and n_rows >= NA_WIN_ROWS
    assert all(x % NA_CHUNK_ROWS == 0 for x in (NA_Q_ROWS, NA_WIN_ROWS, WIN_R // 2, n_rows))
    tq = NA_Q_ROWS * GRID_W
    tk = NA_WIN_ROWS * GRID_W
    chunk = NA_CHUNK_ROWS * GRID_W
    bias = _na_bias_table(rpb, n_rows)
    vt = _values_t(v, chunk)
    vct = _values_t(vc, n_ctx)

    def bias_map(b, hp, i):
        return (hp, jnp.where(i == 0, 0, jnp.where(i == n_blk - 1, 2, 1)), 0, 0)

    return pl.pallas_call(
        functools.partial(_na_kernel, n_rows=n_rows),
        grid=(B, NA_WIDTH // LANES, n_blk),
        in_specs=[pl.BlockSpec((1, LANES, tq), lambda b, hp, i: (b, hp, i)),
                  pl.BlockSpec((1, S, LANES), lambda b, hp, i: (b, 0, hp)),
                  pl.BlockSpec((1, 1) + vt.shape[2:], lambda b, hp, i: (b, hp, 0, 0, 0)),
                  pl.BlockSpec((1, n_ctx, LANES), lambda b, hp, i: (b, 0, hp)),
                  pl.BlockSpec((1, 1) + vct.shape[2:], lambda b, hp, i: (b, hp, 0, 0, 0)),
                  pl.BlockSpec((1, 1, tk, 2 * tq), bias_map)],
        out_specs=pl.BlockSpec((1, tq, LANES), lambda b, hp, i: (b, i, hp)),
        out_shape=jax.ShapeDtypeStruct((B, S, NA_WIDTH), BF16),
        scratch_shapes=[pltpu.VMEM((n, 2 * tq), dt) for dt in (F32, BF16)
                        for n in [chunk] * (NA_WIN_ROWS // NA_CHUNK_ROWS) + [n_ctx]],
        compiler_params=_cparams(("arbitrary", "arbitrary", "arbitrary")),
        name="na_attention",
    )(jnp.swapaxes(q, 1, 2), k, vt, kc, vct, bias)


def _ctx_attn_kernel(q_ref, k_ref, v_ref, o_ref):
    q = q_ref[0]
    k = k_ref[0]
    v = v_ref[0]
    outs = []
    for sel in _head_masks(q.shape):
        qh = jnp.where(sel, q, jnp.zeros_like(q))
        s = _dot_nt(qh, k)
        p = jnp.exp2(s - jnp.max(s, axis=-1, keepdims=True))
        outs.append(_dot(p.astype(BF16), v) / jnp.sum(p, axis=-1, keepdims=True))
    lo_half, _ = _head_masks(outs[0].shape)
    o_ref[0] = jnp.where(lo_half, outs[0], outs[1]).astype(o_ref.dtype)


def _ctx_attention(q, k, v):
    B, n, width = q.shape
    spec = pl.BlockSpec((1, n, LANES), lambda b, hp: (b, 0, hp))
    return pl.pallas_call(
        _ctx_attn_kernel,
        grid=(B, width // LANES),
        in_specs=[spec, spec, spec], out_specs=spec,
        out_shape=jax.ShapeDtypeStruct((B, n, width), BF16),
        compiler_params=_cparams(("arbitrary", "arbitrary")),
        name="ctx_attention",
    )(q, k, v)


def _diff_kernel(lq1_ref, lk1_ref, lq2_ref, lk2_ref, sg_ref, q_ref, k_ref, vt_ref, o_ref, *scratch,
                 tk, n_chunks, lam_init):
    s_bufs = scratch[:DIFF_BUFS]
    p_bufs = scratch[DIFF_BUFS:2 * DIFF_BUFS]
    acc_ref = scratch[2 * DIFF_BUFS]
    tq = q_ref.shape[2]
    q2 = _split_heads_t(q_ref[0])

    def scores(j):
        s = _dot(k_ref[0, j * tk:(j + 1) * tk, :], q2)
        s_bufs[j % DIFF_BUFS][...] = s
        return jnp.max(s, axis=0, keepdims=True)

    ahead = DIFF_BUFS - 1
    cmax = [scores(j) for j in range(min(ahead, n_chunks))]
    m = None
    for j in range(n_chunks):
        if j + ahead < n_chunks:
            cmax.append(scores(j + ahead))
        m_new = cmax[j] if m is None else jnp.maximum(m, cmax[j])
        p_buf = p_bufs[j % DIFF_BUFS]
        p_buf[...] = jnp.exp2((s_bufs[j % DIFF_BUFS][...] - m_new).astype(BF16))
        pv = _dot(vt_ref[0, 0, j], p_buf[...])
        acc_ref[...] = pv if m is None else jnp.exp2(m - m_new) * acc_ref[...] + pv
        m = m_new
    l = acc_ref[LANES:LANES + 1, :]
    acc = acc_ref[:LANES, :]

    lam = (jnp.exp(jnp.sum(lq1_ref[...] * lk1_ref[...], axis=-1, keepdims=True))
           - jnp.exp(jnp.sum(lq2_ref[...] * lk2_ref[...], axis=-1, keepdims=True)) + lam_init)
    o = acc / l
    o = o[:, :tq] - lam * o[:, tq:]
    o = o * lax.rsqrt(jnp.mean(o * o, axis=0, keepdims=True) + EPS) * sg_ref[...]
    o_ref[0] = (o * (1.0 - lam_init)).T.astype(o_ref.dtype)


def _diff_attention(q, k, v, lam_vecs, subln_g, lam_init):
    B, sq, width = q.shape
    sk = k.shape[1]
    n_heads = width // LANES
    tq = min(sq, 512)
    tk = min(sk, 256)
    vt = _values_t(v, tk)
    small = pl.BlockSpec((1, HEAD_DIM), lambda b, h, i: (0, 0))
    in_specs = [small] * 4 + [
        pl.BlockSpec((LANES, 1), lambda b, h, i: (0, 0)),
        pl.BlockSpec((1, LANES, tq), lambda b, h, i: (b, h, i)),
        pl.BlockSpec((1, sk, LANES), lambda b, h, i: (b, 0, h)),
        pl.BlockSpec((1, 1) + vt.shape[2:], lambda b, h, i: (b, h, 0, 0, 0))]
    args = [a.astype(F32)[None] for a in lam_vecs] + [subln_g.astype(F32)[:, None], jnp.swapaxes(q, 1, 2), k, vt]
    return pl.pallas_call(
        functools.partial(_diff_kernel, tk=tk, n_chunks=sk // tk, lam_init=lam_init),
        grid=(B, n_heads, sq // tq),
        in_specs=in_specs,
        out_specs=pl.BlockSpec((1, tq, LANES), lambda b, h, i: (b, i, h)),
        out_shape=jax.ShapeDtypeStruct((B, sq, width), BF16),
        scratch_shapes=[pltpu.VMEM((tk, 2 * tq), F32)] * DIFF_BUFS + [pltpu.VMEM((tk, 2 * tq), BF16)] * DIFF_BUFS
        + [pltpu.VMEM((LANES + ONES_ROWS, 2 * tq), F32)],
        compiler_params=_cparams(("arbitrary", "arbitrary", "arbitrary")),
        name="diff_attention",
    )(*args)


def _filter_kernel(z_ref, w1_ref, b1_ref, fr_ref, w2_ref, b2_ref, w3_ref, b3_ref, dec_ref, h_ref, ss_ref):
    hdot = functools.partial(jnp.dot, precision=HIGHEST, preferred_element_type=F32)
    fr = fr_ref[...]
    h = jnp.sin(fr * (hdot(z_ref[...], w1_ref[...]) + b1_ref[...]))
    h = jnp.sin(fr * (hdot(h, w2_ref[...]) + b2_ref[...]))
    h = hdot(h, w3_ref[...]) + b3_ref[...]
    dec = dec_ref[...]
    h = h * jnp.concatenate([dec] * (2 * HY_ORDER), axis=1)
    h_ref[...] = h

    @pl.when(pl.program_id(0) == 0)
    def _():
        ss_ref[...] = jnp.zeros(ss_ref.shape, F32)

    ss_ref[...] += jnp.sum(h * h, axis=0, keepdims=True)


def _hyena_filters(L, w1, b1, freq, w2, b2, w3, b3):
    t = jnp.linspace(0.0, 1.0, L, dtype=F32)[:, None]
    w = (2.0 * math.pi / L) * jnp.arange(L, dtype=F32)[:, None]
    bands = (HY_EMB - 1) // 2
    fb = jnp.linspace(1e-4, bands - 1, bands, dtype=F32)[None, :]
    z = jnp.concatenate([t, jnp.cos(fb * w), -jnp.sin(fb * w)], axis=-1)
    emb_pad = HY_HIDDEN - HY_EMB
    z = jnp.pad(z, ((0, 0), (0, emb_pad)))
    w1p = jnp.pad(w1.astype(F32), ((0, emb_pad), (0, 0)))
    min_decay = math.log(HY_TARGET) / HY_SLOW_DECAY
    max_decay = math.log(HY_TARGET) / HY_FAST_DECAY
    deltas = jnp.abs(jnp.linspace(min_decay, max_decay, HY_WIDTH, dtype=F32))
    decay = jnp.exp(-t * deltas[None, :])
    tl = min(L, 512)
    width = HY_ORDER * 2 * HY_WIDTH
    const = lambda i: (0, 0)
    return pl.pallas_call(
        _filter_kernel,
        grid=(L // tl,),
        in_specs=[pl.BlockSpec((tl, HY_HIDDEN), lambda i: (i, 0)),
                  pl.BlockSpec((HY_HIDDEN, HY_HIDDEN), const), pl.BlockSpec((1, HY_HIDDEN), const),
                  pl.BlockSpec((1, HY_HIDDEN), const),
                  pl.BlockSpec((HY_HIDDEN, HY_HIDDEN), const), pl.BlockSpec((1, HY_HIDDEN), const),
                  pl.BlockSpec((HY_HIDDEN, width), const), pl.BlockSpec((1, width), const),
                  pl.BlockSpec((tl, HY_WIDTH), lambda i: (i, 0))],
        out_specs=[pl.BlockSpec((tl, width), lambda i: (i, 0)), pl.BlockSpec((1, width), const)],
        out_shape=[jax.ShapeDtypeStruct((L, width), F32), jax.ShapeDtypeStruct((1, width), F32)],
        compiler_params=_cparams(("arbitrary",)),
        name="hyena_filters",
    )(z, w1p, b1.astype(F32)[None], freq.astype(F32)[None], w2.astype(F32), b2.astype(F32)[None],
      w3.astype(F32), b3.astype(F32)[None], decay)


def _fft_sizes(L):
    n2 = 128 if L >= 1024 else 32
    n1 = 2 * L // n2
    return n1, n2


@functools.lru_cache(maxsize=None)
def _dft_tables(L):
    n1s, n2s = _fft_sizes(L)
    N = 2 * L
    k1 = np.arange(n1s)
    n1 = np.arange(n1s // 2)
    n2 = np.arange(n2s)
    m = (k1[None, :, None] * (n2s * n1[None, None, :] + n2[:, None, None])) % N
    th = 2.0 * np.pi * m / N
    c, s = np.cos(th), np.sin(th)
    l1 = np.concatenate([np.concatenate([c, s], axis=2), np.concatenate([-s, c], axis=2)], axis=1)
    ct, st = np.swapaxes(c, 1, 2), np.swapaxes(s, 1, 2)
    l1i = np.concatenate([np.concatenate([ct, -st], axis=2), np.concatenate([st, ct], axis=2)], axis=1)
    th3 = 2.0 * np.pi * ((n2[:, None] * n2[None, :]) % n2s) / n2s
    c3, s3 = np.cos(th3), np.sin(th3)
    l3 = np.block([[c3, s3], [-s3, c3]])
    l3i = np.block([[c3, -s3], [s3, c3]])
    return tuple(np.asarray(a, np.float32) for a in (l1, l3, l3i, l1i))


def _ld2(buf, rows):
    return jnp.concatenate([buf[0, rows, :], buf[1, rows, :]], axis=1)


def _st2(buf, rows, val):
    buf[0, rows, :] = val[:, :LANES]
    buf[1, rows, :] = val[:, LANES:]


def _fft_stage1(zr, zi, ar, ai, l1_ref, n1s, n2s):
    half = n1s // 2

    def body(n2, carry):
        src = pl.ds(n2, half, stride=n2s)
        d = jnp.concatenate([_ld2(zr, src), _ld2(zi, src)], axis=0)
        out = _dot(l1_ref[n2], d.astype(BF16))
        dst = pl.ds(n2, n1s, stride=n2s)
        _st2(ar, dst, out[:n1s])
        _st2(ai, dst, out[n1s:])
        return carry

    lax.fori_loop(0, n2s, body, 0, unroll=HY_UNROLL)


def _fft_stage3(ar, ai, l3_ref, k1, n2s):
    r = pl.multiple_of(k1 * n2s, n2s)
    d = jnp.concatenate([_ld2(ar, pl.ds(r, n2s)), _ld2(ai, pl.ds(r, n2s))], axis=0)
    return r, _dot(l3_ref[...], d.astype(BF16))


def _spectrum_kernel(hf_ref, hb_ref, ssf_ref, ssb_ref, l1_ref, l3_ref, gr_ref, gi_ref,
                     zr, zi, ar, ai, *, n1s, n2s):
    nrm = lax.rsqrt(ssf_ref[...] + ssb_ref[...] + EPS)
    hf = hf_ref[...] * nrm
    hb = hb_ref[...] * nrm
    zr[0] = hf + hb
    zr[1] = hf - hb
    zi[...] = jnp.zeros(zi.shape, F32)
    _fft_stage1(zr, zi, ar, ai, l1_ref, n1s, n2s)
    inv_n = 1.0 / (n1s * n2s)

    def body(k1, carry):
        r, z = _fft_stage3(ar, ai, l3_ref, k1, n2s)
        gr_ref[0, pl.ds(r, n2s), :] = z[:n2s, :LANES] * inv_n
        gi_ref[0, pl.ds(r, n2s), :] = z[n2s:, LANES:] * inv_n
        return carry

    lax.fori_loop(0, n1s, body, 0, unroll=HY_UNROLL)


def _hyena_spectra(h, ss):
    L = h.shape[0]
    n1s, n2s = _fft_sizes(L)
    N = 2 * L
    l1, l3, _, _ = _dft_tables(L)
    n_ct = HY_WIDTH // LANES
    fwd = lambda o, ct: (0, o * 2 * n_ct + ct)
    bwd = lambda o, ct: (0, o * 2 * n_ct + n_ct + ct)
    out_spec = pl.BlockSpec((1, N, LANES), lambda o, ct: (o, 0, ct))
    return pl.pallas_call(
        functools.partial(_spectrum_kernel, n1s=n1s, n2s=n2s),
        grid=(HY_ORDER, n_ct),
        in_specs=[pl.BlockSpec((L, LANES), fwd), pl.BlockSpec((L, LANES), bwd),
                  pl.BlockSpec((1, LANES), fwd), pl.BlockSpec((1, LANES), bwd),
                  _single(l1.shape, lambda o, ct: (0, 0, 0)), _single(l3.shape, lambda o, ct: (0, 0))],
        out_specs=[out_spec, out_spec],
        out_shape=[jax.ShapeDtypeStruct((HY_ORDER, N, HY_WIDTH), F32)] * 2,
        scratch_shapes=[pltpu.VMEM((2, L, LANES), F32), pltpu.VMEM((2, L, LANES), F32),
                        pltpu.VMEM((2, N, LANES), F32), pltpu.VMEM((2, N, LANES), F32)],
        compiler_params=_cparams(("arbitrary", "arbitrary")),
        name="hyena_spectra",
    )(h, h, ss, ss, jnp.asarray(l1, BF16), jnp.asarray(l3, BF16))


def _short_conv(u, w_ref, b_ref):
    L = u.shape[0]
    row = lax.broadcasted_iota(jnp.int32, u.shape, 0)
    prev = jnp.where(row == 0, 0.0, pltpu.roll(u, 1, axis=0))
    nxt = jnp.where(row == L - 1, 0.0, pltpu.roll(u, L - 1, axis=0))
    return b_ref[...] + prev * w_ref[0:1, :] + u * w_ref[1:2, :] + nxt * w_ref[2:3, :]


def _conv_kernel(a_ref, x_ref, wa_ref, ba_ref, wx_ref, bx_ref, gr_ref, gi_ref, skip_ref,
                 l1_ref, l3_ref, l3i_ref, l1i_ref, o_ref, zr, zi, ar, ai, *, conv_a, n1s, n2s):
    half = n1s // 2
    slots = ((zr, 0), (zr, 1), (zi, 0), (zi, 1))
    for s, (buf, hi) in enumerate(slots):
        u = a_ref[s].astype(F32)
        buf[hi] = _short_conv(u, wa_ref, ba_ref) if conv_a else u
    _fft_stage1(zr, zi, ar, ai, l1_ref, n1s, n2s)

    def mid(k1, carry):
        r, z = _fft_stage3(ar, ai, l3_ref, k1, n2s)
        g_r = gr_ref[0, pl.ds(r, n2s), :]
        g_i = gi_ref[0, pl.ds(r, n2s), :]
        g_r = jnp.concatenate([g_r, g_r], axis=1)
        g_i = jnp.concatenate([g_i, g_i], axis=1)
        z_r, z_i = z[:n2s], z[n2s:]
        p = jnp.concatenate([z_r * g_r - z_i * g_i, z_r * g_i + z_i * g_r], axis=0)
        b = _dot(l3i_ref[...], p.astype(BF16))
        _st2(ar, pl.ds(r, n2s), b[:n2s])
        _st2(ai, pl.ds(r, n2s), b[n2s:])
        return carry

    lax.fori_loop(0, n1s, mid, 0, unroll=HY_UNROLL)
    skip = skip_ref[0]
    skip = jnp.concatenate([skip, skip], axis=1)

    def last(n2, carry):
        src = pl.ds(n2, n1s, stride=n2s)
        d = jnp.concatenate([_ld2(ar, src), _ld2(ai, src)], axis=0)
        y = _dot(l1i_ref[n2], d.astype(BF16))
        rows = pl.ds(n2, half, stride=n2s)
        _st2(zr, rows, y[:half] + skip * _ld2(zr, rows))
        _st2(zi, rows, y[half:] + skip * _ld2(zi, rows))
        return carry

    lax.fori_loop(0, n2s, last, 0, unroll=HY_UNROLL)
    for s, (buf, hi) in enumerate(slots):
        xg = _short_conv(x_ref[s].astype(F32), wx_ref, bx_ref)
        o_ref[s] = (xg * buf[hi]).astype(o_ref.dtype)


def _hyena_conv(a, a_col, x, x_col, conv_w, conv_b, wa_col, wx_col, g_r, g_i, skip, order, *, conv_a, out_dtype):
    B, L, _ = a.shape
    n1s, n2s = _fft_sizes(L)
    N = 2 * L
    n_ct = HY_WIDTH // LANES
    seqs = 4
    assert B % seqs == 0
    tables = _dft_tables(L)
    t_specs = [_single(t.shape, (lambda b, ct: (0, 0, 0)) if t.ndim == 3 else (lambda b, ct: (0, 0))) for t in tables]
    return pl.pallas_call(
        functools.partial(_conv_kernel, conv_a=conv_a, n1s=n1s, n2s=n2s),
        grid=(n_ct, B // seqs),
        in_specs=[_single((seqs, L, LANES), lambda ct, b: (b, 0, a_col + ct)),
                  _single((seqs, L, LANES), lambda ct, b: (b, 0, x_col + ct)),
                  pl.BlockSpec((3, LANES), lambda ct, b: (0, wa_col + ct)),
                  pl.BlockSpec((1, LANES), lambda ct, b: (0, wa_col + ct)),
                  pl.BlockSpec((3, LANES), lambda ct, b: (0, wx_col + ct)),
                  pl.BlockSpec((1, LANES), lambda ct, b: (0, wx_col + ct)),
                  _single((1, N, LANES), lambda ct, b: (order, 0, ct)),
                  _single((1, N, LANES), lambda ct, b: (order, 0, ct)),
                  pl.BlockSpec((1, 1, LANES), lambda ct, b: (order, 0, ct))] + t_specs,
        out_specs=_single((seqs, L, LANES), lambda ct, b: (b, 0, ct)),
        out_shape=jax.ShapeDtypeStruct((B, L, HY_WIDTH), out_dtype),
        scratch_shapes=[pltpu.VMEM((2, L, LANES), F32), pltpu.VMEM((2, L, LANES), F32),
                        pltpu.VMEM((2, N, LANES), F32), pltpu.VMEM((2, N, LANES), F32)],
        compiler_params=_cparams(("arbitrary", "arbitrary")),
        name="hyena_conv",
    )(a, x, conv_w, conv_b[None], conv_w, conv_b[None], g_r, g_i, skip[:, None],
      *[jnp.asarray(t, BF16) for t in tables])


def _hyena(hy, conv_w, conv_b, filt, skip):
    L = hy.shape[1]
    n_ct = HY_WIDTH // LANES
    h, ss = _hyena_filters(L, *filt)
    g_r, g_i = _hyena_spectra(h, ss)
    conv_w = conv_w.astype(F32)
    conv_b = conv_b.astype(F32)
    skip = skip.astype(F32)
    z1 = _hyena_conv(hy, 0, hy, n_ct, conv_w, conv_b, 0, n_ct, g_r, g_i, skip, 0, conv_a=True, out_dtype=F32)
    return _hyena_conv(z1, 0, hy, 2 * n_ct, conv_w, conv_b, 0, 2 * n_ct, g_r, g_i, skip, 1,
                       conv_a=False, out_dtype=BF16)


def kernel(x, c, ctx, c_ctx, norm_g, w_mod, b_mod, w_in, w_out, q_norm_g, k_norm_g, na_rpb, hy_conv_w, hy_conv_b, hy_filt_w1, hy_filt_b1, hy_filt_freq, hy_filt_w2, hy_filt_b2, hy_filt_w3, hy_filt_b3, hy_skip, diff_lam_q1, diff_lam_k1, diff_lam_q2, diff_lam_k2, diff_subln_g):
    B, S, D = x.shape
    mod_rows = -(-(B + 1) // 8) * 8
    vecs = jnp.concatenate([c, c_ctx[None], jnp.zeros((mod_rows - B - 1, D), F32)], axis=0)
    mods = _modulation(vecs, w_mod, b_mod)
    w_in_b = w_in.astype(BF16)
    w_out_b = w_out.astype(BF16)
    rope_tabs = _rope_tables(S)
    xc = ctx
    for l in range(DEPTH):
        ctx_out = l < DEPTH - 1
        shift, scale, gate = (mods[l, :B, j * D:(j + 1) * D] for j in range(3))
        c_shift, c_scale, c_gate = (jnp.broadcast_to(mods[l, B:B + 1, j * D:(j + 1) * D], (B, D)) for j in range(3))
        odd = l % 2 == 1
        lat = _inproj(x, shift, scale, norm_g[l], w_in_b[l], q_norm_g[l], k_norm_g[l], odd=odd,
                      rope_tabs=rope_tabs if odd else None)
        cx = _inproj(xc, c_shift, c_scale, norm_g[l], w_in_b[l], q_norm_g[l], k_norm_g[l], odd=odd)
        if not odd:
            e = l // 2
            q, k, v, hy, g = lat
            qc, kc, vc, hyc, gc = cx
            filt = (hy_filt_w1[e], hy_filt_b1[e], hy_filt_freq[e], hy_filt_w2[e], hy_filt_b2[e],
                    hy_filt_w3[e], hy_filt_b3[e])
            o_na = _na_attention(q, k, v, kc, vc, na_rpb[e])
            o_hy = _hyena(hy, hy_conv_w[e], hy_conv_b[e], filt, hy_skip[e])
            parts = [o_na, o_hy]
            if ctx_out:
                c_parts = [_ctx_attention(qc, kc, vc), _hyena(hyc, hy_conv_w[e], hy_conv_b[e], filt, hy_skip[e])]
        else:
            o_i = l // 2
            lam_init = 0.8 - 0.6 * math.exp(-0.3 * l)
            lam_vecs = (diff_lam_q1[o_i], diff_lam_k1[o_i], diff_lam_q2[o_i], diff_lam_k2[o_i])
            q, k, v, g = lat
            qc, kc, vc, gc = cx
            k_all = jnp.concatenate([k, kc], axis=1)
            v_all = jnp.concatenate([v, vc], axis=1)
            parts = [_diff_attention(q, k_all, v_all, lam_vecs, diff_subln_g[o_i], lam_init)]
            if ctx_out:
                c_parts = [_diff_attention(qc, kc, vc, lam_vecs, diff_subln_g[o_i], lam_init)]
        x = _outproj(x, gate, g, w_out_b[l], parts)
        if ctx_out:
            xc = _outproj(xc, c_gate, gc, w_out_b[l], c_parts)
    return x
```

```python
import functools
import math

import numpy as np
import jax
import jax.numpy as jnp
from jax import lax
from jax.experimental import pallas as pl
from jax.experimental.pallas import tpu as pltpu

F32 = jnp.float32
BF16 = jnp.bfloat16
HIGHEST = lax.Precision.HIGHEST

D_MODEL = 1024
DEPTH = 4
GRID_W = 64
HEAD_DIM = 64
N_HEADS_NA = 8
NA_WIDTH = N_HEADS_NA * HEAD_DIM
HY_WIDTH = D_MODEL - NA_WIDTH
HY_ORDER = 2
HY_EMB = 33
HY_HIDDEN = 64
HY_FAST_DECAY = 0.3
HY_SLOW_DECAY = 1.5
HY_TARGET = 1e-2
WIN_R = 8
WIN_C = 16
N_HEADS_DIFF = D_MODEL // (2 * HEAD_DIM)
DIFF_QK = N_HEADS_DIFF * 2 * HEAD_DIM
IN_WIDTH = 4 * D_MODEL
EPS = 1e-6
ROPE_BASE = 10000.0

LANES = 128
MXU_DIM = 256
VMEM_LIMIT = 56 * 1024 * 1024
NEG_BIG = -1e30
LOG2E = math.log2(math.e)
ONES_ROWS = 16

NA_Q_ROWS = 8
NA_WIN_ROWS = 16
NA_CHUNK_ROWS = 4
HY_LANES = 2 * LANES
DIFF_BUFS = 4
HY_UNROLL = 8


def _cparams(sem):
    return pltpu.CompilerParams(dimension_semantics=sem, vmem_limit_bytes=VMEM_LIMIT)


def _single(shape, index_map):
    return pl.BlockSpec(shape, index_map, pipeline_mode=pl.Buffered(1))


def _dot(a, b):
    return jnp.dot(a, b, preferred_element_type=F32)


def _dot_nt(a, b):
    return lax.dot_general(a, b, (((1,), (1,)), ((), ())), preferred_element_type=F32)


def _dot_tn(a, b):
    return lax.dot_general(a, b, (((0,), (0,)), ((), ())), preferred_element_type=F32)


def _mod_kernel(v_ref, w_ref, b_ref, o_ref):
    v = v_ref[...]
    a = v * jax.nn.sigmoid(v)
    o_ref[0] = jnp.dot(a, w_ref[0], precision=HIGHEST, preferred_element_type=F32) + b_ref[0]


def _modulation(vecs, w_mod, b_mod):
    rows = vecs.shape[0]
    tn = 1024
    return pl.pallas_call(
        _mod_kernel,
        grid=(DEPTH, 3 * D_MODEL // tn),
        in_specs=[pl.BlockSpec((rows, D_MODEL), lambda l, j: (0, 0)),
                  pl.BlockSpec((1, D_MODEL, tn), lambda l, j: (l, 0, j)),
                  pl.BlockSpec((1, 1, tn), lambda l, j: (l, 0, j))],
        out_specs=pl.BlockSpec((1, rows, tn), lambda l, j: (l, 0, j)),
        out_shape=jax.ShapeDtypeStruct((DEPTH, rows, 3 * D_MODEL), F32),
        compiler_params=_cparams(("arbitrary", "arbitrary")),
        name="modulation",
    )(vecs, w_mod, b_mod.reshape(DEPTH, 1, 3 * D_MODEL))


_EVEN_SECTIONS = (("q", 0, NA_WIDTH, "qnorm"), ("k", NA_WIDTH, 2 * NA_WIDTH, "knorm"),
                  ("v", 2 * NA_WIDTH, 3 * NA_WIDTH, "copy"),
                  ("hy", 3 * NA_WIDTH, 3 * NA_WIDTH + 3 * HY_WIDTH, "copy"),
                  ("gate", IN_WIDTH - D_MODEL, IN_WIDTH, "silu"))
_ODD_SECTIONS = (("q", 0, DIFF_QK, "qnorm"), ("k", DIFF_QK, 2 * DIFF_QK, "knorm"),
                 ("v", 2 * DIFF_QK, 3 * DIFF_QK, "copy"),
                 ("gate", IN_WIDTH - D_MODEL, IN_WIDTH, "silu"))


def _rope_chunk(a, cos, sin_signed, low_half):
    up = pltpu.roll(a, LANES - HEAD_DIM // 2, axis=1)
    dn = pltpu.roll(a, HEAD_DIM // 2, axis=1)
    return a * cos + jnp.where(low_half, up, dn) * sin_signed


def _inproj_kernel(*refs, sections, rope):
    x_ref, shift_ref, scale_ref, g_ref, w_ref, qg_ref, kg_ref, e_ref = refs[:8]
    pos = 8
    if rope:
        cos_ref, sin_ref = refs[8:10]
        pos = 10
    out_refs = refs[pos:]
    x = x_ref[0]
    ms = jnp.mean(x * x, axis=-1, keepdims=True)
    h = x * lax.rsqrt(ms + EPS) * g_ref[...] * (1.0 + scale_ref[0]) + shift_ref[0]
    hb = h.astype(BF16)
    if rope:
        cos = cos_ref[...]
        sin_signed = sin_ref[...]
        lane = lax.broadcasted_iota(jnp.int32, cos.shape, 1)
        low_half = (lane % HEAD_DIM) < HEAD_DIM // 2
    for o_ref, (_, lo, hi, kind) in zip(out_refs, sections):
        for c0 in range(lo, hi, MXU_DIM):
            acc = _dot(hb, w_ref[:, c0:c0 + MXU_DIM])
            if kind in ("qnorm", "knorm"):
                gain = qg_ref[...] if kind == "qnorm" else kg_ref[...]
                ss = _dot((acc * acc).astype(BF16), e_ref[...])
                acc = acc * lax.rsqrt(ss * (1.0 / HEAD_DIM) + EPS) * gain
                if rope:
                    acc = jnp.concatenate(
                        [_rope_chunk(acc[:, j:j + LANES], cos, sin_signed, low_half)
                         for j in range(0, MXU_DIM, LANES)], axis=1)
            elif kind == "silu":
                acc = acc * jax.nn.sigmoid(acc)
            o_ref[0, :, c0 - lo:c0 - lo + MXU_DIM] = acc.astype(o_ref.dtype)


def _rope_tables(n_tokens):
    t = jnp.arange(n_tokens, dtype=jnp.int32)
    row = (t // GRID_W).astype(F32)
    col = (t % GRID_W).astype(F32)
    n_freq = HEAD_DIM // 4
    inv = ROPE_BASE ** (-jnp.arange(n_freq, dtype=F32) / n_freq)
    ang = jnp.concatenate([row[:, None] * inv, col[:, None] * inv], axis=-1)
    cos, sin = jnp.cos(ang), jnp.sin(ang)
    cos_t = jnp.concatenate([cos, cos, cos, cos], axis=-1)
    sin_t = jnp.concatenate([-sin, sin, -sin, sin], axis=-1)
    return cos_t, sin_t


def _inproj(x, shift, scale, g, w, qg, kg, *, odd, rope_tabs=None):
    B, S, _ = x.shape
    tm = min(S, 512)
    sections = _ODD_SECTIONS if odd else _EVEN_SECTIONS
    rope = rope_tabs is not None
    head = jnp.arange(MXU_DIM) // HEAD_DIM
    e = (head[:, None] == head[None, :]).astype(BF16)
    q_scale = HEAD_DIM ** -0.5 * LOG2E
    qg_t = jnp.tile(qg.astype(F32), MXU_DIM // HEAD_DIM)[None] * q_scale
    kg_t = jnp.tile(kg.astype(F32), MXU_DIM // HEAD_DIM)[None]
    const = lambda b, i: (0, 0)
    in_specs = [pl.BlockSpec((1, tm, D_MODEL), lambda b, i: (b, i, 0)),
                pl.BlockSpec((1, 1, D_MODEL), lambda b, i: (b, 0, 0)),
                pl.BlockSpec((1, 1, D_MODEL), lambda b, i: (b, 0, 0)),
                pl.BlockSpec((1, D_MODEL), const),
                _single((D_MODEL, IN_WIDTH), const),
                pl.BlockSpec((1, MXU_DIM), const),
                pl.BlockSpec((1, MXU_DIM), const),
                pl.BlockSpec((MXU_DIM, MXU_DIM), const)]
    args = [x, shift[:, None], scale[:, None], g[None], w, qg_t, kg_t, e]
    if rope:
        in_specs += [pl.BlockSpec((tm, LANES), lambda b, i: (i, 0))] * 2
        args += list(rope_tabs)
    out_shape = [jax.ShapeDtypeStruct((B, S, hi - lo), BF16) for _, lo, hi, _ in sections]
    out_specs = [pl.BlockSpec((1, tm, hi - lo), lambda b, i: (b, i, 0)) for _, lo, hi, _ in sections]
    return pl.pallas_call(
        functools.partial(_inproj_kernel, sections=sections, rope=rope),
        grid=(B, S // tm),
        in_specs=in_specs, out_specs=out_specs, out_shape=out_shape,
        compiler_params=_cparams(("arbitrary", "arbitrary")),
        name="inproj_odd" if odd else "inproj_even",
    )(*args)


def _outproj_kernel(*refs, n_parts):
    x_ref, gm_ref, gate_ref, w_ref = refs[:4]
    parts = refs[4:4 + n_parts]
    o_ref = refs[4 + n_parts]
    y = jnp.concatenate([p[0].astype(F32) for p in parts], axis=1) * gate_ref[0].astype(F32)
    o_ref[0] = x_ref[0] + gm_ref[0] * _dot(y.astype(BF16), w_ref[...])


def _outproj(x, gate_mod, gate, w, parts):
    B, S, _ = x.shape
    tm = min(S, 512)
    in_specs = [pl.BlockSpec((1, tm, D_MODEL), lambda b, i: (b, i, 0)),
                pl.BlockSpec((1, 1, D_MODEL), lambda b, i: (b, 0, 0)),
                pl.BlockSpec((1, tm, D_MODEL), lambda b, i: (b, i, 0)),
                _single((D_MODEL, D_MODEL), lambda b, i: (0, 0))]
    in_specs += [pl.BlockSpec((1, tm, p.shape[-1]), lambda b, i: (b, i, 0)) for p in parts]
    return pl.pallas_call(
        functools.partial(_outproj_kernel, n_parts=len(parts)),
        grid=(B, S // tm),
        in_specs=in_specs,
        out_specs=pl.BlockSpec((1, tm, D_MODEL), lambda b, i: (b, i, 0)),
        out_shape=jax.ShapeDtypeStruct((B, S, D_MODEL), F32),
        compiler_params=_cparams(("arbitrary", "arbitrary")),
        name="outproj",
    )(x, gate_mod[:, None], gate, w, *parts)


def _head_masks(shape):
    lane = lax.broadcasted_iota(jnp.int32, shape, 1)
    return lane < HEAD_DIM, lane >= HEAD_DIM


def _split_heads_t(qt):
    row = lax.broadcasted_iota(jnp.int32, qt.shape, 0)
    zero = jnp.zeros_like(qt)
    return jnp.concatenate([jnp.where(row < HEAD_DIM, qt, zero), jnp.where(row >= HEAD_DIM, qt, zero)], axis=1)


def _na_kernel(q_ref, k_ref, v_ref, kc_ref, vc_ref, bias_ref, o_ref, *scratch, n_rows):
    s_bufs = scratch[:len(scratch) // 2]
    p_bufs = scratch[len(scratch) // 2:]
    i = pl.program_id(2)
    start = jnp.clip(i * NA_Q_ROWS - WIN_R // 2, 0, n_rows - NA_WIN_ROWS)
    chunk = NA_CHUNK_ROWS * GRID_W
    off = pl.multiple_of(start * GRID_W, chunk)
    tq = q_ref.shape[1]
    q2 = _split_heads_t(q_ref[0].T)
    n_loc = NA_WIN_ROWS // NA_CHUNK_ROWS
    cmax = []
    for t in range(n_loc):
        s = _dot(k_ref[0, pl.ds(off + t * chunk, chunk), :], q2) + bias_ref[0, 0, t * chunk:(t + 1) * chunk, :]
        s_bufs[t][...] = s
        cmax.append(jnp.max(s, axis=0, keepdims=True))
    s = _dot(kc_ref[0], q2)
    s_bufs[n_loc][...] = s
    cmax.append(jnp.max(s, axis=0, keepdims=True))
    m = acc = None
    for t in range(n_loc + 1):
        m_new = cmax[t] if m is None else jnp.maximum(m, cmax[t])
        p_bufs[t][...] = jnp.exp2((s_bufs[t][...] - m_new).astype(BF16))
        v_t = v_ref[0, pl.ds(off + t * chunk, chunk), :] if t < n_loc else vc_ref[0]
        ones = jnp.ones((ONES_ROWS, v_t.shape[0]), v_t.dtype)
        pv = _dot(jnp.concatenate([v_t.T, ones], axis=0), p_bufs[t][...])
        acc = pv if acc is None else jnp.exp2(m - m_new) * acc + pv
        m = m_new
    o = jnp.concatenate([acc[:HEAD_DIM, :tq] / acc[LANES:LANES + 1, :tq],
                         acc[HEAD_DIM:LANES, tq:] / acc[LANES:LANES + 1, tq:]], axis=0)
    o_ref[0] = o.T.astype(o_ref.dtype)


def _rpb_cols_kernel(r_ref, p_ref, m_ref, o_ref):
    gathered = jnp.dot(r_ref[...], p_ref[...], precision=HIGHEST, preferred_element_type=F32)
    o_ref[...] = gathered * LOG2E + m_ref[...]


def _na_bias_table(rpb, n_rows):
    n_heads, n_drow, n_dcol = rpb.shape
    n_blk = n_rows // NA_Q_ROWS
    kr = min(WIN_R, n_rows)
    c = np.arange(GRID_W)
    cs = np.clip(c - WIN_C // 2, 0, GRID_W - WIN_C)
    col_ok = (c[None, :] >= cs[:, None]) & (c[None, :] < cs[:, None] + WIN_C)
    dcol = np.clip(c[None, :] - c[:, None] + WIN_C - 1, 0, 2 * WIN_C - 2)
    pad = -n_dcol % 8
    onehot = (np.arange(n_dcol + pad)[:, None] == dcol.reshape(1, -1)).astype(np.float32)
    col_mask = np.where(col_ok, 0.0, NEG_BIG).astype(np.float32).reshape(1, -1)
    r2 = jnp.pad(rpb.astype(F32).reshape(n_heads * n_drow, n_dcol), ((0, 0), (0, pad)))
    cols = pl.pallas_call(
        _rpb_cols_kernel,
        out_shape=jax.ShapeDtypeStruct((n_heads * n_drow, GRID_W * GRID_W), F32),
        name="rpb_cols",
    )(r2, jnp.asarray(onehot), jnp.asarray(col_mask))
    cols = cols.reshape(n_heads, n_drow, GRID_W, GRID_W)
    masked = jnp.full((n_heads, GRID_W, GRID_W), NEG_BIG, F32)
    tabs = []
    for i in (0, 1, n_blk - 1):
        r0 = i * NA_Q_ROWS
        start = min(max(r0 - WIN_R // 2, 0), n_rows - NA_WIN_ROWS)
        slabs = []
        for a in range(NA_Q_ROWS):
            r = r0 + a
            rs = min(max(r - kr // 2, 0), n_rows - kr)
            for j in range(NA_WIN_ROWS):
                rr = start + j
                slabs.append(cols[:, rr - r + WIN_R - 1] if rs <= rr < rs + kr else masked)
        b = jnp.stack(slabs, axis=1).reshape(n_heads // 2, 2, NA_Q_ROWS, NA_WIN_ROWS, GRID_W, GRID_W)
        b = b.transpose(0, 3, 5, 1, 2, 4)
        tabs.append(b.reshape(n_heads // 2, NA_WIN_ROWS * GRID_W, 2 * NA_Q_ROWS * GRID_W))
    return jnp.stack(tabs, axis=1)


def _na_attention(q, k, v, kc, vc, rpb):
    B, S, _ = q.shape
    n_ctx = kc.shape[1]
    n_rows = S // GRID_W
    n_blk = n_rows // NA_Q_ROWS
    assert n_rows % NA_Q_ROWS == 0 and n_rows >= NA_WIN_ROWS
    assert all(x % NA_CHUNK_ROWS == 0 for x in (NA_Q_ROWS, NA_WIN_ROWS, WIN_R // 2, n_rows))
    tq = NA_Q_ROWS * GRID_W
    tk = NA_WIN_ROWS * GRID_W
    chunk = NA_CHUNK_ROWS * GRID_W
    bias = _na_bias_table(rpb, n_rows)

    def bias_map(b, hp, i):
        return (hp, jnp.where(i == 0, 0, jnp.where(i == n_blk - 1, 2, 1)), 0, 0)

    return pl.pallas_call(
        functools.partial(_na_kernel, n_rows=n_rows),
        grid=(B, NA_WIDTH // LANES, n_blk),
        in_specs=[pl.BlockSpec((1, tq, LANES), lambda b, hp, i: (b, i, hp)),
                  pl.BlockSpec((1, S, LANES), lambda b, hp, i: (b, 0, hp)),
                  pl.BlockSpec((1, S, LANES), lambda b, hp, i: (b, 0, hp)),
                  pl.BlockSpec((1, n_ctx, LANES), lambda b, hp, i: (b, 0, hp)),
                  pl.BlockSpec((1, n_ctx, LANES), lambda b, hp, i: (b, 0, hp)),
                  pl.BlockSpec((1, 1, tk, 2 * tq), bias_map)],
        out_specs=pl.BlockSpec((1, tq, LANES), lambda b, hp, i: (b, i, hp)),
        out_shape=jax.ShapeDtypeStruct((B, S, NA_WIDTH), BF16),
        scratch_shapes=[pltpu.VMEM((n, 2 * tq), dt) for dt in (F32, BF16)
                        for n in [chunk] * (NA_WIN_ROWS // NA_CHUNK_ROWS) + [n_ctx]],
        compiler_params=_cparams(("arbitrary", "arbitrary", "arbitrary")),
        name="na_attention",
    )(q, k, v, kc, vc, bias)


def _ctx_attn_kernel(q_ref, k_ref, v_ref, o_ref):
    q = q_ref[0]
    k = k_ref[0]
    v = v_ref[0]
    outs = []
    for sel in _head_masks(q.shape):
        qh = jnp.where(sel, q, jnp.zeros_like(q))
        s = _dot_nt(qh, k)
        p = jnp.exp2(s - jnp.max(s, axis=-1, keepdims=True))
        outs.append(_dot(p.astype(BF16), v) / jnp.sum(p, axis=-1, keepdims=True))
    lo_half, _ = _head_masks(outs[0].shape)
    o_ref[0] = jnp.where(lo_half, outs[0], outs[1]).astype(o_ref.dtype)


def _ctx_attention(q, k, v):
    B, n, width = q.shape
    spec = pl.BlockSpec((1, n, LANES), lambda b, hp: (b, 0, hp))
    return pl.pallas_call(
        _ctx_attn_kernel,
        grid=(B, width // LANES),
        in_specs=[spec, spec, spec], out_specs=spec,
        out_shape=jax.ShapeDtypeStruct((B, n, width), BF16),
        compiler_params=_cparams(("arbitrary", "arbitrary")),
        name="ctx_attention",
    )(q, k, v)


def _diff_kernel(*refs, tk, n_main, n_extra, lam_init):
    lq1_ref, lk1_ref, lq2_ref, lk2_ref, sg_ref, q_ref, k_ref, v_ref = refs[:8]
    pos = 8
    if n_extra:
        ke_ref, ve_ref = refs[8:10]
        pos = 10
    o_ref = refs[pos]
    scratch = refs[pos + 1:]
    s_bufs = scratch[:DIFF_BUFS]
    p_bufs = scratch[DIFF_BUFS:2 * DIFF_BUFS]
    acc_ref = scratch[2 * DIFF_BUFS]
    tq = q_ref.shape[1]
    q2 = _split_heads_t(q_ref[0].T)
    n_chunks = n_main + n_extra
    ones = jnp.ones((ONES_ROWS, tk), BF16)

    def keys_values(j):
        if j < n_main:
            return k_ref[0, j * tk:(j + 1) * tk, :], v_ref[0, j * tk:(j + 1) * tk, :]
        j -= n_main
        return ke_ref[0, j * tk:(j + 1) * tk, :], ve_ref[0, j * tk:(j + 1) * tk, :]

    def scores(j):
        s = _dot(keys_values(j)[0], q2)
        s_bufs[j % DIFF_BUFS][...] = s
        return jnp.max(s, axis=0, keepdims=True)

    ahead = DIFF_BUFS - 1
    cmax = [scores(j) for j in range(min(ahead, n_chunks))]
    m = None
    for j in range(n_chunks):
        if j + ahead < n_chunks:
            cmax.append(scores(j + ahead))
        m_new = cmax[j] if m is None else jnp.maximum(m, cmax[j])
        p_buf = p_bufs[j % DIFF_BUFS]
        p_buf[...] = jnp.exp2((s_bufs[j % DIFF_BUFS][...] - m_new).astype(BF16))
        pv = _dot(jnp.concatenate([keys_values(j)[1].T, ones], axis=0), p_buf[...])
        acc_ref[...] = pv if m is None else jnp.exp2(m - m_new) * acc_ref[...] + pv
        m = m_new
    l = acc_ref[LANES:LANES + 1, :]
    acc = acc_ref[:LANES, :]

    lam = (jnp.exp(jnp.sum(lq1_ref[...] * lk1_ref[...], axis=-1, keepdims=True))
           - jnp.exp(jnp.sum(lq2_ref[...] * lk2_ref[...], axis=-1, keepdims=True)) + lam_init)
    o = acc / l
    o = o[:, :tq] - lam * o[:, tq:]
    o = o * lax.rsqrt(jnp.mean(o * o, axis=0, keepdims=True) + EPS) * sg_ref[...]
    o_ref[0] = (o * (1.0 - lam_init)).T.astype(o_ref.dtype)


def _diff_attention(q, k, v, lam_vecs, subln_g, lam_init, k_extra=None, v_extra=None):
    B, sq, width = q.shape
    sk = k.shape[1]
    tq = min(sq, 512)
    tk = min(sk, 256)
    n_extra = 0 if k_extra is None else k_extra.shape[1] // tk
    assert sk % tk == 0 and (k_extra is None or k_extra.shape[1] % tk == 0)
    small = pl.BlockSpec((1, HEAD_DIM), lambda b, h, i: (0, 0))
    in_specs = [small] * 4 + [
        pl.BlockSpec((LANES, 1), lambda b, h, i: (0, 0)),
        pl.BlockSpec((1, tq, LANES), lambda b, h, i: (b, i, h)),
        pl.BlockSpec((1, sk, LANES), lambda b, h, i: (b, 0, h)),
        pl.BlockSpec((1, sk, LANES), lambda b, h, i: (b, 0, h))]
    args = [a.astype(F32)[None] for a in lam_vecs] + [subln_g.astype(F32)[:, None], q, k, v]
    if n_extra:
        in_specs += [pl.BlockSpec((1, k_extra.shape[1], LANES), lambda b, h, i: (b, 0, h))] * 2
        args += [k_extra, v_extra]
    return pl.pallas_call(
        functools.partial(_diff_kernel, tk=tk, n_main=sk // tk, n_extra=n_extra, lam_init=lam_init),
        grid=(B, width // LANES, sq // tq),
        in_specs=in_specs,
        out_specs=pl.BlockSpec((1, tq, LANES), lambda b, h, i: (b, i, h)),
        out_shape=jax.ShapeDtypeStruct((B, sq, width), BF16),
        scratch_shapes=[pltpu.VMEM((tk, 2 * tq), F32)] * DIFF_BUFS + [pltpu.VMEM((tk, 2 * tq), BF16)] * DIFF_BUFS
        + [pltpu.VMEM((LANES + ONES_ROWS, 2 * tq), F32)],
        compiler_params=_cparams(("arbitrary", "arbitrary", "arbitrary")),
        name="diff_attention",
    )(*args)


def _filter_kernel(z_ref, w1_ref, b1_ref, fr_ref, w2_ref, b2_ref, w3_ref, b3_ref, dec_ref, h_ref, ss_ref):
    hdot = functools.partial(jnp.dot, precision=HIGHEST, preferred_element_type=F32)
    fr = fr_ref[...]
    h = jnp.sin(fr * (hdot(z_ref[...], w1_ref[...]) + b1_ref[...]))
    h = jnp.sin(fr * (hdot(h, w2_ref[...]) + b2_ref[...]))
    h = hdot(h, w3_ref[...]) + b3_ref[...]
    dec = dec_ref[...]
    h = h * jnp.concatenate([dec] * (2 * HY_ORDER), axis=1)
    h_ref[...] = h

    @pl.when(pl.program_id(0) == 0)
    def _():
        ss_ref[...] = jnp.zeros(ss_ref.shape, F32)

    ss_ref[...] += jnp.sum(h * h, axis=0, keepdims=True)


def _hyena_filters(L, w1, b1, freq, w2, b2, w3, b3):
    t = jnp.linspace(0.0, 1.0, L, dtype=F32)[:, None]
    w = (2.0 * math.pi / L) * jnp.arange(L, dtype=F32)[:, None]
    bands = (HY_EMB - 1) // 2
    fb = jnp.linspace(1e-4, bands - 1, bands, dtype=F32)[None, :]
    z = jnp.concatenate([t, jnp.cos(fb * w), -jnp.sin(fb * w)], axis=-1)
    emb_pad = HY_HIDDEN - HY_EMB
    z = jnp.pad(z, ((0, 0), (0, emb_pad)))
    w1p = jnp.pad(w1.astype(F32), ((0, emb_pad), (0, 0)))
    min_decay = math.log(HY_TARGET) / HY_SLOW_DECAY
    max_decay = math.log(HY_TARGET) / HY_FAST_DECAY
    deltas = jnp.abs(jnp.linspace(min_decay, max_decay, HY_WIDTH, dtype=F32))
    decay = jnp.exp(-t * deltas[None, :])
    tl = min(L, 512)
    width = HY_ORDER * 2 * HY_WIDTH
    const = lambda i: (0, 0)
    return pl.pallas_call(
        _filter_kernel,
        grid=(L // tl,),
        in_specs=[pl.BlockSpec((tl, HY_HIDDEN), lambda i: (i, 0)),
                  pl.BlockSpec((HY_HIDDEN, HY_HIDDEN), const), pl.BlockSpec((1, HY_HIDDEN), const),
                  pl.BlockSpec((1, HY_HIDDEN), const),
                  pl.BlockSpec((HY_HIDDEN, HY_HIDDEN), const), pl.BlockSpec((1, HY_HIDDEN), const),
                  pl.BlockSpec((HY_HIDDEN, width), const), pl.BlockSpec((1, width), const),
                  pl.BlockSpec((tl, HY_WIDTH), lambda i: (i, 0))],
        out_specs=[pl.BlockSpec((tl, width), lambda i: (i, 0)), pl.BlockSpec((1, width), const)],
        out_shape=[jax.ShapeDtypeStruct((L, width), F32), jax.ShapeDtypeStruct((1, width), F32)],
        compiler_params=_cparams(("arbitrary",)),
        name="hyena_filters",
    )(z, w1p, b1.astype(F32)[None], freq.astype(F32)[None], w2.astype(F32), b2.astype(F32)[None],
      w3.astype(F32), b3.astype(F32)[None], decay)


def _fft_sizes(L):
    n2 = 128 if L >= 1024 else 32
    n1 = 2 * L // n2
    return n1, n2


@functools.lru_cache(maxsize=None)
def _dft_tables(L):
    n1s, n2s = _fft_sizes(L)
    N = 2 * L
    k1 = np.arange(n1s)
    n1 = np.arange(n1s // 2)
    n2 = np.arange(n2s)
    m = (k1[None, :, None] * (n2s * n1[None, None, :] + n2[:, None, None])) % N
    th = 2.0 * np.pi * m / N
    c, s = np.cos(th), np.sin(th)
    l1 = np.concatenate([np.concatenate([c, s], axis=2), np.concatenate([-s, c], axis=2)], axis=1)
    ct, st = np.swapaxes(c, 1, 2), np.swapaxes(s, 1, 2)
    l1i = np.concatenate([np.concatenate([ct, -st], axis=2), np.concatenate([st, ct], axis=2)], axis=1)
    th3 = 2.0 * np.pi * ((n2[:, None] * n2[None, :]) % n2s) / n2s
    c3, s3 = np.cos(th3), np.sin(th3)
    l3 = np.block([[c3, s3], [-s3, c3]])
    l3i = np.block([[c3, -s3], [s3, c3]])
    return tuple(np.asarray(a, np.float32) for a in (l1, l3, l3i, l1i))


def _ld2(buf, rows):
    return jnp.concatenate([buf[0, rows, :], buf[1, rows, :]], axis=1)


def _st2(buf, rows, val):
    buf[0, rows, :] = val[:, :LANES]
    buf[1, rows, :] = val[:, LANES:]


def _fft_stage1(zr, zi, ar, ai, l1_ref, n1s, n2s):
    half = n1s // 2

    def body(n2, carry):
        src = pl.ds(n2, half, stride=n2s)
        d = jnp.concatenate([_ld2(zr, src), _ld2(zi, src)], axis=0)
        out = _dot(l1_ref[n2], d.astype(BF16))
        dst = pl.ds(n2, n1s, stride=n2s)
        _st2(ar, dst, out[:n1s])
        _st2(ai, dst, out[n1s:])
        return carry

    lax.fori_loop(0, n2s, body, 0, unroll=HY_UNROLL)


def _fft_stage3(ar, ai, l3_ref, k1, n2s):
    r = pl.multiple_of(k1 * n2s, n2s)
    d = jnp.concatenate([_ld2(ar, pl.ds(r, n2s)), _ld2(ai, pl.ds(r, n2s))], axis=0)
    return r, _dot(l3_ref[...], d.astype(BF16))


def _spectrum_kernel(hf_ref, hb_ref, ssf_ref, ssb_ref, l1_ref, l3_ref, gr_ref, gi_ref,
                     zr, zi, ar, ai, *, n1s, n2s):
    nrm = lax.rsqrt(ssf_ref[...] + ssb_ref[...] + EPS)
    hf = hf_ref[...] * nrm
    hb = hb_ref[...] * nrm
    zr[0] = hf + hb
    zr[1] = hf - hb
    zi[...] = jnp.zeros(zi.shape, F32)
    _fft_stage1(zr, zi, ar, ai, l1_ref, n1s, n2s)
    inv_n = 1.0 / (n1s * n2s)

    def body(k1, carry):
        r, z = _fft_stage3(ar, ai, l3_ref, k1, n2s)
        gr_ref[0, pl.ds(r, n2s), :] = z[:n2s, :LANES] * inv_n
        gi_ref[0, pl.ds(r, n2s), :] = z[n2s:, LANES:] * inv_n
        return carry

    lax.fori_loop(0, n1s, body, 0, unroll=HY_UNROLL)


def _hyena_spectra(h, ss):
    L = h.shape[0]
    n1s, n2s = _fft_sizes(L)
    N = 2 * L
    l1, l3, _, _ = _dft_tables(L)
    n_ct = HY_WIDTH // LANES
    fwd = lambda o, ct: (0, o * 2 * n_ct + ct)
    bwd = lambda o, ct: (0, o * 2 * n_ct + n_ct + ct)
    out_spec = pl.BlockSpec((1, N, LANES), lambda o, ct: (o, 0, ct))
    return pl.pallas_call(
        functools.partial(_spectrum_kernel, n1s=n1s, n2s=n2s),
        grid=(HY_ORDER, n_ct),
        in_specs=[pl.BlockSpec((L, LANES), fwd), pl.BlockSpec((L, LANES), bwd),
                  pl.BlockSpec((1, LANES), fwd), pl.BlockSpec((1, LANES), bwd),
                  _single(l1.shape, lambda o, ct: (0, 0, 0)), _single(l3.shape, lambda o, ct: (0, 0))],
        out_specs=[out_spec, out_spec],
        out_shape=[jax.ShapeDtypeStruct((HY_ORDER, N, HY_WIDTH), F32)] * 2,
        scratch_shapes=[pltpu.VMEM((2, L, LANES), F32), pltpu.VMEM((2, L, LANES), F32),
                        pltpu.VMEM((2, N, LANES), F32), pltpu.VMEM((2, N, LANES), F32)],
        compiler_params=_cparams(("arbitrary", "arbitrary")),
        name="hyena_spectra",
    )(h, h, ss, ss, jnp.asarray(l1, BF16), jnp.asarray(l3, BF16))


def _short_conv(u, w_ref, b_ref):
    L = u.shape[0]
    row = lax.broadcasted_iota(jnp.int32, u.shape, 0)
    prev = jnp.where(row == 0, 0.0, pltpu.roll(u, 1, axis=0))
    nxt = jnp.where(row == L - 1, 0.0, pltpu.roll(u, L - 1, axis=0))
    return b_ref[...] + prev * w_ref[0:1, :] + u * w_ref[1:2, :] + nxt * w_ref[2:3, :]


def _conv_kernel(a_ref, x_ref, wa_ref, ba_ref, wx_ref, bx_ref, gr_ref, gi_ref, skip_ref,
                 l1_ref, l3_ref, l3i_ref, l1i_ref, o_ref, zr, zi, ar, ai, *, conv_a, n1s, n2s):
    half = n1s // 2
    slots = ((zr, 0), (zr, 1), (zi, 0), (zi, 1))
    for s, (buf, hi) in enumerate(slots):
        u = a_ref[s].astype(F32)
        buf[hi] = _short_conv(u, wa_ref, ba_ref) if conv_a else u
    _fft_stage1(zr, zi, ar, ai, l1_ref, n1s, n2s)

    def mid(k1, carry):
        r, z = _fft_stage3(ar, ai, l3_ref, k1, n2s)
        g_r = gr_ref[0, pl.ds(r, n2s), :]
        g_i = gi_ref[0, pl.ds(r, n2s), :]
        g_r = jnp.concatenate([g_r, g_r], axis=1)
        g_i = jnp.concatenate([g_i, g_i], axis=1)
        z_r, z_i = z[:n2s], z[n2s:]
        p = jnp.concatenate([z_r * g_r - z_i * g_i, z_r * g_i + z_i * g_r], axis=0)
        b = _dot(l3i_ref[...], p.astype(BF16))
        _st2(ar, pl.ds(r, n2s), b[:n2s])
        _st2(ai, pl.ds(r, n2s), b[n2s:])
        return carry

    lax.fori_loop(0, n1s, mid, 0, unroll=HY_UNROLL)
    skip = skip_ref[0]
    skip = jnp.concatenate([skip, skip], axis=1)

    def last(n2, carry):
        src = pl.ds(n2, n1s, stride=n2s)
        d = jnp.concatenate([_ld2(ar, src), _ld2(ai, src)], axis=0)
        y = _dot(l1i_ref[n2], d.astype(BF16))
        rows = pl.ds(n2, half, stride=n2s)
        _st2(zr, rows, y[:half] + skip * _ld2(zr, rows))
        _st2(zi, rows, y[half:] + skip * _ld2(zi, rows))
        return carry

    lax.fori_loop(0, n2s, last, 0, unroll=HY_UNROLL)
    for s, (buf, hi) in enumerate(slots):
        xg = _short_conv(x_ref[s].astype(F32), wx_ref, bx_ref)
        o_ref[s] = (xg * buf[hi]).astype(o_ref.dtype)


def _hyena_conv(a, a_col, x, x_col, conv_w, conv_b, wa_col, wx_col, g_r, g_i, skip, order, *, conv_a, out_dtype):
    B, L, _ = a.shape
    n1s, n2s = _fft_sizes(L)
    N = 2 * L
    n_ct = HY_WIDTH // LANES
    seqs = 4
    assert B % seqs == 0
    tables = _dft_tables(L)
    t_specs = [_single(t.shape, (lambda b, ct: (0, 0, 0)) if t.ndim == 3 else (lambda b, ct: (0, 0))) for t in tables]
    return pl.pallas_call(
        functools.partial(_conv_kernel, conv_a=conv_a, n1s=n1s, n2s=n2s),
        grid=(n_ct, B // seqs),
        in_specs=[_single((seqs, L, LANES), lambda ct, b: (b, 0, a_col + ct)),
                  _single((seqs, L, LANES), lambda ct, b: (b, 0, x_col + ct)),
                  pl.BlockSpec((3, LANES), lambda ct, b: (0, wa_col + ct)),
                  pl.BlockSpec((1, LANES), lambda ct, b: (0, wa_col + ct)),
                  pl.BlockSpec((3, LANES), lambda ct, b: (0, wx_col + ct)),
                  pl.BlockSpec((1, LANES), lambda ct, b: (0, wx_col + ct)),
                  _single((1, N, LANES), lambda ct, b: (order, 0, ct)),
                  _single((1, N, LANES), lambda ct, b: (order, 0, ct)),
                  pl.BlockSpec((1, 1, LANES), lambda ct, b: (order, 0, ct))] + t_specs,
        out_specs=_single((seqs, L, LANES), lambda ct, b: (b, 0, ct)),
        out_shape=jax.ShapeDtypeStruct((B, L, HY_WIDTH), out_dtype),
        scratch_shapes=[pltpu.VMEM((2, L, LANES), F32), pltpu.VMEM((2, L, LANES), F32),
                        pltpu.VMEM((2, N, LANES), F32), pltpu.VMEM((2, N, LANES), F32)],
        compiler_params=_cparams(("arbitrary", "arbitrary")),
        name="hyena_conv",
    )(a, x, conv_w, conv_b[None], conv_w, conv_b[None], g_r, g_i, skip[:, None],
      *[jnp.asarray(t, BF16) for t in tables])


def _hyena(hy, conv_w, conv_b, filt, skip):
    L = hy.shape[1]
    n_ct = HY_WIDTH // LANES
    h, ss = _hyena_filters(L, *filt)
    g_r, g_i = _hyena_spectra(h, ss)
    conv_w = conv_w.astype(F32)
    conv_b = conv_b.astype(F32)
    skip = skip.astype(F32)
    z1 = _hyena_conv(hy, 0, hy, n_ct, conv_w, conv_b, 0, n_ct, g_r, g_i, skip, 0, conv_a=True, out_dtype=F32)
    return _hyena_conv(z1, 0, hy, 2 * n_ct, conv_w, conv_b, 0, 2 * n_ct, g_r, g_i, skip, 1,
                       conv_a=False, out_dtype=BF16)


def kernel(x, c, ctx, c_ctx, norm_g, w_mod, b_mod, w_in, w_out, q_norm_g, k_norm_g, na_rpb, hy_conv_w, hy_conv_b, hy_filt_w1, hy_filt_b1, hy_filt_freq, hy_filt_w2, hy_filt_b2, hy_filt_w3, hy_filt_b3, hy_skip, diff_lam_q1, diff_lam_k1, diff_lam_q2, diff_lam_k2, diff_subln_g):
    B, S, D = x.shape
    mod_rows = -(-(B + 1) // 8) * 8
    vecs = jnp.concatenate([c, c_ctx[None], jnp.zeros((mod_rows - B - 1, D), F32)], axis=0)
    mods = _modulation(vecs, w_mod, b_mod)
    w_in_b = w_in.astype(BF16)
    w_out_b = w_out.astype(BF16)
    rope_tabs = _rope_tables(S)
    xc = ctx
    for l in range(DEPTH):
        ctx_out = l < DEPTH - 1
        shift, scale, gate = (mods[l, :B, j * D:(j + 1) * D] for j in range(3))
        c_shift, c_scale, c_gate = (jnp.broadcast_to(mods[l, B:B + 1, j * D:(j + 1) * D], (B, D)) for j in range(3))
        odd = l % 2 == 1
        lat = _inproj(x, shift, scale, norm_g[l], w_in_b[l], q_norm_g[l], k_norm_g[l], odd=odd,
                      rope_tabs=rope_tabs if odd else None)
        cx = _inproj(xc, c_shift, c_scale, norm_g[l], w_in_b[l], q_norm_g[l], k_norm_g[l], odd=odd)
        if not odd:
            e = l // 2
            q, k, v, hy, g = lat
            qc, kc, vc, hyc, gc = cx
            filt = (hy_filt_w1[e], hy_filt_b1[e], hy_filt_freq[e], hy_filt_w2[e], hy_filt_b2[e],
                    hy_filt_w3[e], hy_filt_b3[e])
            o_na = _na_attention(q, k, v, kc, vc, na_rpb[e])
            o_hy = _hyena(hy, hy_conv_w[e], hy_conv_b[e], filt, hy_skip[e])
            parts = [o_na, o_hy]
            if ctx_out:
                c_parts = [_ctx_attention(qc, kc, vc), _hyena(hyc, hy_conv_w[e], hy_conv_b[e], filt, hy_skip[e])]
        else:
            o_i = l // 2
            lam_init = 0.8 - 0.6 * math.exp(-0.3 * l)
            lam_vecs = (diff_lam_q1[o_i], diff_lam_k1[o_i], diff_lam_q2[o_i], diff_lam_k2[o_i])
            q, k, v, g = lat
            qc, kc, vc, gc = cx
            parts = [_diff_attention(q, k, v, lam_vecs, diff_subln_g[o_i], lam_init, kc, vc)]
            if ctx_out:
                c_parts = [_diff_attention(qc, kc, vc, lam_vecs, diff_subln_g[o_i], lam_init)]
        x = _outproj(x, gate, g, w_out_b[l], parts)
        if ctx_out:
            xc = _outproj(xc, c_gate, gc, w_out_b[l], c_parts)
    return x
```

```python
import functools
import math

import numpy as np
import jax
import jax.numpy as jnp
from jax import lax
from jax.experimental import pallas as pl
from jax.experimental.pallas import tpu as pltpu

F32 = jnp.float32
BF16 = jnp.bfloat16
HIGHEST = lax.Precision.HIGHEST

D_MODEL = 1024
DEPTH = 4
GRID_W = 64
HEAD_DIM = 64
N_HEADS_NA = 8
NA_WIDTH = N_HEADS_NA * HEAD_DIM
HY_WIDTH = D_MODEL - NA_WIDTH
HY_ORDER = 2
HY_EMB = 33
HY_HIDDEN = 64
HY_FAST_DECAY = 0.3
HY_SLOW_DECAY = 1.5
HY_TARGET = 1e-2
WIN_R = 8
WIN_C = 16
N_HEADS_DIFF = D_MODEL // (2 * HEAD_DIM)
DIFF_QK = N_HEADS_DIFF * 2 * HEAD_DIM
IN_WIDTH = 4 * D_MODEL
EPS = 1e-6
ROPE_BASE = 10000.0

LANES = 128
MXU_DIM = 256
VMEM_LIMIT = 56 * 1024 * 1024
NEG_BIG = -1e30
LOG2E = math.log2(math.e)
ONES_ROWS = 16

NA_Q_ROWS = 8
NA_WIN_ROWS = 16
NA_CHUNK_ROWS = 4
HY_LANES = 2 * LANES
DIFF_BUFS = 4
DIFF_TK = 256
DIFF_TQ = 512
HY_UNROLL = 8


def _cparams(sem):
    return pltpu.CompilerParams(dimension_semantics=sem, vmem_limit_bytes=VMEM_LIMIT)


def _single(shape, index_map):
    return pl.BlockSpec(shape, index_map, pipeline_mode=pl.Buffered(1))


def _dot(a, b):
    return jnp.dot(a, b, preferred_element_type=F32)


def _dot_nt(a, b):
    return lax.dot_general(a, b, (((1,), (1,)), ((), ())), preferred_element_type=F32)


def _dot_tn(a, b):
    return lax.dot_general(a, b, (((0,), (0,)), ((), ())), preferred_element_type=F32)


def _mod_kernel(v_ref, w_ref, b_ref, o_ref):
    v = v_ref[...]
    a = v * jax.nn.sigmoid(v)
    o_ref[0] = jnp.dot(a, w_ref[0], precision=HIGHEST, preferred_element_type=F32) + b_ref[0]


def _modulation(vecs, w_mod, b_mod):
    rows = vecs.shape[0]
    tn = 1024
    return pl.pallas_call(
        _mod_kernel,
        grid=(DEPTH, 3 * D_MODEL // tn),
        in_specs=[pl.BlockSpec((rows, D_MODEL), lambda l, j: (0, 0)),
                  pl.BlockSpec((1, D_MODEL, tn), lambda l, j: (l, 0, j)),
                  pl.BlockSpec((1, 1, tn), lambda l, j: (l, 0, j))],
        out_specs=pl.BlockSpec((1, rows, tn), lambda l, j: (l, 0, j)),
        out_shape=jax.ShapeDtypeStruct((DEPTH, rows, 3 * D_MODEL), F32),
        compiler_params=_cparams(("arbitrary", "arbitrary")),
        name="modulation",
    )(vecs, w_mod, b_mod.reshape(DEPTH, 1, 3 * D_MODEL))


_EVEN_SECTIONS = (("q", 0, NA_WIDTH, "qnorm"), ("k", NA_WIDTH, 2 * NA_WIDTH, "knorm"),
                  ("v", 2 * NA_WIDTH, 3 * NA_WIDTH, "copy"),
                  ("hy", 3 * NA_WIDTH, 3 * NA_WIDTH + 3 * HY_WIDTH, "copy"),
                  ("gate", IN_WIDTH - D_MODEL, IN_WIDTH, "silu"))
_ODD_SECTIONS = (("q", 0, DIFF_QK, "qnorm"), ("k", DIFF_QK, 2 * DIFF_QK, "knorm"),
                 ("v", 2 * DIFF_QK, 3 * DIFF_QK, "copy"),
                 ("gate", IN_WIDTH - D_MODEL, IN_WIDTH, "silu"))


def _rope_chunk(a, cos, sin_signed, low_half):
    up = pltpu.roll(a, LANES - HEAD_DIM // 2, axis=1)
    dn = pltpu.roll(a, HEAD_DIM // 2, axis=1)
    return a * cos + jnp.where(low_half, up, dn) * sin_signed


def _inproj_kernel(*refs, sections, rope):
    x_ref, shift_ref, scale_ref, g_ref, w_ref, qg_ref, kg_ref, e_ref = refs[:8]
    pos = 8
    if rope:
        cos_ref, sin_ref = refs[8:10]
        pos = 10
    out_refs = refs[pos:]
    x = x_ref[0]
    ms = jnp.mean(x * x, axis=-1, keepdims=True)
    h = x * lax.rsqrt(ms + EPS) * g_ref[...] * (1.0 + scale_ref[0]) + shift_ref[0]
    hb = h.astype(BF16)
    if rope:
        cos = cos_ref[...]
        sin_signed = sin_ref[...]
        lane = lax.broadcasted_iota(jnp.int32, cos.shape, 1)
        low_half = (lane % HEAD_DIM) < HEAD_DIM // 2
    for o_ref, (_, lo, hi, kind) in zip(out_refs, sections):
        for c0 in range(lo, hi, MXU_DIM):
            acc = _dot(hb, w_ref[:, c0:c0 + MXU_DIM])
            if kind in ("qnorm", "knorm"):
                gain = qg_ref[...] if kind == "qnorm" else kg_ref[...]
                ss = _dot((acc * acc).astype(BF16), e_ref[...])
                acc = acc * lax.rsqrt(ss * (1.0 / HEAD_DIM) + EPS) * gain
                if rope:
                    acc = jnp.concatenate(
                        [_rope_chunk(acc[:, j:j + LANES], cos, sin_signed, low_half)
                         for j in range(0, MXU_DIM, LANES)], axis=1)
            elif kind == "silu":
                acc = acc * jax.nn.sigmoid(acc)
            o_ref[0, :, c0 - lo:c0 - lo + MXU_DIM] = acc.astype(o_ref.dtype)


def _rope_tables(n_tokens):
    t = jnp.arange(n_tokens, dtype=jnp.int32)
    row = (t // GRID_W).astype(F32)
    col = (t % GRID_W).astype(F32)
    n_freq = HEAD_DIM // 4
    inv = ROPE_BASE ** (-jnp.arange(n_freq, dtype=F32) / n_freq)
    ang = jnp.concatenate([row[:, None] * inv, col[:, None] * inv], axis=-1)
    cos, sin = jnp.cos(ang), jnp.sin(ang)
    cos_t = jnp.concatenate([cos, cos, cos, cos], axis=-1)
    sin_t = jnp.concatenate([-sin, sin, -sin, sin], axis=-1)
    return cos_t, sin_t


def _inproj(x, shift, scale, g, w, qg, kg, *, odd, rope_tabs=None):
    B, S, _ = x.shape
    tm = min(S, 512)
    sections = _ODD_SECTIONS if odd else _EVEN_SECTIONS
    rope = rope_tabs is not None
    head = jnp.arange(MXU_DIM) // HEAD_DIM
    e = (head[:, None] == head[None, :]).astype(BF16)
    q_scale = HEAD_DIM ** -0.5 * LOG2E
    qg_t = jnp.tile(qg.astype(F32), MXU_DIM // HEAD_DIM)[None] * q_scale
    kg_t = jnp.tile(kg.astype(F32), MXU_DIM // HEAD_DIM)[None]
    const = lambda b, i: (0, 0)
    in_specs = [pl.BlockSpec((1, tm, D_MODEL), lambda b, i: (b, i, 0)),
                pl.BlockSpec((1, 1, D_MODEL), lambda b, i: (b, 0, 0)),
                pl.BlockSpec((1, 1, D_MODEL), lambda b, i: (b, 0, 0)),
                pl.BlockSpec((1, D_MODEL), const),
                _single((D_MODEL, IN_WIDTH), const),
                pl.BlockSpec((1, MXU_DIM), const),
                pl.BlockSpec((1, MXU_DIM), const),
                pl.BlockSpec((MXU_DIM, MXU_DIM), const)]
    args = [x, shift[:, None], scale[:, None], g[None], w, qg_t, kg_t, e]
    if rope:
        in_specs += [pl.BlockSpec((tm, LANES), lambda b, i: (i, 0))] * 2
        args += list(rope_tabs)
    out_shape = [jax.ShapeDtypeStruct((B, S, hi - lo), BF16) for _, lo, hi, _ in sections]
    out_specs = [pl.BlockSpec((1, tm, hi - lo), lambda b, i: (b, i, 0)) for _, lo, hi, _ in sections]
    return pl.pallas_call(
        functools.partial(_inproj_kernel, sections=sections, rope=rope),
        grid=(B, S // tm),
        in_specs=in_specs, out_specs=out_specs, out_shape=out_shape,
        compiler_params=_cparams(("arbitrary", "arbitrary")),
        name="inproj_odd" if odd else "inproj_even",
    )(*args)


def _outproj_kernel(*refs, n_parts):
    x_ref, gm_ref, gate_ref, w_ref = refs[:4]
    parts = refs[4:4 + n_parts]
    o_ref = refs[4 + n_parts]
    y = jnp.concatenate([p[0].astype(F32) for p in parts], axis=1) * gate_ref[0].astype(F32)
    o_ref[0] = x_ref[0] + gm_ref[0] * _dot(y.astype(BF16), w_ref[...])


def _outproj(x, gate_mod, gate, w, parts):
    B, S, _ = x.shape
    tm = min(S, 512)
    in_specs = [pl.BlockSpec((1, tm, D_MODEL), lambda b, i: (b, i, 0)),
                pl.BlockSpec((1, 1, D_MODEL), lambda b, i: (b, 0, 0)),
                pl.BlockSpec((1, tm, D_MODEL), lambda b, i: (b, i, 0)),
                _single((D_MODEL, D_MODEL), lambda b, i: (0, 0))]
    in_specs += [pl.BlockSpec((1, tm, p.shape[-1]), lambda b, i: (b, i, 0)) for p in parts]
    return pl.pallas_call(
        functools.partial(_outproj_kernel, n_parts=len(parts)),
        grid=(B, S // tm),
        in_specs=in_specs,
        out_specs=pl.BlockSpec((1, tm, D_MODEL), lambda b, i: (b, i, 0)),
        out_shape=jax.ShapeDtypeStruct((B, S, D_MODEL), F32),
        compiler_params=_cparams(("arbitrary", "arbitrary")),
        name="outproj",
    )(x, gate_mod[:, None], gate, w, *parts)


def _head_masks(shape):
    lane = lax.broadcasted_iota(jnp.int32, shape, 1)
    return lane < HEAD_DIM, lane >= HEAD_DIM


def _split_heads_t(qt):
    row = lax.broadcasted_iota(jnp.int32, qt.shape, 0)
    zero = jnp.zeros_like(qt)
    return jnp.concatenate([jnp.where(row < HEAD_DIM, qt, zero), jnp.where(row >= HEAD_DIM, qt, zero)], axis=1)


def _na_kernel(q_ref, k_ref, v_ref, kc_ref, vc_ref, bias_ref, o_ref, *scratch, n_rows):
    s_bufs = scratch[:len(scratch) // 2]
    p_bufs = scratch[len(scratch) // 2:]
    i = pl.program_id(2)
    start = jnp.clip(i * NA_Q_ROWS - WIN_R // 2, 0, n_rows - NA_WIN_ROWS)
    chunk = NA_CHUNK_ROWS * GRID_W
    off = pl.multiple_of(start * GRID_W, chunk)
    tq = q_ref.shape[1]
    q2 = _split_heads_t(q_ref[0].T)
    n_loc = NA_WIN_ROWS // NA_CHUNK_ROWS
    cmax = []
    for t in range(n_loc):
        s = _dot(k_ref[0, pl.ds(off + t * chunk, chunk), :], q2) + bias_ref[0, 0, t * chunk:(t + 1) * chunk, :]
        s_bufs[t][...] = s.astype(BF16)
        cmax.append(jnp.max(s, axis=0, keepdims=True))
    s = _dot(kc_ref[0], q2)
    s_bufs[n_loc][...] = s.astype(BF16)
    cmax.append(jnp.max(s, axis=0, keepdims=True))
    m = acc = None
    for t in range(n_loc + 1):
        m_new = cmax[t] if m is None else jnp.maximum(m, cmax[t])
        p_bufs[t][...] = jnp.exp2(s_bufs[t][...] - m_new.astype(BF16))
        v_t = v_ref[0, pl.ds(off + t * chunk, chunk), :] if t < n_loc else vc_ref[0]
        ones = jnp.ones((ONES_ROWS, v_t.shape[0]), v_t.dtype)
        pv = _dot(jnp.concatenate([v_t.T, ones], axis=0), p_bufs[t][...])
        acc = pv if acc is None else jnp.exp2(m - m_new) * acc + pv
        m = m_new
    o = jnp.concatenate([acc[:HEAD_DIM, :tq] / acc[LANES:LANES + 1, :tq],
                         acc[HEAD_DIM:LANES, tq:] / acc[LANES:LANES + 1, tq:]], axis=0)
    o_ref[0] = o.T.astype(o_ref.dtype)


def _rpb_cols_kernel(r_ref, p_ref, m_ref, o_ref):
    gathered = jnp.dot(r_ref[...], p_ref[...], precision=HIGHEST, preferred_element_type=F32)
    o_ref[...] = gathered * LOG2E + m_ref[...]


def _na_bias_table(rpb, n_rows):
    n_heads, n_drow, n_dcol = rpb.shape
    n_blk = n_rows // NA_Q_ROWS
    kr = min(WIN_R, n_rows)
    c = np.arange(GRID_W)
    cs = np.clip(c - WIN_C // 2, 0, GRID_W - WIN_C)
    col_ok = (c[None, :] >= cs[:, None]) & (c[None, :] < cs[:, None] + WIN_C)
    dcol = np.clip(c[None, :] - c[:, None] + WIN_C - 1, 0, 2 * WIN_C - 2)
    pad = -n_dcol % 8
    onehot = (np.arange(n_dcol + pad)[:, None] == dcol.reshape(1, -1)).astype(np.float32)
    col_mask = np.where(col_ok, 0.0, NEG_BIG).astype(np.float32).reshape(1, -1)
    r2 = jnp.pad(rpb.astype(F32).reshape(n_heads * n_drow, n_dcol), ((0, 0), (0, pad)))
    cols = pl.pallas_call(
        _rpb_cols_kernel,
        out_shape=jax.ShapeDtypeStruct((n_heads * n_drow, GRID_W * GRID_W), F32),
        name="rpb_cols",
    )(r2, jnp.asarray(onehot), jnp.asarray(col_mask))
    cols = cols.reshape(n_heads, n_drow, GRID_W, GRID_W)
    masked = jnp.full((n_heads, GRID_W, GRID_W), NEG_BIG, F32)
    tabs = []
    for i in (0, 1, n_blk - 1):
        r0 = i * NA_Q_ROWS
        start = min(max(r0 - WIN_R // 2, 0), n_rows - NA_WIN_ROWS)
        slabs = []
        for a in range(NA_Q_ROWS):
            r = r0 + a
            rs = min(max(r - kr // 2, 0), n_rows - kr)
            for j in range(NA_WIN_ROWS):
                rr = start + j
                slabs.append(cols[:, rr - r + WIN_R - 1] if rs <= rr < rs + kr else masked)
        b = jnp.stack(slabs, axis=1).reshape(n_heads // 2, 2, NA_Q_ROWS, NA_WIN_ROWS, GRID_W, GRID_W)
        b = b.transpose(0, 3, 5, 1, 2, 4)
        tabs.append(b.reshape(n_heads // 2, NA_WIN_ROWS * GRID_W, 2 * NA_Q_ROWS * GRID_W))
    return jnp.stack(tabs, axis=1)


def _na_attention(q, k, v, kc, vc, rpb):
    B, S, _ = q.shape
    n_ctx = kc.shape[1]
    n_rows = S // GRID_W
    n_blk = n_rows // NA_Q_ROWS
    assert n_rows % NA_Q_ROWS == 0 and n_rows >= NA_WIN_ROWS
    assert all(x % NA_CHUNK_ROWS == 0 for x in (NA_Q_ROWS, NA_WIN_ROWS, WIN_R // 2, n_rows))
    tq = NA_Q_ROWS * GRID_W
    tk = NA_WIN_ROWS * GRID_W
    chunk = NA_CHUNK_ROWS * GRID_W
    bias = _na_bias_table(rpb, n_rows)

    def bias_map(b, hp, i):
        return (hp, jnp.where(i == 0, 0, jnp.where(i == n_blk - 1, 2, 1)), 0, 0)

    return pl.pallas_call(
        functools.partial(_na_kernel, n_rows=n_rows),
        grid=(B, NA_WIDTH // LANES, n_blk),
        in_specs=[pl.BlockSpec((1, tq, LANES), lambda b, hp, i: (b, i, hp)),
                  pl.BlockSpec((1, S, LANES), lambda b, hp, i: (b, 0, hp)),
                  pl.BlockSpec((1, S, LANES), lambda b, hp, i: (b, 0, hp)),
                  pl.BlockSpec((1, n_ctx, LANES), lambda b, hp, i: (b, 0, hp)),
                  pl.BlockSpec((1, n_ctx, LANES), lambda b, hp, i: (b, 0, hp)),
                  pl.BlockSpec((1, 1, tk, 2 * tq), bias_map)],
        out_specs=pl.BlockSpec((1, tq, LANES), lambda b, hp, i: (b, i, hp)),
        out_shape=jax.ShapeDtypeStruct((B, S, NA_WIDTH), BF16),
        scratch_shapes=[pltpu.VMEM((n, 2 * tq), BF16) for _ in range(2)
                        for n in [chunk] * (NA_WIN_ROWS // NA_CHUNK_ROWS) + [n_ctx]],
        compiler_params=_cparams(("arbitrary", "arbitrary", "arbitrary")),
        name="na_attention",
    )(q, k, v, kc, vc, bias)


def _ctx_attn_kernel(q_ref, k_ref, v_ref, o_ref):
    q = q_ref[0]
    k = k_ref[0]
    v = v_ref[0]
    outs = []
    for sel in _head_masks(q.shape):
        qh = jnp.where(sel, q, jnp.zeros_like(q))
        s = _dot_nt(qh, k)
        p = jnp.exp2(s - jnp.max(s, axis=-1, keepdims=True))
        outs.append(_dot(p.astype(BF16), v) / jnp.sum(p, axis=-1, keepdims=True))
    lo_half, _ = _head_masks(outs[0].shape)
    o_ref[0] = jnp.where(lo_half, outs[0], outs[1]).astype(o_ref.dtype)


def _ctx_attention(q, k, v):
    B, n, width = q.shape
    spec = pl.BlockSpec((1, n, LANES), lambda b, hp: (b, 0, hp))
    return pl.pallas_call(
        _ctx_attn_kernel,
        grid=(B, width // LANES),
        in_specs=[spec, spec, spec], out_specs=spec,
        out_shape=jax.ShapeDtypeStruct((B, n, width), BF16),
        compiler_params=_cparams(("arbitrary", "arbitrary")),
        name="ctx_attention",
    )(q, k, v)


def _diff_kernel(*refs, tk, sk, se, lam_init):
    lq1_ref, lk1_ref, lq2_ref, lk2_ref, sg_ref, q_ref, k_ref, v_ref = refs[:8]
    pos = 8
    if se:
        ke_ref, ve_ref = refs[8:10]
        pos = 10
    o_ref = refs[pos]
    scratch = refs[pos + 1:]
    s_bufs = scratch[:DIFF_BUFS]
    p_bufs = scratch[DIFF_BUFS:2 * DIFF_BUFS]
    acc_ref = scratch[2 * DIFF_BUFS]
    tq = q_ref.shape[1]
    q2 = _split_heads_t(q_ref[0].T)
    chunks = [(k_ref, v_ref, r, min(tk, sk - r)) for r in range(0, sk, tk)]
    if se:
        chunks += [(ke_ref, ve_ref, r, min(tk, se - r)) for r in range(0, se, tk)]

    def scores(j):
        kr, _, r, n = chunks[j]
        s = _dot(kr[0, r:r + n, :], q2)
        s_bufs[j % DIFF_BUFS][0:n, :] = s.astype(BF16)
        return jnp.max(s, axis=0, keepdims=True)

    ahead = DIFF_BUFS - 1
    cmax = [scores(j) for j in range(min(ahead, len(chunks)))]
    m = None
    for j, (_, vr, r, n) in enumerate(chunks):
        if j + ahead < len(chunks):
            cmax.append(scores(j + ahead))
        m_new = cmax[j] if m is None else jnp.maximum(m, cmax[j])
        p_buf = p_bufs[j % DIFF_BUFS]
        p_buf[0:n, :] = jnp.exp2(s_bufs[j % DIFF_BUFS][0:n, :] - m_new.astype(BF16))
        vt = jnp.concatenate([vr[0, r:r + n, :].T, jnp.ones((ONES_ROWS, n), BF16)], axis=0)
        pv = _dot(vt, p_buf[0:n, :])
        acc_ref[...] = pv if m is None else jnp.exp2(m - m_new) * acc_ref[...] + pv
        m = m_new
    l = acc_ref[LANES:LANES + 1, :]
    acc = acc_ref[:LANES, :]

    lam = (jnp.exp(jnp.sum(lq1_ref[...] * lk1_ref[...], axis=-1, keepdims=True))
           - jnp.exp(jnp.sum(lq2_ref[...] * lk2_ref[...], axis=-1, keepdims=True)) + lam_init)
    o = acc / l
    o = o[:, :tq] - lam * o[:, tq:]
    o = o * lax.rsqrt(jnp.mean(o * o, axis=0, keepdims=True) + EPS) * sg_ref[...]
    o_ref[0] = (o * (1.0 - lam_init)).T.astype(o_ref.dtype)


def _diff_attention(q, k, v, lam_vecs, subln_g, lam_init, k_extra=None, v_extra=None):
    B, sq, width = q.shape
    sk = k.shape[1]
    tq = min(sq, DIFF_TQ)
    tk = min(sk, DIFF_TK)
    se = 0 if k_extra is None else k_extra.shape[1]
    small = pl.BlockSpec((1, HEAD_DIM), lambda b, h, i: (0, 0))
    in_specs = [small] * 4 + [
        pl.BlockSpec((LANES, 1), lambda b, h, i: (0, 0)),
        pl.BlockSpec((1, tq, LANES), lambda b, h, i: (b, i, h)),
        pl.BlockSpec((1, sk, LANES), lambda b, h, i: (b, 0, h)),
        pl.BlockSpec((1, sk, LANES), lambda b, h, i: (b, 0, h))]
    args = [a.astype(F32)[None] for a in lam_vecs] + [subln_g.astype(F32)[:, None], q, k, v]
    if se:
        in_specs += [pl.BlockSpec((1, se, LANES), lambda b, h, i: (b, 0, h))] * 2
        args += [k_extra, v_extra]
    return pl.pallas_call(
        functools.partial(_diff_kernel, tk=tk, sk=sk, se=se, lam_init=lam_init),
        grid=(B, width // LANES, sq // tq),
        in_specs=in_specs,
        out_specs=pl.BlockSpec((1, tq, LANES), lambda b, h, i: (b, i, h)),
        out_shape=jax.ShapeDtypeStruct((B, sq, width), BF16),
        scratch_shapes=[pltpu.VMEM((tk, 2 * tq), BF16)] * (2 * DIFF_BUFS)
        + [pltpu.VMEM((LANES + ONES_ROWS, 2 * tq), F32)],
        compiler_params=_cparams(("arbitrary", "arbitrary", "arbitrary")),
        name="diff_attention",
    )(*args)


def _filter_kernel(z_ref, w1_ref, b1_ref, fr_ref, w2_ref, b2_ref, w3_ref, b3_ref, dec_ref, h_ref, ss_ref):
    hdot = functools.partial(jnp.dot, precision=HIGHEST, preferred_element_type=F32)
    fr = fr_ref[...]
    h = jnp.sin(fr * (hdot(z_ref[...], w1_ref[...]) + b1_ref[...]))
    h = jnp.sin(fr * (hdot(h, w2_ref[...]) + b2_ref[...]))
    h = hdot(h, w3_ref[...]) + b3_ref[...]
    dec = dec_ref[...]
    h = h * jnp.concatenate([dec] * (2 * HY_ORDER), axis=1)
    h_ref[...] = h

    @pl.when(pl.program_id(0) == 0)
    def _():
        ss_ref[...] = jnp.zeros(ss_ref.shape, F32)

    ss_ref[...] += jnp.sum(h * h, axis=0, keepdims=True)


def _hyena_filters(L, w1, b1, freq, w2, b2, w3, b3):
    t = jnp.linspace(0.0, 1.0, L, dtype=F32)[:, None]
    w = (2.0 * math.pi / L) * jnp.arange(L, dtype=F32)[:, None]
    bands = (HY_EMB - 1) // 2
    fb = jnp.linspace(1e-4, bands - 1, bands, dtype=F32)[None, :]
    z = jnp.concatenate([t, jnp.cos(fb * w), -jnp.sin(fb * w)], axis=-1)
    emb_pad = HY_HIDDEN - HY_EMB
    z = jnp.pad(z, ((0, 0), (0, emb_pad)))
    w1p = jnp.pad(w1.astype(F32), ((0, emb_pad), (0, 0)))
    min_decay = math.log(HY_TARGET) / HY_SLOW_DECAY
    max_decay = math.log(HY_TARGET) / HY_FAST_DECAY
    deltas = jnp.abs(jnp.linspace(min_decay, max_decay, HY_WIDTH, dtype=F32))
    decay = jnp.exp(-t * deltas[None, :])
    tl = min(L, 512)
    width = HY_ORDER * 2 * HY_WIDTH
    const = lambda i: (0, 0)
    return pl.pallas_call(
        _filter_kernel,
        grid=(L // tl,),
        in_specs=[pl.BlockSpec((tl, HY_HIDDEN), lambda i: (i, 0)),
                  pl.BlockSpec((HY_HIDDEN, HY_HIDDEN), const), pl.BlockSpec((1, HY_HIDDEN), const),
                  pl.BlockSpec((1, HY_HIDDEN), const),
                  pl.BlockSpec((HY_HIDDEN, HY_HIDDEN), const), pl.BlockSpec((1, HY_HIDDEN), const),
                  pl.BlockSpec((HY_HIDDEN, width), const), pl.BlockSpec((1, width), const),
                  pl.BlockSpec((tl, HY_WIDTH), lambda i: (i, 0))],
        out_specs=[pl.BlockSpec((tl, width), lambda i: (i, 0)), pl.BlockSpec((1, width), const)],
        out_shape=[jax.ShapeDtypeStruct((L, width), F32), jax.ShapeDtypeStruct((1, width), F32)],
        compiler_params=_cparams(("arbitrary",)),
        name="hyena_filters",
    )(z, w1p, b1.astype(F32)[None], freq.astype(F32)[None], w2.astype(F32), b2.astype(F32)[None],
      w3.astype(F32), b3.astype(F32)[None], decay)


def _fft_sizes(L):
    n2 = 128 if L >= 1024 else 32
    n1 = 2 * L // n2
    return n1, n2


@functools.lru_cache(maxsize=None)
def _dft_tables(L):
    n1s, n2s = _fft_sizes(L)
    N = 2 * L
    k1 = np.arange(n1s)
    n1 = np.arange(n1s // 2)
    n2 = np.arange(n2s)
    m = (k1[None, :, None] * (n2s * n1[None, None, :] + n2[:, None, None])) % N
    th = 2.0 * np.pi * m / N
    c, s = np.cos(th), np.sin(th)
    l1 = np.concatenate([np.concatenate([c, s], axis=2), np.concatenate([-s, c], axis=2)], axis=1)
    ct, st = np.swapaxes(c, 1, 2), np.swapaxes(s, 1, 2)
    l1i = np.concatenate([np.concatenate([ct, -st], axis=2), np.concatenate([st, ct], axis=2)], axis=1)
    th3 = 2.0 * np.pi * ((n2[:, None] * n2[None, :]) % n2s) / n2s
    c3, s3 = np.cos(th3), np.sin(th3)
    l3 = np.block([[c3, s3], [-s3, c3]])
    l3i = np.block([[c3, -s3], [s3, c3]])
    return tuple(np.asarray(a, np.float32) for a in (l1, l3, l3i, l1i))


def _ld2(buf, rows):
    return jnp.concatenate([buf[0, rows, :], buf[1, rows, :]], axis=1)


def _st2(buf, rows, val):
    buf[0, rows, :] = val[:, :LANES]
    buf[1, rows, :] = val[:, LANES:]


def _fft_stage1(zr, zi, ar, ai, l1_ref, n1s, n2s):
    half = n1s // 2

    def body(n2, carry):
        src = pl.ds(n2, half, stride=n2s)
        d = jnp.concatenate([_ld2(zr, src), _ld2(zi, src)], axis=0)
        out = _dot(l1_ref[n2], d.astype(BF16))
        dst = pl.ds(n2, n1s, stride=n2s)
        _st2(ar, dst, out[:n1s])
        _st2(ai, dst, out[n1s:])
        return carry

    lax.fori_loop(0, n2s, body, 0, unroll=HY_UNROLL)


def _fft_stage3(ar, ai, l3_ref, k1, n2s):
    r = pl.multiple_of(k1 * n2s, n2s)
    d = jnp.concatenate([_ld2(ar, pl.ds(r, n2s)), _ld2(ai, pl.ds(r, n2s))], axis=0)
    return r, _dot(l3_ref[...], d.astype(BF16))


def _spectrum_kernel(hf_ref, hb_ref, ssf_ref, ssb_ref, skip_ref, l1_ref, l3_ref, gr_ref, gi_ref,
                     zr, zi, ar, ai, *, n1s, n2s):
    nrm = lax.rsqrt(ssf_ref[...] + ssb_ref[...] + EPS)
    hf = hf_ref[...] * nrm
    hb = hb_ref[...] * nrm
    zr[0] = hf + hb
    zr[1] = hf - hb
    zi[...] = jnp.zeros(zi.shape, F32)
    _fft_stage1(zr, zi, ar, ai, l1_ref, n1s, n2s)
    inv_n = 1.0 / (n1s * n2s)

    def body(k1, carry):
        r, z = _fft_stage3(ar, ai, l3_ref, k1, n2s)
        gr_ref[0, pl.ds(r, n2s), :] = (z[:n2s, :LANES] + skip_ref[0]) * inv_n
        gi_ref[0, pl.ds(r, n2s), :] = z[n2s:, LANES:] * inv_n
        return carry

    lax.fori_loop(0, n1s, body, 0, unroll=HY_UNROLL)


def _hyena_spectra(h, ss, skip):
    L = h.shape[0]
    n1s, n2s = _fft_sizes(L)
    N = 2 * L
    l1, l3, _, _ = _dft_tables(L)
    n_ct = HY_WIDTH // LANES
    fwd = lambda o, ct: (0, o * 2 * n_ct + ct)
    bwd = lambda o, ct: (0, o * 2 * n_ct + n_ct + ct)
    out_spec = pl.BlockSpec((1, N, LANES), lambda o, ct: (o, 0, ct))
    return pl.pallas_call(
        functools.partial(_spectrum_kernel, n1s=n1s, n2s=n2s),
        grid=(HY_ORDER, n_ct),
        in_specs=[pl.BlockSpec((L, LANES), fwd), pl.BlockSpec((L, LANES), bwd),
                  pl.BlockSpec((1, LANES), fwd), pl.BlockSpec((1, LANES), bwd),
                  pl.BlockSpec((1, 1, LANES), lambda o, ct: (o, 0, ct)),
                  _single(l1.shape, lambda o, ct: (0, 0, 0)), _single(l3.shape, lambda o, ct: (0, 0))],
        out_specs=[out_spec, out_spec],
        out_shape=[jax.ShapeDtypeStruct((HY_ORDER, N, HY_WIDTH), F32)] * 2,
        scratch_shapes=[pltpu.VMEM((2, L, LANES), F32), pltpu.VMEM((2, L, LANES), F32),
                        pltpu.VMEM((2, N, LANES), F32), pltpu.VMEM((2, N, LANES), F32)],
        compiler_params=_cparams(("arbitrary", "arbitrary")),
        name="hyena_spectra",
    )(h, h, ss, ss, skip[:, None], jnp.asarray(l1, BF16), jnp.asarray(l3, BF16))


def _short_conv(u, w_ref, b_ref):
    L = u.shape[0]
    row = lax.broadcasted_iota(jnp.int32, u.shape, 0)
    prev = jnp.where(row == 0, 0.0, pltpu.roll(u, 1, axis=0))
    nxt = jnp.where(row == L - 1, 0.0, pltpu.roll(u, L - 1, axis=0))
    return b_ref[...] + prev * w_ref[0:1, :] + u * w_ref[1:2, :] + nxt * w_ref[2:3, :]


def _conv_kernel(a_ref, x_ref, wa_ref, ba_ref, wx_ref, bx_ref, gr_ref, gi_ref,
                 l1_ref, l3_ref, l3i_ref, l1i_ref, o_ref, zr, zi, ar, ai, *, conv_a, n1s, n2s):
    half = n1s // 2
    slots = ((zr, 0), (zr, 1), (zi, 0), (zi, 1))
    for s, (buf, hi) in enumerate(slots):
        u = a_ref[s].astype(F32)
        buf[hi] = _short_conv(u, wa_ref, ba_ref) if conv_a else u
    _fft_stage1(zr, zi, ar, ai, l1_ref, n1s, n2s)

    def mid(k1, carry):
        r, z = _fft_stage3(ar, ai, l3_ref, k1, n2s)
        g_r = gr_ref[0, pl.ds(r, n2s), :]
        g_i = gi_ref[0, pl.ds(r, n2s), :]
        g_r = jnp.concatenate([g_r, g_r], axis=1)
        g_i = jnp.concatenate([g_i, g_i], axis=1)
        z_r, z_i = z[:n2s], z[n2s:]
        p = jnp.concatenate([z_r * g_r - z_i * g_i, z_r * g_i + z_i * g_r], axis=0)
        b = _dot(l3i_ref[...], p.astype(BF16))
        _st2(ar, pl.ds(r, n2s), b[:n2s])
        _st2(ai, pl.ds(r, n2s), b[n2s:])
        return carry

    lax.fori_loop(0, n1s, mid, 0, unroll=HY_UNROLL)
    def last(n2, carry):
        src = pl.ds(n2, n1s, stride=n2s)
        d = jnp.concatenate([_ld2(ar, src), _ld2(ai, src)], axis=0)
        y = _dot(l1i_ref[n2], d.astype(BF16))
        rows = pl.ds(n2, half, stride=n2s)
        _st2(zr, rows, y[:half])
        _st2(zi, rows, y[half:])
        return carry

    lax.fori_loop(0, n2s, last, 0, unroll=HY_UNROLL)
    for s, (buf, hi) in enumerate(slots):
        xg = _short_conv(x_ref[s].astype(F32), wx_ref, bx_ref)
        o_ref[s] = (xg * buf[hi]).astype(o_ref.dtype)


def _hyena_conv(a, a_col, x, x_col, conv_w, conv_b, wa_col, wx_col, g_r, g_i, order, *, conv_a, out_dtype):
    B, L, _ = a.shape
    n1s, n2s = _fft_sizes(L)
    N = 2 * L
    n_ct = HY_WIDTH // LANES
    seqs = 4
    assert B % seqs == 0
    tables = _dft_tables(L)
    t_specs = [_single(t.shape, (lambda b, ct: (0, 0, 0)) if t.ndim == 3 else (lambda b, ct: (0, 0))) for t in tables]
    return pl.pallas_call(
        functools.partial(_conv_kernel, conv_a=conv_a, n1s=n1s, n2s=n2s),
        grid=(n_ct, B // seqs),
        in_specs=[_single((seqs, L, LANES), lambda ct, b: (b, 0, a_col + ct)),
                  _single((seqs, L, LANES), lambda ct, b: (b, 0, x_col + ct)),
                  pl.BlockSpec((3, LANES), lambda ct, b: (0, wa_col + ct)),
                  pl.BlockSpec((1, LANES), lambda ct, b: (0, wa_col + ct)),
                  pl.BlockSpec((3, LANES), lambda ct, b: (0, wx_col + ct)),
                  pl.BlockSpec((1, LANES), lambda ct, b: (0, wx_col + ct)),
                  _single((1, N, LANES), lambda ct, b: (order, 0, ct)),
                  _single((1, N, LANES), lambda ct, b: (order, 0, ct))] + t_specs,
        out_specs=_single((seqs, L, LANES), lambda ct, b: (b, 0, ct)),
        out_shape=jax.ShapeDtypeStruct((B, L, HY_WIDTH), out_dtype),
        scratch_shapes=[pltpu.VMEM((2, L, LANES), F32), pltpu.VMEM((2, L, LANES), F32),
                        pltpu.VMEM((2, N, LANES), F32), pltpu.VMEM((2, N, LANES), F32)],
        compiler_params=_cparams(("arbitrary", "arbitrary")),
        name="hyena_conv",
    )(a, x, conv_w, conv_b[None], conv_w, conv_b[None], g_r, g_i, *[jnp.asarray(t, BF16) for t in tables])


def _hyena(hy, conv_w, conv_b, filt, skip):
    L = hy.shape[1]
    n_ct = HY_WIDTH // LANES
    h, ss = _hyena_filters(L, *filt)
    g_r, g_i = _hyena_spectra(h, ss, skip.astype(F32))
    conv_w = conv_w.astype(F32)
    conv_b = conv_b.astype(F32)
    z1 = _hyena_conv(hy, 0, hy, n_ct, conv_w, conv_b, 0, n_ct, g_r, g_i, 0, conv_a=True, out_dtype=F32)
    return _hyena_conv(z1, 0, hy, 2 * n_ct, conv_w, conv_b, 0, 2 * n_ct, g_r, g_i, 1, conv_a=False, out_dtype=BF16)


def kernel(x, c, ctx, c_ctx, norm_g, w_mod, b_mod, w_in, w_out, q_norm_g, k_norm_g, na_rpb, hy_conv_w, hy_conv_b, hy_filt_w1, hy_filt_b1, hy_filt_freq, hy_filt_w2, hy_filt_b2, hy_filt_w3, hy_filt_b3, hy_skip, diff_lam_q1, diff_lam_k1, diff_lam_q2, diff_lam_k2, diff_subln_g):
    B, S, D = x.shape
    mod_rows = -(-(B + 1) // 8) * 8
    vecs = jnp.concatenate([c, c_ctx[None], jnp.zeros((mod_rows - B - 1, D), F32)], axis=0)
    mods = _modulation(vecs, w_mod, b_mod)
    w_in_b = w_in.astype(BF16)
    w_out_b = w_out.astype(BF16)
    rope_tabs = _rope_tables(S)
    xc = ctx
    for l in range(DEPTH):
        ctx_out = l < DEPTH - 1
        shift, scale, gate = (mods[l, :B, j * D:(j + 1) * D] for j in range(3))
        c_shift, c_scale, c_gate = (jnp.broadcast_to(mods[l, B:B + 1, j * D:(j + 1) * D], (B, D)) for j in range(3))
        odd = l % 2 == 1
        lat = _inproj(x, shift, scale, norm_g[l], w_in_b[l], q_norm_g[l], k_norm_g[l], odd=odd,
                      rope_tabs=rope_tabs if odd else None)
        cx = _inproj(xc, c_shift, c_scale, norm_g[l], w_in_b[l], q_norm_g[l], k_norm_g[l], odd=odd)
        if not odd:
            e = l // 2
            q, k, v, hy, g = lat
            qc, kc, vc, hyc, gc = cx
            filt = (hy_filt_w1[e], hy_filt_b1[e], hy_filt_freq[e], hy_filt_w2[e], hy_filt_b2[e],
                    hy_filt_w3[e], hy_filt_b3[e])
            o_na = _na_attention(q, k, v, kc, vc, na_rpb[e])
            o_hy = _hyena(hy, hy_conv_w[e], hy_conv_b[e], filt, hy_skip[e])
            parts = [o_na, o_hy]
            if ctx_out:
                c_parts = [_ctx_attention(qc, kc, vc), _hyena(hyc, hy_conv_w[e], hy_conv_b[e], filt, hy_skip[e])]
        else:
            o_i = l // 2
            lam_init = 0.8 - 0.6 * math.exp(-0.3 * l)
            lam_vecs = (diff_lam_q1[o_i], diff_lam_k1[o_i], diff_lam_q2[o_i], diff_lam_k2[o_i])
            q, k, v, g = lat
            qc, kc, vc, gc = cx
            parts = [_diff_attention(q, k, v, lam_vecs, diff_subln_g[o_i], lam_init, kc, vc)]
            if ctx_out:
                c_parts = [_diff_attention(qc, kc, vc, lam_vecs, diff_subln_g[o_i], lam_init)]
        x = _outproj(x, gate, g, w_out_b[l], parts)
        if ctx_out:
            xc = _outproj(xc, c_gate, gc, w_out_b[l], c_parts)
    return x
```

```python
import functools
import math

import numpy as np
import jax
import jax.numpy as jnp
from jax import lax
from jax.experimental import pallas as pl
from jax.experimental.pallas import tpu as pltpu

F32 = jnp.float32
BF16 = jnp.bfloat16
HIGHEST = lax.Precision.HIGHEST

D_MODEL = 1024
DEPTH = 4
GRID_W = 64
HEAD_DIM = 64
N_HEADS_NA = 8
NA_WIDTH = N_HEADS_NA * HEAD_DIM
HY_WIDTH = D_MODEL - NA_WIDTH
HY_ORDER = 2
HY_EMB = 33
HY_HIDDEN = 64
HY_FAST_DECAY = 0.3
HY_SLOW_DECAY = 1.5
HY_TARGET = 1e-2
WIN_R = 8
WIN_C = 16
N_HEADS_DIFF = D_MODEL // (2 * HEAD_DIM)
DIFF_QK = N_HEADS_DIFF * 2 * HEAD_DIM
IN_WIDTH = 4 * D_MODEL
EPS = 1e-6
ROPE_BASE = 10000.0

LANES = 128
MXU_DIM = 256
VMEM_LIMIT = 56 * 1024 * 1024
NEG_BIG = -1e30
LOG2E = math.log2(math.e)
ONES_ROWS = 16

NA_Q_ROWS = 8
NA_WIN_ROWS = 16
NA_CHUNK_ROWS = 4
DIFF_BUFS = 4
DIFF_TK = 256
DIFF_TQ = 512
HY_UNROLL = 8
HY_ROWS = 8


def _cparams(sem):
    return pltpu.CompilerParams(dimension_semantics=sem, vmem_limit_bytes=VMEM_LIMIT)


def _single(shape, index_map):
    return pl.BlockSpec(shape, index_map, pipeline_mode=pl.Buffered(1))


def _dot(a, b):
    return jnp.dot(a, b, preferred_element_type=F32)


def _dot_nt(a, b):
    return lax.dot_general(a, b, (((1,), (1,)), ((), ())), preferred_element_type=F32)


def _mod_kernel(v_ref, w_ref, b_ref, o_ref):
    v = v_ref[...]
    a = v * jax.nn.sigmoid(v)
    o_ref[0] = jnp.dot(a, w_ref[0], precision=HIGHEST, preferred_element_type=F32) + b_ref[0]


def _modulation(vecs, w_mod, b_mod):
    rows = vecs.shape[0]
    tn = 1024
    return pl.pallas_call(
        _mod_kernel,
        grid=(DEPTH, 3 * D_MODEL // tn),
        in_specs=[pl.BlockSpec((rows, D_MODEL), lambda l, j: (0, 0)),
                  pl.BlockSpec((1, D_MODEL, tn), lambda l, j: (l, 0, j)),
                  pl.BlockSpec((1, 1, tn), lambda l, j: (l, 0, j))],
        out_specs=pl.BlockSpec((1, rows, tn), lambda l, j: (l, 0, j)),
        out_shape=jax.ShapeDtypeStruct((DEPTH, rows, 3 * D_MODEL), F32),
        compiler_params=_cparams(("arbitrary", "arbitrary")),
        name="modulation",
    )(vecs, w_mod, b_mod.reshape(DEPTH, 1, 3 * D_MODEL))


_EVEN_SECTIONS = (("q", 0, NA_WIDTH, "qnorm"), ("k", NA_WIDTH, 2 * NA_WIDTH, "knorm"),
                  ("v", 2 * NA_WIDTH, 3 * NA_WIDTH, "copy"),
                  ("hy", 3 * NA_WIDTH, 3 * NA_WIDTH + 3 * HY_WIDTH, "copy"),
                  ("gate", IN_WIDTH - D_MODEL, IN_WIDTH, "silu"))
_ODD_SECTIONS = (("q", 0, DIFF_QK, "qnorm"), ("k", DIFF_QK, 2 * DIFF_QK, "knorm"),
                 ("v", 2 * DIFF_QK, 3 * DIFF_QK, "copy"),
                 ("gate", IN_WIDTH - D_MODEL, IN_WIDTH, "silu"))


def _rope_chunk(a, cos, sin_signed, low_half):
    up = pltpu.roll(a, LANES - HEAD_DIM // 2, axis=1)
    dn = pltpu.roll(a, HEAD_DIM // 2, axis=1)
    return a * cos + jnp.where(low_half, up, dn) * sin_signed


def _inproj_kernel(*refs, sections, rope):
    x_ref, shift_ref, scale_ref, g_ref, w_ref, qg_ref, kg_ref, e_ref = refs[:8]
    pos = 8
    if rope:
        cos_ref, sin_ref = refs[8:10]
        pos = 10
    out_refs = refs[pos:]
    x = x_ref[0]
    ms = jnp.mean(x * x, axis=-1, keepdims=True)
    h = x * lax.rsqrt(ms + EPS) * g_ref[...] * (1.0 + scale_ref[0]) + shift_ref[0]
    hb = h.astype(BF16)
    if rope:
        cos = cos_ref[...]
        sin_signed = sin_ref[...]
        lane = lax.broadcasted_iota(jnp.int32, cos.shape, 1)
        low_half = (lane % HEAD_DIM) < HEAD_DIM // 2
    for o_ref, (_, lo, hi, kind) in zip(out_refs, sections):
        for c0 in range(lo, hi, MXU_DIM):
            acc = _dot(hb, w_ref[:, c0:c0 + MXU_DIM])
            if kind in ("qnorm", "knorm"):
                gain = qg_ref[...] if kind == "qnorm" else kg_ref[...]
                ss = _dot((acc * acc).astype(BF16), e_ref[...])
                acc = acc * lax.rsqrt(ss * (1.0 / HEAD_DIM) + EPS) * gain
                if rope:
                    acc = jnp.concatenate(
                        [_rope_chunk(acc[:, j:j + LANES], cos, sin_signed, low_half)
                         for j in range(0, MXU_DIM, LANES)], axis=1)
            elif kind == "silu":
                acc = acc * jax.nn.sigmoid(acc)
            o_ref[0, :, c0 - lo:c0 - lo + MXU_DIM] = acc.astype(o_ref.dtype)


def _rope_tables(n_tokens):
    t = jnp.arange(n_tokens, dtype=jnp.int32)
    row = (t // GRID_W).astype(F32)
    col = (t % GRID_W).astype(F32)
    n_freq = HEAD_DIM // 4
    inv = ROPE_BASE ** (-jnp.arange(n_freq, dtype=F32) / n_freq)
    ang = jnp.concatenate([row[:, None] * inv, col[:, None] * inv], axis=-1)
    cos, sin = jnp.cos(ang), jnp.sin(ang)
    cos_t = jnp.concatenate([cos, cos, cos, cos], axis=-1)
    sin_t = jnp.concatenate([-sin, sin, -sin, sin], axis=-1)
    return cos_t, sin_t


def _inproj(x, shift, scale, g, w, qg, kg, *, odd, rope_tabs=None):
    B, S, _ = x.shape
    tm = min(S, 512)
    sections = _ODD_SECTIONS if odd else _EVEN_SECTIONS
    rope = rope_tabs is not None
    head = jnp.arange(MXU_DIM) // HEAD_DIM
    e = (head[:, None] == head[None, :]).astype(BF16)
    q_scale = HEAD_DIM ** -0.5 * LOG2E
    qg_t = jnp.tile(qg.astype(F32), MXU_DIM // HEAD_DIM)[None] * q_scale
    kg_t = jnp.tile(kg.astype(F32), MXU_DIM // HEAD_DIM)[None]
    const = lambda b, i: (0, 0)
    in_specs = [pl.BlockSpec((1, tm, D_MODEL), lambda b, i: (b, i, 0)),
                pl.BlockSpec((1, 1, D_MODEL), lambda b, i: (b, 0, 0)),
                pl.BlockSpec((1, 1, D_MODEL), lambda b, i: (b, 0, 0)),
                pl.BlockSpec((1, D_MODEL), const),
                _single((D_MODEL, IN_WIDTH), const),
                pl.BlockSpec((1, MXU_DIM), const),
                pl.BlockSpec((1, MXU_DIM), const),
                pl.BlockSpec((MXU_DIM, MXU_DIM), const)]
    args = [x, shift[:, None], scale[:, None], g[None], w, qg_t, kg_t, e]
    if rope:
        in_specs += [pl.BlockSpec((tm, LANES), lambda b, i: (i, 0))] * 2
        args += list(rope_tabs)
    out_shape = [jax.ShapeDtypeStruct((B, S, hi - lo), BF16) for _, lo, hi, _ in sections]
    out_specs = [pl.BlockSpec((1, tm, hi - lo), lambda b, i: (b, i, 0)) for _, lo, hi, _ in sections]
    return pl.pallas_call(
        functools.partial(_inproj_kernel, sections=sections, rope=rope),
        grid=(B, S // tm),
        in_specs=in_specs, out_specs=out_specs, out_shape=out_shape,
        compiler_params=_cparams(("arbitrary", "arbitrary")),
        name="inproj_odd" if odd else "inproj_even",
    )(*args)


def _outproj_kernel(*refs, n_parts):
    x_ref, gm_ref, gate_ref, w_ref = refs[:4]
    parts = refs[4:4 + n_parts]
    o_ref = refs[4 + n_parts]
    y = jnp.concatenate([p[0].astype(F32) for p in parts], axis=1) * gate_ref[0].astype(F32)
    o_ref[0] = x_ref[0] + gm_ref[0] * _dot(y.astype(BF16), w_ref[...])


def _outproj(x, gate_mod, gate, w, parts):
    B, S, _ = x.shape
    tm = min(S, 512)
    in_specs = [pl.BlockSpec((1, tm, D_MODEL), lambda b, i: (b, i, 0)),
                pl.BlockSpec((1, 1, D_MODEL), lambda b, i: (b, 0, 0)),
                pl.BlockSpec((1, tm, D_MODEL), lambda b, i: (b, i, 0)),
                _single((D_MODEL, D_MODEL), lambda b, i: (0, 0))]
    in_specs += [pl.BlockSpec((1, tm, p.shape[-1]), lambda b, i: (b, i, 0)) for p in parts]
    return pl.pallas_call(
        functools.partial(_outproj_kernel, n_parts=len(parts)),
        grid=(B, S // tm),
        in_specs=in_specs,
        out_specs=pl.BlockSpec((1, tm, D_MODEL), lambda b, i: (b, i, 0)),
        out_shape=jax.ShapeDtypeStruct((B, S, D_MODEL), F32),
        compiler_params=_cparams(("arbitrary", "arbitrary")),
        name="outproj",
    )(x, gate_mod[:, None], gate, w, *parts)


def _head_masks(shape):
    lane = lax.broadcasted_iota(jnp.int32, shape, 1)
    return lane < HEAD_DIM, lane >= HEAD_DIM


def _split_heads_t(qt):
    row = lax.broadcasted_iota(jnp.int32, qt.shape, 0)
    zero = jnp.zeros_like(qt)
    return jnp.concatenate([jnp.where(row < HEAD_DIM, qt, zero), jnp.where(row >= HEAD_DIM, qt, zero)], axis=1)


def _na_kernel(q_ref, k_ref, v_ref, kc_ref, vc_ref, bias_ref, o_ref, *scratch, n_rows):
    s_bufs = scratch[:len(scratch) // 2]
    p_bufs = scratch[len(scratch) // 2:]
    i = pl.program_id(2)
    start = jnp.clip(i * NA_Q_ROWS - WIN_R // 2, 0, n_rows - NA_WIN_ROWS)
    chunk = NA_CHUNK_ROWS * GRID_W
    off = pl.multiple_of(start * GRID_W, chunk)
    tq = q_ref.shape[1]
    q2 = _split_heads_t(q_ref[0].T)
    n_loc = NA_WIN_ROWS // NA_CHUNK_ROWS
    cmax = []
    for t in range(n_loc):
        s = _dot(k_ref[0, pl.ds(off + t * chunk, chunk), :], q2) + bias_ref[0, 0, t * chunk:(t + 1) * chunk, :]
        s_bufs[t][...] = s
        cmax.append(jnp.max(s, axis=0, keepdims=True))
    s = _dot(kc_ref[0], q2)
    s_bufs[n_loc][...] = s
    cmax.append(jnp.max(s, axis=0, keepdims=True))
    m = acc = None
    for t in range(n_loc + 1):
        m_new = cmax[t] if m is None else jnp.maximum(m, cmax[t])
        p_bufs[t][...] = jnp.exp2((s_bufs[t][...] - m_new).astype(BF16))
        v_t = v_ref[0, pl.ds(off + t * chunk, chunk), :] if t < n_loc else vc_ref[0]
        ones = jnp.ones((ONES_ROWS, v_t.shape[0]), v_t.dtype)
        pv = _dot(jnp.concatenate([v_t.T, ones], axis=0), p_bufs[t][...])
        acc = pv if acc is None else jnp.exp2(m - m_new) * acc + pv
        m = m_new
    o = jnp.concatenate([acc[:HEAD_DIM, :tq] / acc[LANES:LANES + 1, :tq],
                         acc[HEAD_DIM:LANES, tq:] / acc[LANES:LANES + 1, tq:]], axis=0)
    o_ref[0] = o.T.astype(o_ref.dtype)


def _rpb_cols_kernel(r_ref, p_ref, m_ref, o_ref):
    gathered = jnp.dot(r_ref[...], p_ref[...], precision=HIGHEST, preferred_element_type=F32)
    o_ref[...] = gathered * LOG2E + m_ref[...]


def _na_bias_table(rpb, n_rows):
    n_heads, n_drow, n_dcol = rpb.shape
    n_blk = n_rows // NA_Q_ROWS
    kr = min(WIN_R, n_rows)
    c = np.arange(GRID_W)
    cs = np.clip(c - WIN_C // 2, 0, GRID_W - WIN_C)
    col_ok = (c[None, :] >= cs[:, None]) & (c[None, :] < cs[:, None] + WIN_C)
    dcol = np.clip(c[None, :] - c[:, None] + WIN_C - 1, 0, 2 * WIN_C - 2)
    pad = -n_dcol % 8
    onehot = (np.arange(n_dcol + pad)[:, None] == dcol.reshape(1, -1)).astype(np.float32)
    col_mask = np.where(col_ok, 0.0, NEG_BIG).astype(np.float32).reshape(1, -1)
    r2 = jnp.pad(rpb.astype(F32).reshape(n_heads * n_drow, n_dcol), ((0, 0), (0, pad)))
    cols = pl.pallas_call(
        _rpb_cols_kernel,
        out_shape=jax.ShapeDtypeStruct((n_heads * n_drow, GRID_W * GRID_W), F32),
        name="rpb_cols",
    )(r2, jnp.asarray(onehot), jnp.asarray(col_mask))
    cols = cols.reshape(n_heads, n_drow, GRID_W, GRID_W)
    masked = jnp.full((n_heads, GRID_W, GRID_W), NEG_BIG, F32)
    tabs = []
    for i in (0, 1, n_blk - 1):
        r0 = i * NA_Q_ROWS
        start = min(max(r0 - WIN_R // 2, 0), n_rows - NA_WIN_ROWS)
        slabs = []
        for a in range(NA_Q_ROWS):
            r = r0 + a
            rs = min(max(r - kr // 2, 0), n_rows - kr)
            for j in range(NA_WIN_ROWS):
                rr = start + j
                slabs.append(cols[:, rr - r + WIN_R - 1] if rs <= rr < rs + kr else masked)
        b = jnp.stack(slabs, axis=1).reshape(n_heads // 2, 2, NA_Q_ROWS, NA_WIN_ROWS, GRID_W, GRID_W)
        b = b.transpose(0, 3, 5, 1, 2, 4)
        tabs.append(b.reshape(n_heads // 2, NA_WIN_ROWS * GRID_W, 2 * NA_Q_ROWS * GRID_W))
    return jnp.stack(tabs, axis=1)


def _na_attention(q, k, v, kc, vc, rpb):
    B, S, _ = q.shape
    n_ctx = kc.shape[1]
    n_rows = S // GRID_W
    n_blk = n_rows // NA_Q_ROWS
    assert n_rows % NA_Q_ROWS == 0 and n_rows >= NA_WIN_ROWS
    assert all(x % NA_CHUNK_ROWS == 0 for x in (NA_Q_ROWS, NA_WIN_ROWS, WIN_R // 2, n_rows))
    tq = NA_Q_ROWS * GRID_W
    tk = NA_WIN_ROWS * GRID_W
    chunk = NA_CHUNK_ROWS * GRID_W
    bias = _na_bias_table(rpb, n_rows)

    def bias_map(b, hp, i):
        return (hp, jnp.where(i == 0, 0, jnp.where(i == n_blk - 1, 2, 1)), 0, 0)

    return pl.pallas_call(
        functools.partial(_na_kernel, n_rows=n_rows),
        grid=(B, NA_WIDTH // LANES, n_blk),
        in_specs=[pl.BlockSpec((1, tq, LANES), lambda b, hp, i: (b, i, hp)),
                  pl.BlockSpec((1, S, LANES), lambda b, hp, i: (b, 0, hp)),
                  pl.BlockSpec((1, S, LANES), lambda b, hp, i: (b, 0, hp)),
                  pl.BlockSpec((1, n_ctx, LANES), lambda b, hp, i: (b, 0, hp)),
                  pl.BlockSpec((1, n_ctx, LANES), lambda b, hp, i: (b, 0, hp)),
                  pl.BlockSpec((1, 1, tk, 2 * tq), bias_map)],
        out_specs=pl.BlockSpec((1, tq, LANES), lambda b, hp, i: (b, i, hp)),
        out_shape=jax.ShapeDtypeStruct((B, S, NA_WIDTH), BF16),
        scratch_shapes=[pltpu.VMEM((n, 2 * tq), dt) for dt in (F32, BF16)
                        for n in [chunk] * (NA_WIN_ROWS // NA_CHUNK_ROWS) + [n_ctx]],
        compiler_params=_cparams(("arbitrary", "arbitrary", "arbitrary")),
        name="na_attention",
    )(q, k, v, kc, vc, bias)


def _ctx_attn_kernel(q_ref, k_ref, v_ref, o_ref):
    q = q_ref[0]
    k = k_ref[0]
    v = v_ref[0]
    outs = []
    for sel in _head_masks(q.shape):
        qh = jnp.where(sel, q, jnp.zeros_like(q))
        s = _dot_nt(qh, k)
        p = jnp.exp2(s - jnp.max(s, axis=-1, keepdims=True))
        outs.append(_dot(p.astype(BF16), v) / jnp.sum(p, axis=-1, keepdims=True))
    lo_half, _ = _head_masks(outs[0].shape)
    o_ref[0] = jnp.where(lo_half, outs[0], outs[1]).astype(o_ref.dtype)


def _ctx_attention(q, k, v):
    B, n, width = q.shape
    spec = pl.BlockSpec((1, n, LANES), lambda b, hp: (b, 0, hp))
    return pl.pallas_call(
        _ctx_attn_kernel,
        grid=(B, width // LANES),
        in_specs=[spec, spec, spec], out_specs=spec,
        out_shape=jax.ShapeDtypeStruct((B, n, width), BF16),
        compiler_params=_cparams(("arbitrary", "arbitrary")),
        name="ctx_attention",
    )(q, k, v)


def _diff_kernel(*refs, tk, sk, se, lam_init):
    lq1_ref, lk1_ref, lq2_ref, lk2_ref, sg_ref, q_ref, k_ref, v_ref = refs[:8]
    pos = 8
    if se:
        ke_ref, ve_ref = refs[8:10]
        pos = 10
    o_ref = refs[pos]
    scratch = refs[pos + 1:]
    s_bufs = scratch[:DIFF_BUFS]
    p_bufs = scratch[DIFF_BUFS:2 * DIFF_BUFS]
    acc_ref = scratch[2 * DIFF_BUFS]
    tq = q_ref.shape[1]
    q2 = _split_heads_t(q_ref[0].T)
    chunks = [(k_ref, v_ref, r, min(tk, sk - r)) for r in range(0, sk, tk)]
    if se:
        chunks += [(ke_ref, ve_ref, r, min(tk, se - r)) for r in range(0, se, tk)]

    def scores(j):
        kr, _, r, n = chunks[j]
        s = _dot(kr[0, r:r + n, :], q2)
        s_bufs[j % DIFF_BUFS][0:n, :] = s
        return jnp.max(s, axis=0, keepdims=True)

    ahead = DIFF_BUFS - 1
    cmax = [scores(j) for j in range(min(ahead, len(chunks)))]
    m = None
    for j, (_, vr, r, n) in enumerate(chunks):
        if j + ahead < len(chunks):
            cmax.append(scores(j + ahead))
        m_new = cmax[j] if m is None else jnp.maximum(m, cmax[j])
        p_buf = p_bufs[j % DIFF_BUFS]
        p_buf[0:n, :] = jnp.exp2((s_bufs[j % DIFF_BUFS][0:n, :] - m_new).astype(BF16))
        vt = jnp.concatenate([vr[0, r:r + n, :].T, jnp.ones((ONES_ROWS, n), BF16)], axis=0)
        pv = _dot(vt, p_buf[0:n, :])
        acc_ref[...] = pv if m is None else jnp.exp2(m - m_new) * acc_ref[...] + pv
        m = m_new
    l = acc_ref[LANES:LANES + 1, :]
    acc = acc_ref[:LANES, :]

    lam = (jnp.exp(jnp.sum(lq1_ref[...] * lk1_ref[...], axis=-1, keepdims=True))
           - jnp.exp(jnp.sum(lq2_ref[...] * lk2_ref[...], axis=-1, keepdims=True)) + lam_init)
    o = acc / l
    o = o[:, :tq] - lam * o[:, tq:]
    o = o * lax.rsqrt(jnp.mean(o * o, axis=0, keepdims=True) + EPS) * sg_ref[...]
    o_ref[0] = (o * (1.0 - lam_init)).T.astype(o_ref.dtype)


def _diff_attention(q, k, v, lam_vecs, subln_g, lam_init, k_extra=None, v_extra=None):
    B, sq, width = q.shape
    sk = k.shape[1]
    tq = min(sq, DIFF_TQ)
    tk = min(sk, DIFF_TK)
    se = 0 if k_extra is None else k_extra.shape[1]
    small = pl.BlockSpec((1, HEAD_DIM), lambda b, h, i: (0, 0))
    in_specs = [small] * 4 + [
        pl.BlockSpec((LANES, 1), lambda b, h, i: (0, 0)),
        pl.BlockSpec((1, tq, LANES), lambda b, h, i: (b, i, h)),
        pl.BlockSpec((1, sk, LANES), lambda b, h, i: (b, 0, h)),
        pl.BlockSpec((1, sk, LANES), lambda b, h, i: (b, 0, h))]
    args = [a.astype(F32)[None] for a in lam_vecs] + [subln_g.astype(F32)[:, None], q, k, v]
    if se:
        in_specs += [pl.BlockSpec((1, se, LANES), lambda b, h, i: (b, 0, h))] * 2
        args += [k_extra, v_extra]
    return pl.pallas_call(
        functools.partial(_diff_kernel, tk=tk, sk=sk, se=se, lam_init=lam_init),
        grid=(B, width // LANES, sq // tq),
        in_specs=in_specs,
        out_specs=pl.BlockSpec((1, tq, LANES), lambda b, h, i: (b, i, h)),
        out_shape=jax.ShapeDtypeStruct((B, sq, width), BF16),
        scratch_shapes=[pltpu.VMEM((tk, 2 * tq), F32)] * DIFF_BUFS + [pltpu.VMEM((tk, 2 * tq), BF16)] * DIFF_BUFS
        + [pltpu.VMEM((LANES + ONES_ROWS, 2 * tq), F32)],
        compiler_params=_cparams(("arbitrary", "arbitrary", "arbitrary")),
        name="diff_attention",
    )(*args)


def _filter_kernel(z_ref, w1_ref, b1_ref, fr_ref, w2_ref, b2_ref, w3_ref, b3_ref, dec_ref, h_ref, ss_ref):
    hdot = functools.partial(jnp.dot, precision=HIGHEST, preferred_element_type=F32)
    fr = fr_ref[...]
    h = jnp.sin(fr * (hdot(z_ref[...], w1_ref[...]) + b1_ref[...]))
    h = jnp.sin(fr * (hdot(h, w2_ref[...]) + b2_ref[...]))
    h = hdot(h, w3_ref[...]) + b3_ref[...]
    dec = dec_ref[...]
    h = h * jnp.concatenate([dec] * (2 * HY_ORDER), axis=1)
    h_ref[...] = h

    @pl.when(pl.program_id(0) == 0)
    def _():
        ss_ref[...] = jnp.zeros(ss_ref.shape, F32)

    ss_ref[...] += jnp.sum(h * h, axis=0, keepdims=True)


def _hyena_filters(L, w1, b1, freq, w2, b2, w3, b3):
    t = jnp.linspace(0.0, 1.0, L, dtype=F32)[:, None]
    w = (2.0 * math.pi / L) * jnp.arange(L, dtype=F32)[:, None]
    bands = (HY_EMB - 1) // 2
    fb = jnp.linspace(1e-4, bands - 1, bands, dtype=F32)[None, :]
    z = jnp.concatenate([t, jnp.cos(fb * w), -jnp.sin(fb * w)], axis=-1)
    emb_pad = HY_HIDDEN - HY_EMB
    z = jnp.pad(z, ((0, 0), (0, emb_pad)))
    w1p = jnp.pad(w1.astype(F32), ((0, emb_pad), (0, 0)))
    min_decay = math.log(HY_TARGET) / HY_SLOW_DECAY
    max_decay = math.log(HY_TARGET) / HY_FAST_DECAY
    deltas = jnp.abs(jnp.linspace(min_decay, max_decay, HY_WIDTH, dtype=F32))
    decay = jnp.exp(-t * deltas[None, :])
    tl = min(L, 512)
    width = HY_ORDER * 2 * HY_WIDTH
    const = lambda i: (0, 0)
    return pl.pallas_call(
        _filter_kernel,
        grid=(L // tl,),
        in_specs=[pl.BlockSpec((tl, HY_HIDDEN), lambda i: (i, 0)),
                  pl.BlockSpec((HY_HIDDEN, HY_HIDDEN), const), pl.BlockSpec((1, HY_HIDDEN), const),
                  pl.BlockSpec((1, HY_HIDDEN), const),
                  pl.BlockSpec((HY_HIDDEN, HY_HIDDEN), const), pl.BlockSpec((1, HY_HIDDEN), const),
                  pl.BlockSpec((HY_HIDDEN, width), const), pl.BlockSpec((1, width), const),
                  pl.BlockSpec((tl, HY_WIDTH), lambda i: (i, 0))],
        out_specs=[pl.BlockSpec((tl, width), lambda i: (i, 0)), pl.BlockSpec((1, width), const)],
        out_shape=[jax.ShapeDtypeStruct((L, width), F32), jax.ShapeDtypeStruct((1, width), F32)],
        compiler_params=_cparams(("arbitrary",)),
        name="hyena_filters",
    )(z, w1p, b1.astype(F32)[None], freq.astype(F32)[None], w2.astype(F32), b2.astype(F32)[None],
      w3.astype(F32), b3.astype(F32)[None], decay)


def _fft_sizes(L):
    n2 = 128 if L >= 1024 else 32
    n1 = 2 * L // n2
    return n1, n2


@functools.lru_cache(maxsize=None)
def _dft_tables(L):
    n1s, n2s = _fft_sizes(L)
    N = 2 * L
    k1 = np.arange(n1s)
    n1 = np.arange(n1s // 2)
    n2 = np.arange(n2s)
    th1 = 2.0 * np.pi * ((k1[:, None] * n1[None, :]) % n1s) / n1s
    c, s = np.cos(th1), np.sin(th1)
    eye = np.eye(HY_ROWS)
    lk = np.kron(np.block([[c, s], [-s, c]]), eye)
    lki = np.kron(np.block([[c.T, -s.T], [s.T, c.T]]), eye)
    tht = 2.0 * np.pi * ((k1[:, None] * n2[None, :]) % N) / N
    tw = tht.reshape(n1s, n2s // HY_ROWS, HY_ROWS).transpose(1, 0, 2).reshape(n2s // HY_ROWS, n1s * HY_ROWS, 1)
    twc = np.broadcast_to(np.cos(tw), tw.shape[:2] + (LANES,))
    tws = np.broadcast_to(np.sin(tw), tw.shape[:2] + (LANES,))
    th3 = 2.0 * np.pi * ((n2[:, None] * n2[None, :]) % n2s) / n2s
    c3, s3 = np.cos(th3), np.sin(th3)
    l3 = np.block([[c3, s3], [-s3, c3]])
    l3i = np.block([[c3, -s3], [s3, c3]])
    return tuple(np.asarray(a, np.float32) for a in (lk, twc, tws, l3, l3i, lki))


def _tiles(buf, n, g):
    rows = pl.ds(pl.multiple_of(g * HY_ROWS, HY_ROWS), HY_ROWS)
    return jnp.concatenate(
        [jnp.concatenate([buf[h, i, rows, :] for i in range(n)], axis=0) for h in range(2)], axis=1)


def _put_tiles(buf, n, g, val):
    rows = pl.ds(pl.multiple_of(g * HY_ROWS, HY_ROWS), HY_ROWS)
    for h in range(2):
        for i in range(n):
            buf[h, i, rows, :] = val[i * HY_ROWS:(i + 1) * HY_ROWS, h * LANES:(h + 1) * LANES]


def _twiddle(twc_ref, tws_ref, g):
    c = twc_ref[g].astype(F32)
    s = tws_ref[g].astype(F32)
    return jnp.concatenate([c, c], axis=1), jnp.concatenate([s, s], axis=1)


def _fft_stage1(zr, zi, ar, ai, lk_ref, twc_ref, tws_ref, n1s, n2s):
    half = n1s // 2
    m = n1s * HY_ROWS

    def body(g, carry):
        d = jnp.concatenate([_tiles(zr, half, g), _tiles(zi, half, g)], axis=0)
        out = _dot(lk_ref[...], d.astype(BF16))
        c, s = _twiddle(twc_ref, tws_ref, g)
        o_r, o_i = out[:m], out[m:]
        _put_tiles(ar, n1s, g, o_r * c + o_i * s)
        _put_tiles(ai, n1s, g, o_i * c - o_r * s)
        return carry

    lax.fori_loop(0, n2s // HY_ROWS, body, 0, unroll=2)


def _fft_stage3(ar, ai, l3_ref, k1):
    d = jnp.concatenate([jnp.concatenate([ar[0, k1], ar[1, k1]], axis=1),
                         jnp.concatenate([ai[0, k1], ai[1, k1]], axis=1)], axis=0)
    return _dot(l3_ref[...], d.astype(BF16))


def _put_rows(buf, k1, val):
    buf[0, k1] = val[:, :LANES]
    buf[1, k1] = val[:, LANES:]


def _spectrum_kernel(hf_ref, hb_ref, ssf_ref, ssb_ref, skip_ref, lk_ref, twc_ref, tws_ref, l3_ref, gr_ref, gi_ref,
                     zr, zi, ar, ai, *, n1s, n2s):
    nrm = lax.rsqrt(ssf_ref[...] + ssb_ref[...] + EPS)
    hf = hf_ref[...] * nrm
    hb = hb_ref[...] * nrm
    shape = zr.shape[1:]
    zr[0] = (hf + hb).reshape(shape)
    zr[1] = (hf - hb).reshape(shape)
    zi[...] = jnp.zeros(zi.shape, F32)
    _fft_stage1(zr, zi, ar, ai, lk_ref, twc_ref, tws_ref, n1s, n2s)
    inv_n = 1.0 / (n1s * n2s)

    def body(k1, carry):
        z = _fft_stage3(ar, ai, l3_ref, k1)
        r = pl.multiple_of(k1 * n2s, n2s)
        gr_ref[0, pl.ds(r, n2s), :] = (z[:n2s, :LANES] + skip_ref[0]) * inv_n
        gi_ref[0, pl.ds(r, n2s), :] = z[n2s:, LANES:] * inv_n
        return carry

    lax.fori_loop(0, n1s, body, 0, unroll=HY_UNROLL)


def _hyena_scratch(L):
    n1s, n2s = _fft_sizes(L)
    return [pltpu.VMEM((2, n1s // 2, n2s, LANES), F32), pltpu.VMEM((2, n1s // 2, n2s, LANES), F32),
            pltpu.VMEM((2, n1s, n2s, LANES), F32), pltpu.VMEM((2, n1s, n2s, LANES), F32)]


def _table_specs(tables):
    zeros = {2: (lambda *_: (0, 0)), 3: (lambda *_: (0, 0, 0))}
    return [_single(t.shape, zeros[t.ndim]) for t in tables]


def _hyena_spectra(h, ss, skip):
    L = h.shape[0]
    n1s, n2s = _fft_sizes(L)
    N = 2 * L
    tables = _dft_tables(L)[:4]
    n_ct = HY_WIDTH // LANES
    fwd = lambda o, ct: (0, o * 2 * n_ct + ct)
    bwd = lambda o, ct: (0, o * 2 * n_ct + n_ct + ct)
    out_spec = _single((1, N, LANES), lambda o, ct: (o, 0, ct))
    return pl.pallas_call(
        functools.partial(_spectrum_kernel, n1s=n1s, n2s=n2s),
        grid=(HY_ORDER, n_ct),
        in_specs=[_single((L, LANES), fwd), _single((L, LANES), bwd),
                  pl.BlockSpec((1, LANES), fwd), pl.BlockSpec((1, LANES), bwd),
                  pl.BlockSpec((1, 1, LANES), lambda o, ct: (o, 0, ct))] + _table_specs(tables),
        out_specs=[out_spec, out_spec],
        out_shape=[jax.ShapeDtypeStruct((HY_ORDER, N, HY_WIDTH), F32)] * 2,
        scratch_shapes=_hyena_scratch(L),
        compiler_params=_cparams(("arbitrary", "arbitrary")),
        name="hyena_spectra",
    )(h, h, ss, ss, skip[:, None], *[jnp.asarray(t, BF16) for t in tables])


def _short_conv(u, w_ref, b_ref):
    L = u.shape[0]
    row = lax.broadcasted_iota(jnp.int32, u.shape, 0)
    prev = jnp.where(row == 0, 0.0, pltpu.roll(u, 1, axis=0))
    nxt = jnp.where(row == L - 1, 0.0, pltpu.roll(u, L - 1, axis=0))
    return b_ref[...] + prev * w_ref[0:1, :] + u * w_ref[1:2, :] + nxt * w_ref[2:3, :]


def _conv_kernel(a_ref, x_ref, wa_ref, ba_ref, wx_ref, bx_ref, gr_ref, gi_ref,
                 lk_ref, twc_ref, tws_ref, l3_ref, l3i_ref, lki_ref, o_ref, zr, zi, ar, ai, *, conv_a, n1s, n2s):
    half = n1s // 2
    shape = zr.shape[1:]
    slots = ((zr, 0), (zr, 1), (zi, 0), (zi, 1))
    for s, (buf, hi) in enumerate(slots):
        u = a_ref[s].astype(F32)
        buf[hi] = (_short_conv(u, wa_ref, ba_ref) if conv_a else u).reshape(shape)
    _fft_stage1(zr, zi, ar, ai, lk_ref, twc_ref, tws_ref, n1s, n2s)

    def mid(k1, carry):
        z = _fft_stage3(ar, ai, l3_ref, k1)
        r = pl.multiple_of(k1 * n2s, n2s)
        g_r = gr_ref[0, pl.ds(r, n2s), :]
        g_i = gi_ref[0, pl.ds(r, n2s), :]
        g_r = jnp.concatenate([g_r, g_r], axis=1)
        g_i = jnp.concatenate([g_i, g_i], axis=1)
        z_r, z_i = z[:n2s], z[n2s:]
        p = jnp.concatenate([z_r * g_r - z_i * g_i, z_r * g_i + z_i * g_r], axis=0)
        b = _dot(l3i_ref[...], p.astype(BF16))
        _put_rows(ar, k1, b[:n2s])
        _put_rows(ai, k1, b[n2s:])
        return carry

    lax.fori_loop(0, n1s, mid, 0, unroll=HY_UNROLL)
    m = n1s * HY_ROWS

    def last(g, carry):
        c, s = _twiddle(twc_ref, tws_ref, g)
        b_r, b_i = _tiles(ar, n1s, g), _tiles(ai, n1s, g)
        d = jnp.concatenate([b_r * c - b_i * s, b_i * c + b_r * s], axis=0)
        y = _dot(lki_ref[...], d.astype(BF16))
        _put_tiles(zr, half, g, y[:m // 2])
        _put_tiles(zi, half, g, y[m // 2:])
        return carry

    lax.fori_loop(0, n2s // HY_ROWS, last, 0, unroll=2)
    for s, (buf, hi) in enumerate(slots):
        xg = _short_conv(x_ref[s].astype(F32), wx_ref, bx_ref)
        o_ref[s] = (xg * buf[hi].reshape(xg.shape)).astype(o_ref.dtype)


def _hyena_conv(a, a_col, x, x_col, conv_w, conv_b, wa_col, wx_col, g_r, g_i, order, *, conv_a, out_dtype):
    B, L, _ = a.shape
    n1s, n2s = _fft_sizes(L)
    N = 2 * L
    n_ct = HY_WIDTH // LANES
    seqs = 4
    assert B % seqs == 0
    tables = _dft_tables(L)
    return pl.pallas_call(
        functools.partial(_conv_kernel, conv_a=conv_a, n1s=n1s, n2s=n2s),
        grid=(n_ct, B // seqs),
        in_specs=[_single((seqs, L, LANES), lambda ct, b: (b, 0, a_col + ct)),
                  _single((seqs, L, LANES), lambda ct, b: (b, 0, x_col + ct)),
                  pl.BlockSpec((3, LANES), lambda ct, b: (0, wa_col + ct)),
                  pl.BlockSpec((1, LANES), lambda ct, b: (0, wa_col + ct)),
                  pl.BlockSpec((3, LANES), lambda ct, b: (0, wx_col + ct)),
                  pl.BlockSpec((1, LANES), lambda ct, b: (0, wx_col + ct)),
                  _single((1, N, LANES), lambda ct, b: (order, 0, ct)),
                  _single((1, N, LANES), lambda ct, b: (order, 0, ct))] + _table_specs(tables),
        out_specs=_single((seqs, L, LANES), lambda ct, b: (b, 0, ct)),
        out_shape=jax.ShapeDtypeStruct((B, L, HY_WIDTH), out_dtype),
        scratch_shapes=_hyena_scratch(L),
        compiler_params=_cparams(("arbitrary", "arbitrary")),
        name="hyena_conv",
    )(a, x, conv_w, conv_b[None], conv_w, conv_b[None], g_r, g_i, *[jnp.asarray(t, BF16) for t in tables])


def _hyena(hy, conv_w, conv_b, filt, skip):
    L = hy.shape[1]
    n_ct = HY_WIDTH // LANES
    h, ss = _hyena_filters(L, *filt)
    g_r, g_i = _hyena_spectra(h, ss, skip.astype(F32))
    conv_w = conv_w.astype(F32)
    conv_b = conv_b.astype(F32)
    z1 = _hyena_conv(hy, 0, hy, n_ct, conv_w, conv_b, 0, n_ct, g_r, g_i, 0, conv_a=True, out_dtype=BF16)
    return _hyena_conv(z1, 0, hy, 2 * n_ct, conv_w, conv_b, 0, 2 * n_ct, g_r, g_i, 1, conv_a=False, out_dtype=BF16)


def kernel(x, c, ctx, c_ctx, norm_g, w_mod, b_mod, w_in, w_out, q_norm_g, k_norm_g, na_rpb, hy_conv_w, hy_conv_b, hy_filt_w1, hy_filt_b1, hy_filt_freq, hy_filt_w2, hy_filt_b2, hy_filt_w3, hy_filt_b3, hy_skip, diff_lam_q1, diff_lam_k1, diff_lam_q2, diff_lam_k2, diff_subln_g):
    B, S, D = x.shape
    mod_rows = -(-(B + 1) // 8) * 8
    vecs = jnp.concatenate([c, c_ctx[None], jnp.zeros((mod_rows - B - 1, D), F32)], axis=0)
    mods = _modulation(vecs, w_mod, b_mod)
    w_in_b = w_in.astype(BF16)
    w_out_b = w_out.astype(BF16)
    rope_tabs = _rope_tables(S)
    xc = ctx
    for l in range(DEPTH):
        ctx_out = l < DEPTH - 1
        shift, scale, gate = (mods[l, :B, j * D:(j + 1) * D] for j in range(3))
        c_shift, c_scale, c_gate = (jnp.broadcast_to(mods[l, B:B + 1, j * D:(j + 1) * D], (B, D)) for j in range(3))
        odd = l % 2 == 1
        lat = _inproj(x, shift, scale, norm_g[l], w_in_b[l], q_norm_g[l], k_norm_g[l], odd=odd,
                      rope_tabs=rope_tabs if odd else None)
        cx = _inproj(xc, c_shift, c_scale, norm_g[l], w_in_b[l], q_norm_g[l], k_norm_g[l], odd=odd)
        if not odd:
            e = l // 2
            q, k, v, hy, g = lat
            qc, kc, vc, hyc, gc = cx
            filt = (hy_filt_w1[e], hy_filt_b1[e], hy_filt_freq[e], hy_filt_w2[e], hy_filt_b2[e],
                    hy_filt_w3[e], hy_filt_b3[e])
            o_na = _na_attention(q, k, v, kc, vc, na_rpb[e])
            o_hy = _hyena(hy, hy_conv_w[e], hy_conv_b[e], filt, hy_skip[e])
            parts = [o_na, o_hy]
            if ctx_out:
                c_parts = [_ctx_attention(qc, kc, vc), _hyena(hyc, hy_conv_w[e], hy_conv_b[e], filt, hy_skip[e])]
        else:
            o_i = l // 2
            lam_init = 0.8 - 0.6 * math.exp(-0.3 * l)
            lam_vecs = (diff_lam_q1[o_i], diff_lam_k1[o_i], diff_lam_q2[o_i], diff_lam_k2[o_i])
            q, k, v, g = lat
            qc, kc, vc, gc = cx
            parts = [_diff_attention(q, k, v, lam_vecs, diff_subln_g[o_i], lam_init, kc, vc)]
            if ctx_out:
                c_parts = [_diff_attention(qc, kc, vc, lam_vecs, diff_subln_g[o_i], lam_init)]
        x = _outproj(x, gate, g, w_out_b[l], parts)
        if ctx_out:
            xc = _outproj(xc, c_gate, gc, w_out_b[l], c_parts)
    return x
```

```python
import functools
import math

import numpy as np
import jax
import jax.numpy as jnp
from jax import lax
from jax.experimental import pallas as pl
from jax.experimental.pallas import tpu as pltpu

F32 = jnp.float32
BF16 = jnp.bfloat16
HIGHEST = lax.Precision.HIGHEST

D_MODEL = 1024
DEPTH = 4
GRID_W = 64
HEAD_DIM = 64
N_HEADS_NA = 8
NA_WIDTH = N_HEADS_NA * HEAD_DIM
HY_WIDTH = D_MODEL - NA_WIDTH
HY_ORDER = 2
HY_EMB = 33
HY_HIDDEN = 64
HY_FAST_DECAY = 0.3
HY_SLOW_DECAY = 1.5
HY_TARGET = 1e-2
WIN_R = 8
WIN_C = 16
N_HEADS_DIFF = D_MODEL // (2 * HEAD_DIM)
DIFF_QK = N_HEADS_DIFF * 2 * HEAD_DIM
IN_WIDTH = 4 * D_MODEL
EPS = 1e-6
ROPE_BASE = 10000.0

LANES = 128
MXU_DIM = 256
VMEM_LIMIT = 56 * 1024 * 1024
NEG_BIG = -1e30
LOG2E = math.log2(math.e)
ONES_ROWS = 16

NA_Q_ROWS = 8
NA_WIN_ROWS = 16
NA_CHUNK_ROWS = 4
DIFF_BUFS = 4
DIFF_TK = 256
DIFF_TQ = 512
HY_UNROLL = 8
HY_ROWS = 8


def _cparams(sem):
    return pltpu.CompilerParams(dimension_semantics=sem, vmem_limit_bytes=VMEM_LIMIT)


def _single(shape, index_map):
    return pl.BlockSpec(shape, index_map, pipeline_mode=pl.Buffered(1))


def _dot(a, b):
    return jnp.dot(a, b, preferred_element_type=F32)


def _dot_nt(a, b):
    return lax.dot_general(a, b, (((1,), (1,)), ((), ())), preferred_element_type=F32)


def _mod_kernel(v_ref, w_ref, b_ref, o_ref):
    v = v_ref[...]
    a = v * jax.nn.sigmoid(v)
    o_ref[0] = jnp.dot(a, w_ref[0], precision=HIGHEST, preferred_element_type=F32) + b_ref[0]


def _modulation(vecs, w_mod, b_mod):
    rows = vecs.shape[0]
    tn = 1024
    return pl.pallas_call(
        _mod_kernel,
        grid=(DEPTH, 3 * D_MODEL // tn),
        in_specs=[pl.BlockSpec((rows, D_MODEL), lambda l, j: (0, 0)),
                  pl.BlockSpec((1, D_MODEL, tn), lambda l, j: (l, 0, j)),
                  pl.BlockSpec((1, 1, tn), lambda l, j: (l, 0, j))],
        out_specs=pl.BlockSpec((1, rows, tn), lambda l, j: (l, 0, j)),
        out_shape=jax.ShapeDtypeStruct((DEPTH, rows, 3 * D_MODEL), F32),
        compiler_params=_cparams(("arbitrary", "arbitrary")),
        name="modulation",
    )(vecs, w_mod, b_mod.reshape(DEPTH, 1, 3 * D_MODEL))


_EVEN_SECTIONS = (("q", 0, NA_WIDTH, "qnorm"), ("k", NA_WIDTH, 2 * NA_WIDTH, "knorm"),
                  ("v", 2 * NA_WIDTH, 3 * NA_WIDTH, "copy"),
                  ("hy", 3 * NA_WIDTH, 3 * NA_WIDTH + 3 * HY_WIDTH, "copy"),
                  ("gate", IN_WIDTH - D_MODEL, IN_WIDTH, "silu"))
_ODD_SECTIONS = (("q", 0, DIFF_QK, "qnorm"), ("k", DIFF_QK, 2 * DIFF_QK, "knorm"),
                 ("v", 2 * DIFF_QK, 3 * DIFF_QK, "copy"),
                 ("gate", IN_WIDTH - D_MODEL, IN_WIDTH, "silu"))


def _rope_chunk(a, cos, sin_signed, low_half):
    up = pltpu.roll(a, LANES - HEAD_DIM // 2, axis=1)
    dn = pltpu.roll(a, HEAD_DIM // 2, axis=1)
    return a * cos + jnp.where(low_half, up, dn) * sin_signed


def _inproj_kernel(*refs, sections, rope):
    x_ref, shift_ref, scale_ref, g_ref, w_ref, qg_ref, kg_ref, e_ref = refs[:8]
    pos = 8
    if rope:
        cos_ref, sin_ref = refs[8:10]
        pos = 10
    out_refs = refs[pos:]
    x = x_ref[0]
    ms = jnp.mean(x * x, axis=-1, keepdims=True)
    h = x * lax.rsqrt(ms + EPS) * g_ref[...] * (1.0 + scale_ref[0]) + shift_ref[0]
    hb = h.astype(BF16)
    if rope:
        cos = cos_ref[...]
        sin_signed = sin_ref[...]
        lane = lax.broadcasted_iota(jnp.int32, cos.shape, 1)
        low_half = (lane % HEAD_DIM) < HEAD_DIM // 2
    for o_ref, (_, lo, hi, kind) in zip(out_refs, sections):
        for c0 in range(lo, hi, MXU_DIM):
            acc = _dot(hb, w_ref[:, c0:c0 + MXU_DIM])
            if kind in ("qnorm", "knorm"):
                gain = qg_ref[...] if kind == "qnorm" else kg_ref[...]
                ss = _dot((acc * acc).astype(BF16), e_ref[...])
                acc = acc * lax.rsqrt(ss * (1.0 / HEAD_DIM) + EPS) * gain
                if rope:
                    acc = jnp.concatenate(
                        [_rope_chunk(acc[:, j:j + LANES], cos, sin_signed, low_half)
                         for j in range(0, MXU_DIM, LANES)], axis=1)
            elif kind == "silu":
                acc = acc * jax.nn.sigmoid(acc)
            o_ref[0, :, c0 - lo:c0 - lo + MXU_DIM] = acc.astype(o_ref.dtype)


def _rope_tables(n_tokens):
    t = jnp.arange(n_tokens, dtype=jnp.int32)
    row = (t // GRID_W).astype(F32)
    col = (t % GRID_W).astype(F32)
    n_freq = HEAD_DIM // 4
    inv = ROPE_BASE ** (-jnp.arange(n_freq, dtype=F32) / n_freq)
    ang = jnp.concatenate([row[:, None] * inv, col[:, None] * inv], axis=-1)
    cos, sin = jnp.cos(ang), jnp.sin(ang)
    cos_t = jnp.concatenate([cos, cos, cos, cos], axis=-1)
    sin_t = jnp.concatenate([-sin, sin, -sin, sin], axis=-1)
    return cos_t, sin_t


def _inproj(x, shift, scale, g, w, qg, kg, *, odd, rope_tabs=None):
    B, S, _ = x.shape
    tm = min(S, 512)
    sections = _ODD_SECTIONS if odd else _EVEN_SECTIONS
    rope = rope_tabs is not None
    head = jnp.arange(MXU_DIM) // HEAD_DIM
    e = (head[:, None] == head[None, :]).astype(BF16)
    q_scale = HEAD_DIM ** -0.5 * LOG2E
    qg_t = jnp.tile(qg.astype(F32), MXU_DIM // HEAD_DIM)[None] * q_scale
    kg_t = jnp.tile(kg.astype(F32), MXU_DIM // HEAD_DIM)[None]
    const = lambda b, i: (0, 0)
    in_specs = [pl.BlockSpec((1, tm, D_MODEL), lambda b, i: (b, i, 0)),
                pl.BlockSpec((1, 1, D_MODEL), lambda b, i: (b, 0, 0)),
                pl.BlockSpec((1, 1, D_MODEL), lambda b, i: (b, 0, 0)),
                pl.BlockSpec((1, D_MODEL), const),
                _single((D_MODEL, IN_WIDTH), const),
                pl.BlockSpec((1, MXU_DIM), const),
                pl.BlockSpec((1, MXU_DIM), const),
                pl.BlockSpec((MXU_DIM, MXU_DIM), const)]
    args = [x, shift[:, None], scale[:, None], g[None], w, qg_t, kg_t, e]
    if rope:
        in_specs += [pl.BlockSpec((tm, LANES), lambda b, i: (i, 0))] * 2
        args += list(rope_tabs)
    out_shape = [jax.ShapeDtypeStruct((B, S, hi - lo), BF16) for _, lo, hi, _ in sections]
    out_specs = [pl.BlockSpec((1, tm, hi - lo), lambda b, i: (b, i, 0)) for _, lo, hi, _ in sections]
    return pl.pallas_call(
        functools.partial(_inproj_kernel, sections=sections, rope=rope),
        grid=(B, S // tm),
        in_specs=in_specs, out_specs=out_specs, out_shape=out_shape,
        compiler_params=_cparams(("arbitrary", "arbitrary")),
        name="inproj_odd" if odd else "inproj_even",
    )(*args)


def _outproj_kernel(*refs, n_parts):
    x_ref, gm_ref, gate_ref, w_ref = refs[:4]
    parts = refs[4:4 + n_parts]
    o_ref = refs[4 + n_parts]
    y = jnp.concatenate([p[0].astype(F32) for p in parts], axis=1) * gate_ref[0].astype(F32)
    o_ref[0] = x_ref[0] + gm_ref[0] * _dot(y.astype(BF16), w_ref[...])


def _outproj(x, gate_mod, gate, w, parts):
    B, S, _ = x.shape
    tm = min(S, 512)
    in_specs = [pl.BlockSpec((1, tm, D_MODEL), lambda b, i: (b, i, 0)),
                pl.BlockSpec((1, 1, D_MODEL), lambda b, i: (b, 0, 0)),
                pl.BlockSpec((1, tm, D_MODEL), lambda b, i: (b, i, 0)),
                _single((D_MODEL, D_MODEL), lambda b, i: (0, 0))]
    in_specs += [pl.BlockSpec((1, tm, p.shape[-1]), lambda b, i: (b, i, 0)) for p in parts]
    return pl.pallas_call(
        functools.partial(_outproj_kernel, n_parts=len(parts)),
        grid=(B, S // tm),
        in_specs=in_specs,
        out_specs=pl.BlockSpec((1, tm, D_MODEL), lambda b, i: (b, i, 0)),
        out_shape=jax.ShapeDtypeStruct((B, S, D_MODEL), F32),
        compiler_params=_cparams(("arbitrary", "arbitrary")),
        name="outproj",
    )(x, gate_mod[:, None], gate, w, *parts)


def _head_masks(shape):
    lane = lax.broadcasted_iota(jnp.int32, shape, 1)
    return lane < HEAD_DIM, lane >= HEAD_DIM


def _split_heads_t(qt):
    row = lax.broadcasted_iota(jnp.int32, qt.shape, 0)
    zero = jnp.zeros_like(qt)
    return jnp.concatenate([jnp.where(row < HEAD_DIM, qt, zero), jnp.where(row >= HEAD_DIM, qt, zero)], axis=1)


def _na_kernel(q_ref, k_ref, v_ref, kc_ref, vc_ref, bias_ref, o_ref, *scratch, n_rows):
    s_bufs = scratch[:len(scratch) // 2]
    p_bufs = scratch[len(scratch) // 2:]
    i = pl.program_id(2)
    start = jnp.clip(i * NA_Q_ROWS - WIN_R // 2, 0, n_rows - NA_WIN_ROWS)
    chunk = NA_CHUNK_ROWS * GRID_W
    off = pl.multiple_of(start * GRID_W, chunk)
    tq = q_ref.shape[1]
    q2 = _split_heads_t(q_ref[0].T)
    n_loc = NA_WIN_ROWS // NA_CHUNK_ROWS
    cmax = []
    for t in range(n_loc):
        s = _dot(k_ref[0, pl.ds(off + t * chunk, chunk), :], q2) + bias_ref[0, 0, t * chunk:(t + 1) * chunk, :]
        s_bufs[t][...] = s
        cmax.append(jnp.max(s, axis=0, keepdims=True))
    s = _dot(kc_ref[0], q2)
    s_bufs[n_loc][...] = s
    cmax.append(jnp.max(s, axis=0, keepdims=True))
    m = acc = None
    for t in range(n_loc + 1):
        m_new = cmax[t] if m is None else jnp.maximum(m, cmax[t])
        p_bufs[t][...] = jnp.exp2((s_bufs[t][...] - m_new).astype(BF16))
        v_t = v_ref[0, pl.ds(off + t * chunk, chunk), :] if t < n_loc else vc_ref[0]
        ones = jnp.ones((ONES_ROWS, v_t.shape[0]), v_t.dtype)
        pv = _dot(jnp.concatenate([v_t.T, ones], axis=0), p_bufs[t][...])
        acc = pv if acc is None else jnp.exp2(m - m_new) * acc + pv
        m = m_new
    o = jnp.concatenate([acc[:HEAD_DIM, :tq] / acc[LANES:LANES + 1, :tq],
                         acc[HEAD_DIM:LANES, tq:] / acc[LANES:LANES + 1, tq:]], axis=0)
    o_ref[0] = o.T.astype(o_ref.dtype)


def _rpb_cols_kernel(r_ref, p_ref, m_ref, o_ref):
    gathered = jnp.dot(r_ref[...], p_ref[...], precision=HIGHEST, preferred_element_type=F32)
    o_ref[...] = gathered * LOG2E + m_ref[...]


def _na_bias_kernel(cols_ref, o_ref, *, drow_tables):
    masked = jnp.full((GRID_W, GRID_W), NEG_BIG, F32)
    for t, drow in enumerate(drow_tables):
        @pl.when(pl.program_id(1) == t)
        def _(drow=drow):
            for j in range(NA_WIN_ROWS):
                for h2 in range(2):
                    for a in range(0, NA_Q_ROWS, 2):
                        pair = [masked if drow[b][j] is None else cols_ref[h2, drow[b][j]] for b in (a, a + 1)]
                        lane0 = (h2 * NA_Q_ROWS + a) * GRID_W
                        o_ref[0, 0, j * GRID_W:(j + 1) * GRID_W, lane0:lane0 + 2 * GRID_W] = (
                            jnp.concatenate(pair, axis=1))


def _na_bias_table(rpb, n_rows):
    n_heads, n_drow, n_dcol = rpb.shape
    n_blk = n_rows // NA_Q_ROWS
    kr = min(WIN_R, n_rows)
    c = np.arange(GRID_W)
    cs = np.clip(c - WIN_C // 2, 0, GRID_W - WIN_C)
    col_ok = (c[None, :] >= cs[:, None]) & (c[None, :] < cs[:, None] + WIN_C)
    dcol = np.clip(c[None, :] - c[:, None] + WIN_C - 1, 0, 2 * WIN_C - 2)
    pad = -n_dcol % 8
    onehot = (np.arange(n_dcol + pad)[:, None] == dcol.T.reshape(1, -1)).astype(np.float32)
    col_mask = np.where(col_ok.T, 0.0, NEG_BIG).astype(np.float32).reshape(1, -1)
    r2 = jnp.pad(rpb.astype(F32).reshape(n_heads * n_drow, n_dcol), ((0, 0), (0, pad)))
    cols = pl.pallas_call(
        _rpb_cols_kernel,
        out_shape=jax.ShapeDtypeStruct((n_heads * n_drow, GRID_W * GRID_W), F32),
        name="rpb_cols",
    )(r2, jnp.asarray(onehot), jnp.asarray(col_mask))
    cols = cols.reshape(n_heads, n_drow, GRID_W, GRID_W)
    drow_tables = []
    for i in (0, 1, n_blk - 1):
        r0 = i * NA_Q_ROWS
        start = min(max(r0 - WIN_R // 2, 0), n_rows - NA_WIN_ROWS)
        table = []
        for a in range(NA_Q_ROWS):
            r = r0 + a
            rs = min(max(r - kr // 2, 0), n_rows - kr)
            table.append(tuple(start + j - r + WIN_R - 1 if rs <= start + j < rs + kr else None
                               for j in range(NA_WIN_ROWS)))
        drow_tables.append(tuple(table))
    tq, tk = NA_Q_ROWS * GRID_W, NA_WIN_ROWS * GRID_W
    return pl.pallas_call(
        functools.partial(_na_bias_kernel, drow_tables=tuple(drow_tables)),
        grid=(n_heads // 2, len(drow_tables)),
        in_specs=[pl.BlockSpec((2, n_drow, GRID_W, GRID_W), lambda hp, t: (hp, 0, 0, 0))],
        out_specs=pl.BlockSpec((1, 1, tk, 2 * tq), lambda hp, t: (hp, t, 0, 0)),
        out_shape=jax.ShapeDtypeStruct((n_heads // 2, len(drow_tables), tk, 2 * tq), F32),
        compiler_params=_cparams(("arbitrary", "arbitrary")),
        name="na_bias",
    )(cols)


def _na_attention(q, k, v, kc, vc, rpb):
    B, S, _ = q.shape
    n_ctx = kc.shape[1]
    n_rows = S // GRID_W
    n_blk = n_rows // NA_Q_ROWS
    assert n_rows % NA_Q_ROWS == 0 and n_rows >= NA_WIN_ROWS
    assert all(x % NA_CHUNK_ROWS == 0 for x in (NA_Q_ROWS, NA_WIN_ROWS, WIN_R // 2, n_rows))
    tq = NA_Q_ROWS * GRID_W
    tk = NA_WIN_ROWS * GRID_W
    chunk = NA_CHUNK_ROWS * GRID_W
    bias = _na_bias_table(rpb, n_rows)

    def bias_map(b, hp, i):
        return (hp, jnp.where(i == 0, 0, jnp.where(i == n_blk - 1, 2, 1)), 0, 0)

    return pl.pallas_call(
        functools.partial(_na_kernel, n_rows=n_rows),
        grid=(B, NA_WIDTH // LANES, n_blk),
        in_specs=[pl.BlockSpec((1, tq, LANES), lambda b, hp, i: (b, i, hp)),
                  pl.BlockSpec((1, S, LANES), lambda b, hp, i: (b, 0, hp)),
                  pl.BlockSpec((1, S, LANES), lambda b, hp, i: (b, 0, hp)),
                  pl.BlockSpec((1, n_ctx, LANES), lambda b, hp, i: (b, 0, hp)),
                  pl.BlockSpec((1, n_ctx, LANES), lambda b, hp, i: (b, 0, hp)),
                  pl.BlockSpec((1, 1, tk, 2 * tq), bias_map)],
        out_specs=pl.BlockSpec((1, tq, LANES), lambda b, hp, i: (b, i, hp)),
        out_shape=jax.ShapeDtypeStruct((B, S, NA_WIDTH), BF16),
        scratch_shapes=[pltpu.VMEM((n, 2 * tq), dt) for dt in (F32, BF16)
                        for n in [chunk] * (NA_WIN_ROWS // NA_CHUNK_ROWS) + [n_ctx]],
        compiler_params=_cparams(("arbitrary", "arbitrary", "arbitrary")),
        name="na_attention",
    )(q, k, v, kc, vc, bias)


def _ctx_attn_kernel(q_ref, k_ref, v_ref, o_ref):
    q = q_ref[0]
    k = k_ref[0]
    v = v_ref[0]
    outs = []
    for sel in _head_masks(q.shape):
        qh = jnp.where(sel, q, jnp.zeros_like(q))
        s = _dot_nt(qh, k)
        p = jnp.exp2(s - jnp.max(s, axis=-1, keepdims=True))
        outs.append(_dot(p.astype(BF16), v) / jnp.sum(p, axis=-1, keepdims=True))
    lo_half, _ = _head_masks(outs[0].shape)
    o_ref[0] = jnp.where(lo_half, outs[0], outs[1]).astype(o_ref.dtype)


def _ctx_attention(q, k, v):
    B, n, width = q.shape
    spec = pl.BlockSpec((1, n, LANES), lambda b, hp: (b, 0, hp))
    return pl.pallas_call(
        _ctx_attn_kernel,
        grid=(B, width // LANES),
        in_specs=[spec, spec, spec], out_specs=spec,
        out_shape=jax.ShapeDtypeStruct((B, n, width), BF16),
        compiler_params=_cparams(("arbitrary", "arbitrary")),
        name="ctx_attention",
    )(q, k, v)


def _diff_kernel(*refs, tk, sk, se, lam_init):
    lq1_ref, lk1_ref, lq2_ref, lk2_ref, sg_ref, q_ref, k_ref, v_ref = refs[:8]
    pos = 8
    if se:
        ke_ref, ve_ref = refs[8:10]
        pos = 10
    o_ref = refs[pos]
    scratch = refs[pos + 1:]
    s_bufs = scratch[:DIFF_BUFS]
    p_bufs = scratch[DIFF_BUFS:2 * DIFF_BUFS]
    acc_ref = scratch[2 * DIFF_BUFS]
    tq = q_ref.shape[1]
    q2 = _split_heads_t(q_ref[0].T)
    chunks = [(k_ref, v_ref, r, min(tk, sk - r)) for r in range(0, sk, tk)]
    if se:
        chunks += [(ke_ref, ve_ref, r, min(tk, se - r)) for r in range(0, se, tk)]

    def scores(j):
        kr, _, r, n = chunks[j]
        s = _dot(kr[0, r:r + n, :], q2)
        s_bufs[j % DIFF_BUFS][0:n, :] = s
        return jnp.max(s, axis=0, keepdims=True)

    ahead = DIFF_BUFS - 1
    cmax = [scores(j) for j in range(min(ahead, len(chunks)))]
    m = None
    for j, (_, vr, r, n) in enumerate(chunks):
        if j + ahead < len(chunks):
            cmax.append(scores(j + ahead))
        m_new = cmax[j] if m is None else jnp.maximum(m, cmax[j])
        p_buf = p_bufs[j % DIFF_BUFS]
        p_buf[0:n, :] = jnp.exp2((s_bufs[j % DIFF_BUFS][0:n, :] - m_new).astype(BF16))
        vt = jnp.concatenate([vr[0, r:r + n, :].T, jnp.ones((ONES_ROWS, n), BF16)], axis=0)
        pv = _dot(vt, p_buf[0:n, :])
        acc_ref[...] = pv if m is None else jnp.exp2(m - m_new) * acc_ref[...] + pv
        m = m_new
    l = acc_ref[LANES:LANES + 1, :]
    acc = acc_ref[:LANES, :]

    lam = (jnp.exp(jnp.sum(lq1_ref[...] * lk1_ref[...], axis=-1, keepdims=True))
           - jnp.exp(jnp.sum(lq2_ref[...] * lk2_ref[...], axis=-1, keepdims=True)) + lam_init)
    o = acc / l
    o = o[:, :tq] - lam * o[:, tq:]
    o = o * lax.rsqrt(jnp.mean(o * o, axis=0, keepdims=True) + EPS) * sg_ref[...]
    o_ref[0] = (o * (1.0 - lam_init)).T.astype(o_ref.dtype)


def _diff_attention(q, k, v, lam_vecs, subln_g, lam_init, k_extra=None, v_extra=None):
    B, sq, width = q.shape
    sk = k.shape[1]
    tq = min(sq, DIFF_TQ)
    tk = min(sk, DIFF_TK)
    se = 0 if k_extra is None else k_extra.shape[1]
    small = pl.BlockSpec((1, HEAD_DIM), lambda b, h, i: (0, 0))
    in_specs = [small] * 4 + [
        pl.BlockSpec((LANES, 1), lambda b, h, i: (0, 0)),
        pl.BlockSpec((1, tq, LANES), lambda b, h, i: (b, i, h)),
        pl.BlockSpec((1, sk, LANES), lambda b, h, i: (b, 0, h)),
        pl.BlockSpec((1, sk, LANES), lambda b, h, i: (b, 0, h))]
    args = [a.astype(F32)[None] for a in lam_vecs] + [subln_g.astype(F32)[:, None], q, k, v]
    if se:
        in_specs += [pl.BlockSpec((1, se, LANES), lambda b, h, i: (b, 0, h))] * 2
        args += [k_extra, v_extra]
    return pl.pallas_call(
        functools.partial(_diff_kernel, tk=tk, sk=sk, se=se, lam_init=lam_init),
        grid=(B, width // LANES, sq // tq),
        in_specs=in_specs,
        out_specs=pl.BlockSpec((1, tq, LANES), lambda b, h, i: (b, i, h)),
        out_shape=jax.ShapeDtypeStruct((B, sq, width), BF16),
        scratch_shapes=[pltpu.VMEM((tk, 2 * tq), F32)] * DIFF_BUFS + [pltpu.VMEM((tk, 2 * tq), BF16)] * DIFF_BUFS
        + [pltpu.VMEM((LANES + ONES_ROWS, 2 * tq), F32)],
        compiler_params=_cparams(("arbitrary", "arbitrary", "arbitrary")),
        name="diff_attention",
    )(*args)


def _filter_kernel(z_ref, w1_ref, b1_ref, fr_ref, w2_ref, b2_ref, w3_ref, b3_ref, dec_ref, h_ref, ss_ref):
    hdot = functools.partial(jnp.dot, precision=HIGHEST, preferred_element_type=F32)
    fr = fr_ref[...]
    h = jnp.sin(fr * (hdot(z_ref[...], w1_ref[...]) + b1_ref[...]))
    h = jnp.sin(fr * (hdot(h, w2_ref[...]) + b2_ref[...]))
    h = hdot(h, w3_ref[...]) + b3_ref[...]
    dec = dec_ref[...]
    h = h * jnp.concatenate([dec] * (2 * HY_ORDER), axis=1)
    h_ref[...] = h

    @pl.when(pl.program_id(0) == 0)
    def _():
        ss_ref[...] = jnp.zeros(ss_ref.shape, F32)

    ss_ref[...] += jnp.sum(h * h, axis=0, keepdims=True)


def _hyena_filters(L, w1, b1, freq, w2, b2, w3, b3):
    t = jnp.linspace(0.0, 1.0, L, dtype=F32)[:, None]
    w = (2.0 * math.pi / L) * jnp.arange(L, dtype=F32)[:, None]
    bands = (HY_EMB - 1) // 2
    fb = jnp.linspace(1e-4, bands - 1, bands, dtype=F32)[None, :]
    z = jnp.concatenate([t, jnp.cos(fb * w), -jnp.sin(fb * w)], axis=-1)
    emb_pad = HY_HIDDEN - HY_EMB
    z = jnp.pad(z, ((0, 0), (0, emb_pad)))
    w1p = jnp.pad(w1.astype(F32), ((0, emb_pad), (0, 0)))
    min_decay = math.log(HY_TARGET) / HY_SLOW_DECAY
    max_decay = math.log(HY_TARGET) / HY_FAST_DECAY
    deltas = jnp.abs(jnp.linspace(min_decay, max_decay, HY_WIDTH, dtype=F32))
    decay = jnp.exp(-t * deltas[None, :])
    tl = min(L, 512)
    width = HY_ORDER * 2 * HY_WIDTH
    const = lambda i: (0, 0)
    return pl.pallas_call(
        _filter_kernel,
        grid=(L // tl,),
        in_specs=[pl.BlockSpec((tl, HY_HIDDEN), lambda i: (i, 0)),
                  pl.BlockSpec((HY_HIDDEN, HY_HIDDEN), const), pl.BlockSpec((1, HY_HIDDEN), const),
                  pl.BlockSpec((1, HY_HIDDEN), const),
                  pl.BlockSpec((HY_HIDDEN, HY_HIDDEN), const), pl.BlockSpec((1, HY_HIDDEN), const),
                  pl.BlockSpec((HY_HIDDEN, width), const), pl.BlockSpec((1, width), const),
                  pl.BlockSpec((tl, HY_WIDTH), lambda i: (i, 0))],
        out_specs=[pl.BlockSpec((tl, width), lambda i: (i, 0)), pl.BlockSpec((1, width), const)],
        out_shape=[jax.ShapeDtypeStruct((L, width), F32), jax.ShapeDtypeStruct((1, width), F32)],
        compiler_params=_cparams(("arbitrary",)),
        name="hyena_filters",
    )(z, w1p, b1.astype(F32)[None], freq.astype(F32)[None], w2.astype(F32), b2.astype(F32)[None],
      w3.astype(F32), b3.astype(F32)[None], decay)


def _fft_sizes(L):
    n2 = 128 if L >= 1024 else 32
    n1 = 2 * L // n2
    return n1, n2


@functools.lru_cache(maxsize=None)
def _dft_tables(L):
    n1s, n2s = _fft_sizes(L)
    N = 2 * L
    k1 = np.arange(n1s)
    n1 = np.arange(n1s // 2)
    n2 = np.arange(n2s)
    th1 = 2.0 * np.pi * ((k1[:, None] * n1[None, :]) % n1s) / n1s
    c, s = np.cos(th1), np.sin(th1)
    eye = np.eye(HY_ROWS)
    lk = np.kron(np.block([[c, s], [-s, c]]), eye)
    lki = np.kron(np.block([[c.T, -s.T], [s.T, c.T]]), eye)
    tht = 2.0 * np.pi * ((k1[:, None] * n2[None, :]) % N) / N
    tw = tht.reshape(n1s, n2s // HY_ROWS, HY_ROWS).transpose(1, 0, 2).reshape(n2s // HY_ROWS, n1s * HY_ROWS, 1)
    twc = np.broadcast_to(np.cos(tw), tw.shape[:2] + (LANES,))
    tws = np.broadcast_to(np.sin(tw), tw.shape[:2] + (LANES,))
    th3 = 2.0 * np.pi * ((n2[:, None] * n2[None, :]) % n2s) / n2s
    c3, s3 = np.cos(th3), np.sin(th3)
    l3 = np.block([[c3, s3], [-s3, c3]])
    l3i = np.block([[c3, -s3], [s3, c3]])
    return tuple(np.asarray(a, np.float32) for a in (lk, twc, tws, l3, l3i, lki))


def _tiles(buf, n, g):
    rows = pl.ds(pl.multiple_of(g * HY_ROWS, HY_ROWS), HY_ROWS)
    return jnp.concatenate(
        [jnp.concatenate([buf[h, i, rows, :] for i in range(n)], axis=0) for h in range(2)], axis=1)


def _put_tiles(buf, n, g, val):
    rows = pl.ds(pl.multiple_of(g * HY_ROWS, HY_ROWS), HY_ROWS)
    for h in range(2):
        for i in range(n):
            buf[h, i, rows, :] = val[i * HY_ROWS:(i + 1) * HY_ROWS, h * LANES:(h + 1) * LANES]


def _twiddle(twc_ref, tws_ref, g):
    c = twc_ref[g].astype(F32)
    s = tws_ref[g].astype(F32)
    return jnp.concatenate([c, c], axis=1), jnp.concatenate([s, s], axis=1)


def _fft_stage1(zr, zi, ar, ai, lk_ref, twc_ref, tws_ref, n1s, n2s):
    half = n1s // 2
    m = n1s * HY_ROWS

    def body(g, carry):
        d = jnp.concatenate([_tiles(zr, half, g), _tiles(zi, half, g)], axis=0)
        out = _dot(lk_ref[...], d.astype(BF16))
        c, s = _twiddle(twc_ref, tws_ref, g)
        o_r, o_i = out[:m], out[m:]
        _put_tiles(ar, n1s, g, o_r * c + o_i * s)
        _put_tiles(ai, n1s, g, o_i * c - o_r * s)
        return carry

    lax.fori_loop(0, n2s // HY_ROWS, body, 0, unroll=2)


def _fft_stage3(ar, ai, l3_ref, k1):
    d = jnp.concatenate([jnp.concatenate([ar[0, k1], ar[1, k1]], axis=1),
                         jnp.concatenate([ai[0, k1], ai[1, k1]], axis=1)], axis=0)
    return _dot(l3_ref[...], d.astype(BF16))


def _put_rows(buf, k1, val):
    buf[0, k1] = val[:, :LANES]
    buf[1, k1] = val[:, LANES:]


def _spectrum_kernel(hf_ref, hb_ref, ssf_ref, ssb_ref, skip_ref, lk_ref, twc_ref, tws_ref, l3_ref, gr_ref, gi_ref,
                     zr, zi, ar, ai, *, n1s, n2s):
    nrm = lax.rsqrt(ssf_ref[...] + ssb_ref[...] + EPS)
    hf = hf_ref[...] * nrm
    hb = hb_ref[...] * nrm
    shape = zr.shape[1:]
    zr[0] = (hf + hb).reshape(shape)
    zr[1] = (hf - hb).reshape(shape)
    zi[...] = jnp.zeros(zi.shape, F32)
    _fft_stage1(zr, zi, ar, ai, lk_ref, twc_ref, tws_ref, n1s, n2s)
    inv_n = 1.0 / (n1s * n2s)

    def body(k1, carry):
        z = _fft_stage3(ar, ai, l3_ref, k1)
        r = pl.multiple_of(k1 * n2s, n2s)
        gr_ref[0, pl.ds(r, n2s), :] = (z[:n2s, :LANES] + skip_ref[0]) * inv_n
        gi_ref[0, pl.ds(r, n2s), :] = z[n2s:, LANES:] * inv_n
        return carry

    lax.fori_loop(0, n1s, body, 0, unroll=HY_UNROLL)


def _hyena_scratch(L):
    n1s, n2s = _fft_sizes(L)
    return [pltpu.VMEM((2, n1s // 2, n2s, LANES), F32), pltpu.VMEM((2, n1s // 2, n2s, LANES), F32),
            pltpu.VMEM((2, n1s, n2s, LANES), F32), pltpu.VMEM((2, n1s, n2s, LANES), F32)]


def _table_specs(tables):
    zeros = {2: (lambda *_: (0, 0)), 3: (lambda *_: (0, 0, 0))}
    return [_single(t.shape, zeros[t.ndim]) for t in tables]


def _hyena_spectra(h, ss, skip):
    L = h.shape[0]
    n1s, n2s = _fft_sizes(L)
    N = 2 * L
    tables = _dft_tables(L)[:4]
    n_ct = HY_WIDTH // LANES
    fwd = lambda o, ct: (0, o * 2 * n_ct + ct)
    bwd = lambda o, ct: (0, o * 2 * n_ct + n_ct + ct)
    out_spec = _single((1, N, LANES), lambda o, ct: (o, 0, ct))
    return pl.pallas_call(
        functools.partial(_spectrum_kernel, n1s=n1s, n2s=n2s),
        grid=(HY_ORDER, n_ct),
        in_specs=[_single((L, LANES), fwd), _single((L, LANES), bwd),
                  pl.BlockSpec((1, LANES), fwd), pl.BlockSpec((1, LANES), bwd),
                  pl.BlockSpec((1, 1, LANES), lambda o, ct: (o, 0, ct))] + _table_specs(tables),
        out_specs=[out_spec, out_spec],
        out_shape=[jax.ShapeDtypeStruct((HY_ORDER, N, HY_WIDTH), F32)] * 2,
        scratch_shapes=_hyena_scratch(L),
        compiler_params=_cparams(("arbitrary", "arbitrary")),
        name="hyena_spectra",
    )(h, h, ss, ss, skip[:, None], *[jnp.asarray(t, BF16) for t in tables])


def _short_conv(u, w_ref, b_ref):
    L = u.shape[0]
    row = lax.broadcasted_iota(jnp.int32, (HY_ROWS, u.shape[1]), 0)
    prev = pltpu.roll(u, 1, axis=0)
    prev = jnp.concatenate([jnp.where(row == 0, 0.0, prev[:HY_ROWS]), prev[HY_ROWS:]], axis=0)
    nxt = pltpu.roll(u, L - 1, axis=0)
    nxt = jnp.concatenate([nxt[:L - HY_ROWS], jnp.where(row == HY_ROWS - 1, 0.0, nxt[L - HY_ROWS:])], axis=0)
    return b_ref[...] + prev * w_ref[0:1, :] + u * w_ref[1:2, :] + nxt * w_ref[2:3, :]


def _conv_kernel(a_ref, x_ref, wa_ref, ba_ref, wx_ref, bx_ref, gr_ref, gi_ref,
                 lk_ref, twc_ref, tws_ref, l3_ref, l3i_ref, lki_ref, o_ref, zr, zi, ar, ai, *, conv_a, n1s, n2s):
    half = n1s // 2
    shape = zr.shape[1:]
    slots = ((zr, 0), (zr, 1), (zi, 0), (zi, 1))
    for s, (buf, hi) in enumerate(slots):
        u = a_ref[s].astype(F32)
        buf[hi] = (_short_conv(u, wa_ref, ba_ref) if conv_a else u).reshape(shape)
    _fft_stage1(zr, zi, ar, ai, lk_ref, twc_ref, tws_ref, n1s, n2s)

    def mid(k1, carry):
        z = _fft_stage3(ar, ai, l3_ref, k1)
        r = pl.multiple_of(k1 * n2s, n2s)
        g_r = gr_ref[0, pl.ds(r, n2s), :]
        g_i = gi_ref[0, pl.ds(r, n2s), :]
        g_r = jnp.concatenate([g_r, g_r], axis=1)
        g_i = jnp.concatenate([g_i, g_i], axis=1)
        z_r, z_i = z[:n2s], z[n2s:]
        p = jnp.concatenate([z_r * g_r - z_i * g_i, z_r * g_i + z_i * g_r], axis=0)
        b = _dot(l3i_ref[...], p.astype(BF16))
        _put_rows(ar, k1, b[:n2s])
        _put_rows(ai, k1, b[n2s:])
        return carry

    lax.fori_loop(0, n1s, mid, 0, unroll=HY_UNROLL)
    m = n1s * HY_ROWS

    def last(g, carry):
        c, s = _twiddle(twc_ref, tws_ref, g)
        b_r, b_i = _tiles(ar, n1s, g), _tiles(ai, n1s, g)
        d = jnp.concatenate([b_r * c - b_i * s, b_i * c + b_r * s], axis=0)
        y = _dot(lki_ref[...], d.astype(BF16))
        _put_tiles(zr, half, g, y[:m // 2])
        _put_tiles(zi, half, g, y[m // 2:])
        return carry

    lax.fori_loop(0, n2s // HY_ROWS, last, 0, unroll=2)
    for s, (buf, hi) in enumerate(slots):
        xg = _short_conv(x_ref[s].astype(F32), wx_ref, bx_ref)
        o_ref[s] = (xg * buf[hi].reshape(xg.shape)).astype(o_ref.dtype)


def _hyena_conv(a, a_col, x, x_col, conv_w, conv_b, wa_col, wx_col, g_r, g_i, order, *, conv_a, out_dtype):
    B, L, _ = a.shape
    n1s, n2s = _fft_sizes(L)
    N = 2 * L
    n_ct = HY_WIDTH // LANES
    seqs = 4
    assert B % seqs == 0
    tables = _dft_tables(L)
    return pl.pallas_call(
        functools.partial(_conv_kernel, conv_a=conv_a, n1s=n1s, n2s=n2s),
        grid=(n_ct, B // seqs),
        in_specs=[_single((seqs, L, LANES), lambda ct, b: (b, 0, a_col + ct)),
                  _single((seqs, L, LANES), lambda ct, b: (b, 0, x_col + ct)),
                  pl.BlockSpec((3, LANES), lambda ct, b: (0, wa_col + ct)),
                  pl.BlockSpec((1, LANES), lambda ct, b: (0, wa_col + ct)),
                  pl.BlockSpec((3, LANES), lambda ct, b: (0, wx_col + ct)),
                  pl.BlockSpec((1, LANES), lambda ct, b: (0, wx_col + ct)),
                  _single((1, N, LANES), lambda ct, b: (order, 0, ct)),
                  _single((1, N, LANES), lambda ct, b: (order, 0, ct))] + _table_specs(tables),
        out_specs=_single((seqs, L, LANES), lambda ct, b: (b, 0, ct)),
        out_shape=jax.ShapeDtypeStruct((B, L, HY_WIDTH), out_dtype),
        scratch_shapes=_hyena_scratch(L),
        compiler_params=_cparams(("arbitrary", "arbitrary")),
        name="hyena_conv",
    )(a, x, conv_w, conv_b[None], conv_w, conv_b[None], g_r, g_i, *[jnp.asarray(t, BF16) for t in tables])


def _hyena(hy, conv_w, conv_b, filt, skip):
    L = hy.shape[1]
    n_ct = HY_WIDTH // LANES
    h, ss = _hyena_filters(L, *filt)
    g_r, g_i = _hyena_spectra(h, ss, skip.astype(F32))
    conv_w = conv_w.astype(F32)
    conv_b = conv_b.astype(F32)
    z1 = _hyena_conv(hy, 0, hy, n_ct, conv_w, conv_b, 0, n_ct, g_r, g_i, 0, conv_a=True, out_dtype=BF16)
    return _hyena_conv(z1, 0, hy, 2 * n_ct, conv_w, conv_b, 0, 2 * n_ct, g_r, g_i, 1, conv_a=False, out_dtype=BF16)


def kernel(x, c, ctx, c_ctx, norm_g, w_mod, b_mod, w_in, w_out, q_norm_g, k_norm_g, na_rpb, hy_conv_w, hy_conv_b, hy_filt_w1, hy_filt_b1, hy_filt_freq, hy_filt_w2, hy_filt_b2, hy_filt_w3, hy_filt_b3, hy_skip, diff_lam_q1, diff_lam_k1, diff_lam_q2, diff_lam_k2, diff_subln_g):
    B, S, D = x.shape
    mod_rows = -(-(B + 1) // 8) * 8
    vecs = jnp.concatenate([c, c_ctx[None], jnp.zeros((mod_rows - B - 1, D), F32)], axis=0)
    mods = _modulation(vecs, w_mod, b_mod)
    w_in_b = w_in.astype(BF16)
    w_out_b = w_out.astype(BF16)
    rope_tabs = _rope_tables(S)
    xc = ctx
    for l in range(DEPTH):
        ctx_out = l < DEPTH - 1
        shift, scale, gate = (mods[l, :B, j * D:(j + 1) * D] for j in range(3))
        c_shift, c_scale, c_gate = (jnp.broadcast_to(mods[l, B:B + 1, j * D:(j + 1) * D], (B, D)) for j in range(3))
        odd = l % 2 == 1
        lat = _inproj(x, shift, scale, norm_g[l], w_in_b[l], q_norm_g[l], k_norm_g[l], odd=odd,
                      rope_tabs=rope_tabs if odd else None)
        cx = _inproj(xc, c_shift, c_scale, norm_g[l], w_in_b[l], q_norm_g[l], k_norm_g[l], odd=odd)
        if not odd:
            e = l // 2
            q, k, v, hy, g = lat
            qc, kc, vc, hyc, gc = cx
            filt = (hy_filt_w1[e], hy_filt_b1[e], hy_filt_freq[e], hy_filt_w2[e], hy_filt_b2[e],
                    hy_filt_w3[e], hy_filt_b3[e])
            o_na = _na_attention(q, k, v, kc, vc, na_rpb[e])
            o_hy = _hyena(hy, hy_conv_w[e], hy_conv_b[e], filt, hy_skip[e])
            parts = [o_na, o_hy]
            if ctx_out:
                c_parts = [_ctx_attention(qc, kc, vc), _hyena(hyc, hy_conv_w[e], hy_conv_b[e], filt, hy_skip[e])]
        else:
            o_i = l // 2
            lam_init = 0.8 - 0.6 * math.exp(-0.3 * l)
            lam_vecs = (diff_lam_q1[o_i], diff_lam_k1[o_i], diff_lam_q2[o_i], diff_lam_k2[o_i])
            q, k, v, g = lat
            qc, kc, vc, gc = cx
            parts = [_diff_attention(q, k, v, lam_vecs, diff_subln_g[o_i], lam_init, kc, vc)]
            if ctx_out:
                c_parts = [_diff_attention(qc, kc, vc, lam_vecs, diff_subln_g[o_i], lam_init)]
        x = _outproj(x, gate, g, w_out_b[l], parts)
        if ctx_out:
            xc = _outproj(xc, c_gate, gc, w_out_b[l], c_parts)
    return x
```

```python
import functools
import math

import numpy as np
import jax
import jax.numpy as jnp
from jax import lax
from jax.experimental import pallas as pl
from jax.experimental.pallas import tpu as pltpu

F32 = jnp.float32
BF16 = jnp.bfloat16
HIGHEST = lax.Precision.HIGHEST

D_MODEL = 1024
DEPTH = 4
GRID_W = 64
HEAD_DIM = 64
N_HEADS_NA = 8
NA_WIDTH = N_HEADS_NA * HEAD_DIM
HY_WIDTH = D_MODEL - NA_WIDTH
HY_ORDER = 2
HY_EMB = 33
HY_HIDDEN = 64
HY_FAST_DECAY = 0.3
HY_SLOW_DECAY = 1.5
HY_TARGET = 1e-2
WIN_R = 8
WIN_C = 16
N_HEADS_DIFF = D_MODEL // (2 * HEAD_DIM)
DIFF_QK = N_HEADS_DIFF * 2 * HEAD_DIM
IN_WIDTH = 4 * D_MODEL
EPS = 1e-6
ROPE_BASE = 10000.0

LANES = 128
MXU_DIM = 256
VMEM_LIMIT = 56 * 1024 * 1024
NEG_BIG = -1e30
LOG2E = math.log2(math.e)
ONES_ROWS = 16

NA_Q_ROWS = 8
NA_WIN_ROWS = 16
NA_CHUNK_ROWS = 4
DIFF_BUFS = 4
DIFF_TK = 256
DIFF_TQ = 512
HY_UNROLL = 16
HY_ROWS = 8


def _cparams(sem):
    return pltpu.CompilerParams(dimension_semantics=sem, vmem_limit_bytes=VMEM_LIMIT)


def _single(shape, index_map):
    return pl.BlockSpec(shape, index_map, pipeline_mode=pl.Buffered(1))


def _dot(a, b):
    return jnp.dot(a, b, preferred_element_type=F32)


def _dot_nt(a, b):
    return lax.dot_general(a, b, (((1,), (1,)), ((), ())), preferred_element_type=F32)


def _mod_kernel(v_ref, w_ref, b_ref, o_ref):
    v = v_ref[...]
    a = v * jax.nn.sigmoid(v)
    o_ref[0] = jnp.dot(a, w_ref[0], precision=HIGHEST, preferred_element_type=F32) + b_ref[0]


def _modulation(vecs, w_mod, b_mod):
    rows = vecs.shape[0]
    tn = 1024
    return pl.pallas_call(
        _mod_kernel,
        grid=(DEPTH, 3 * D_MODEL // tn),
        in_specs=[pl.BlockSpec((rows, D_MODEL), lambda l, j: (0, 0)),
                  pl.BlockSpec((1, D_MODEL, tn), lambda l, j: (l, 0, j)),
                  pl.BlockSpec((1, 1, tn), lambda l, j: (l, 0, j))],
        out_specs=pl.BlockSpec((1, rows, tn), lambda l, j: (l, 0, j)),
        out_shape=jax.ShapeDtypeStruct((DEPTH, rows, 3 * D_MODEL), F32),
        compiler_params=_cparams(("arbitrary", "arbitrary")),
        name="modulation",
    )(vecs, w_mod, b_mod.reshape(DEPTH, 1, 3 * D_MODEL))


_EVEN_SECTIONS = (("q", 0, NA_WIDTH, "qnorm"), ("k", NA_WIDTH, 2 * NA_WIDTH, "knorm"),
                  ("v", 2 * NA_WIDTH, 3 * NA_WIDTH, "copy"),
                  ("hy", 3 * NA_WIDTH, 3 * NA_WIDTH + 3 * HY_WIDTH, "copy"),
                  ("gate", IN_WIDTH - D_MODEL, IN_WIDTH, "silu"))
_ODD_SECTIONS = (("q", 0, DIFF_QK, "qnorm"), ("k", DIFF_QK, 2 * DIFF_QK, "knorm"),
                 ("v", 2 * DIFF_QK, 3 * DIFF_QK, "copy"),
                 ("gate", IN_WIDTH - D_MODEL, IN_WIDTH, "silu"))


def _rope_chunk(a, cos, sin_signed, low_half):
    up = pltpu.roll(a, LANES - HEAD_DIM // 2, axis=1)
    dn = pltpu.roll(a, HEAD_DIM // 2, axis=1)
    return a * cos + jnp.where(low_half, up, dn) * sin_signed


def _inproj_kernel(*refs, sections, rope):
    x_ref, shift_ref, scale_ref, g_ref, w_ref, qg_ref, kg_ref, e_ref = refs[:8]
    pos = 8
    if rope:
        cos_ref, sin_ref = refs[8:10]
        pos = 10
    out_refs = refs[pos:]
    x = x_ref[0]
    ms = jnp.mean(x * x, axis=-1, keepdims=True)
    h = x * lax.rsqrt(ms + EPS) * g_ref[...] * (1.0 + scale_ref[0]) + shift_ref[0]
    hb = h.astype(BF16)
    if rope:
        cos = cos_ref[...]
        sin_signed = sin_ref[...]
        lane = lax.broadcasted_iota(jnp.int32, cos.shape, 1)
        low_half = (lane % HEAD_DIM) < HEAD_DIM // 2
    for o_ref, (_, lo, hi, kind) in zip(out_refs, sections):
        for c0 in range(lo, hi, MXU_DIM):
            acc = _dot(hb, w_ref[:, c0:c0 + MXU_DIM])
            if kind in ("qnorm", "knorm"):
                gain = qg_ref[...] if kind == "qnorm" else kg_ref[...]
                ss = _dot((acc * acc).astype(BF16), e_ref[...])
                acc = acc * lax.rsqrt(ss * (1.0 / HEAD_DIM) + EPS) * gain
                if rope:
                    acc = jnp.concatenate(
                        [_rope_chunk(acc[:, j:j + LANES], cos, sin_signed, low_half)
                         for j in range(0, MXU_DIM, LANES)], axis=1)
            elif kind == "silu":
                acc = acc * jax.nn.sigmoid(acc)
            o_ref[0, :, c0 - lo:c0 - lo + MXU_DIM] = acc.astype(o_ref.dtype)


def _rope_tables(n_tokens):
    t = jnp.arange(n_tokens, dtype=jnp.int32)
    row = (t // GRID_W).astype(F32)
    col = (t % GRID_W).astype(F32)
    n_freq = HEAD_DIM // 4
    inv = ROPE_BASE ** (-jnp.arange(n_freq, dtype=F32) / n_freq)
    ang = jnp.concatenate([row[:, None] * inv, col[:, None] * inv], axis=-1)
    cos, sin = jnp.cos(ang), jnp.sin(ang)
    cos_t = jnp.concatenate([cos, cos, cos, cos], axis=-1)
    sin_t = jnp.concatenate([-sin, sin, -sin, sin], axis=-1)
    return cos_t, sin_t


def _inproj(x, shift, scale, g, w, qg, kg, *, odd, rope_tabs=None):
    B, S, _ = x.shape
    tm = min(S, 512)
    sections = _ODD_SECTIONS if odd else _EVEN_SECTIONS
    rope = rope_tabs is not None
    head = jnp.arange(MXU_DIM) // HEAD_DIM
    e = (head[:, None] == head[None, :]).astype(BF16)
    q_scale = HEAD_DIM ** -0.5 * LOG2E
    qg_t = jnp.tile(qg.astype(F32), MXU_DIM // HEAD_DIM)[None] * q_scale
    kg_t = jnp.tile(kg.astype(F32), MXU_DIM // HEAD_DIM)[None]
    const = lambda b, i: (0, 0)
    in_specs = [pl.BlockSpec((1, tm, D_MODEL), lambda b, i: (b, i, 0)),
                pl.BlockSpec((1, 1, D_MODEL), lambda b, i: (b, 0, 0)),
                pl.BlockSpec((1, 1, D_MODEL), lambda b, i: (b, 0, 0)),
                pl.BlockSpec((1, D_MODEL), const),
                _single((D_MODEL, IN_WIDTH), const),
                pl.BlockSpec((1, MXU_DIM), const),
                pl.BlockSpec((1, MXU_DIM), const),
                pl.BlockSpec((MXU_DIM, MXU_DIM), const)]
    args = [x, shift[:, None], scale[:, None], g[None], w, qg_t, kg_t, e]
    if rope:
        in_specs += [pl.BlockSpec((tm, LANES), lambda b, i: (i, 0))] * 2
        args += list(rope_tabs)
    out_shape = [jax.ShapeDtypeStruct((B, S, hi - lo), BF16) for _, lo, hi, _ in sections]
    out_specs = [pl.BlockSpec((1, tm, hi - lo), lambda b, i: (b, i, 0)) for _, lo, hi, _ in sections]
    return pl.pallas_call(
        functools.partial(_inproj_kernel, sections=sections, rope=rope),
        grid=(B, S // tm),
        in_specs=in_specs, out_specs=out_specs, out_shape=out_shape,
        compiler_params=_cparams(("arbitrary", "arbitrary")),
        name="inproj_odd" if odd else "inproj_even",
    )(*args)


def _outproj_kernel(*refs, n_parts):
    x_ref, gm_ref, gate_ref, w_ref = refs[:4]
    parts = refs[4:4 + n_parts]
    o_ref = refs[4 + n_parts]
    y = jnp.concatenate([p[0].astype(F32) for p in parts], axis=1) * gate_ref[0].astype(F32)
    o_ref[0] = x_ref[0] + gm_ref[0] * _dot(y.astype(BF16), w_ref[...])


def _outproj(x, gate_mod, gate, w, parts):
    B, S, _ = x.shape
    tm = min(S, 512)
    in_specs = [pl.BlockSpec((1, tm, D_MODEL), lambda b, i: (b, i, 0)),
                pl.BlockSpec((1, 1, D_MODEL), lambda b, i: (b, 0, 0)),
                pl.BlockSpec((1, tm, D_MODEL), lambda b, i: (b, i, 0)),
                _single((D_MODEL, D_MODEL), lambda b, i: (0, 0))]
    in_specs += [pl.BlockSpec((1, tm, p.shape[-1]), lambda b, i: (b, i, 0)) for p in parts]
    return pl.pallas_call(
        functools.partial(_outproj_kernel, n_parts=len(parts)),
        grid=(B, S // tm),
        in_specs=in_specs,
        out_specs=pl.BlockSpec((1, tm, D_MODEL), lambda b, i: (b, i, 0)),
        out_shape=jax.ShapeDtypeStruct((B, S, D_MODEL), F32),
        compiler_params=_cparams(("arbitrary", "arbitrary")),
        name="outproj",
    )(x, gate_mod[:, None], gate, w, *parts)


def _head_masks(shape):
    lane = lax.broadcasted_iota(jnp.int32, shape, 1)
    return lane < HEAD_DIM, lane >= HEAD_DIM


def _split_heads_t(qt):
    row = lax.broadcasted_iota(jnp.int32, qt.shape, 0)
    zero = jnp.zeros_like(qt)
    return jnp.concatenate([jnp.where(row < HEAD_DIM, qt, zero), jnp.where(row >= HEAD_DIM, qt, zero)], axis=1)


def _na_kernel(q_ref, k_ref, v_ref, kc_ref, vc_ref, bias_ref, o_ref, *scratch, n_rows):
    s_bufs = scratch[:len(scratch) // 2]
    p_bufs = scratch[len(scratch) // 2:]
    i = pl.program_id(2)
    start = jnp.clip(i * NA_Q_ROWS - WIN_R // 2, 0, n_rows - NA_WIN_ROWS)
    chunk = NA_CHUNK_ROWS * GRID_W
    off = pl.multiple_of(start * GRID_W, chunk)
    tq = q_ref.shape[1]
    q2 = _split_heads_t(q_ref[0].T)
    n_loc = NA_WIN_ROWS // NA_CHUNK_ROWS
    cmax = []
    for t in range(n_loc):
        s = _dot(k_ref[0, pl.ds(off + t * chunk, chunk), :], q2) + bias_ref[0, 0, t * chunk:(t + 1) * chunk, :]
        s_bufs[t][...] = s
        cmax.append(jnp.max(s, axis=0, keepdims=True))
    s = _dot(kc_ref[0], q2)
    s_bufs[n_loc][...] = s
    cmax.append(jnp.max(s, axis=0, keepdims=True))
    m = acc = None
    for t in range(n_loc + 1):
        m_new = cmax[t] if m is None else jnp.maximum(m, cmax[t])
        p_bufs[t][...] = jnp.exp2((s_bufs[t][...] - m_new).astype(BF16))
        v_t = v_ref[0, pl.ds(off + t * chunk, chunk), :] if t < n_loc else vc_ref[0]
        ones = jnp.ones((ONES_ROWS, v_t.shape[0]), v_t.dtype)
        pv = _dot(jnp.concatenate([v_t.T, ones], axis=0), p_bufs[t][...])
        acc = pv if acc is None else jnp.exp2(m - m_new) * acc + pv
        m = m_new
    o = jnp.concatenate([acc[:HEAD_DIM, :tq] / acc[LANES:LANES + 1, :tq],
                         acc[HEAD_DIM:LANES, tq:] / acc[LANES:LANES + 1, tq:]], axis=0)
    o_ref[0] = o.T.astype(o_ref.dtype)


def _rpb_cols_kernel(r_ref, p_ref, m_ref, o_ref):
    gathered = jnp.dot(r_ref[...], p_ref[...], precision=HIGHEST, preferred_element_type=F32)
    o_ref[...] = gathered * LOG2E + m_ref[...]


def _na_bias_kernel(cols_ref, o_ref, *, drow_tables):
    masked = jnp.full((GRID_W, GRID_W), NEG_BIG, F32)
    for t, drow in enumerate(drow_tables):
        @pl.when(pl.program_id(1) == t)
        def _(drow=drow):
            for j in range(NA_WIN_ROWS):
                for h2 in range(2):
                    for a in range(0, NA_Q_ROWS, 2):
                        pair = [masked if drow[b][j] is None else cols_ref[h2, drow[b][j]] for b in (a, a + 1)]
                        lane0 = (h2 * NA_Q_ROWS + a) * GRID_W
                        o_ref[0, 0, j * GRID_W:(j + 1) * GRID_W, lane0:lane0 + 2 * GRID_W] = (
                            jnp.concatenate(pair, axis=1))


def _na_bias_table(rpb, n_rows):
    n_heads, n_drow, n_dcol = rpb.shape
    n_blk = n_rows // NA_Q_ROWS
    kr = min(WIN_R, n_rows)
    c = np.arange(GRID_W)
    cs = np.clip(c - WIN_C // 2, 0, GRID_W - WIN_C)
    col_ok = (c[None, :] >= cs[:, None]) & (c[None, :] < cs[:, None] + WIN_C)
    dcol = np.clip(c[None, :] - c[:, None] + WIN_C - 1, 0, 2 * WIN_C - 2)
    pad = -n_dcol % 8
    onehot = (np.arange(n_dcol + pad)[:, None] == dcol.T.reshape(1, -1)).astype(np.float32)
    col_mask = np.where(col_ok.T, 0.0, NEG_BIG).astype(np.float32).reshape(1, -1)
    r2 = jnp.pad(rpb.astype(F32).reshape(n_heads * n_drow, n_dcol), ((0, 0), (0, pad)))
    cols = pl.pallas_call(
        _rpb_cols_kernel,
        out_shape=jax.ShapeDtypeStruct((n_heads * n_drow, GRID_W * GRID_W), F32),
        name="rpb_cols",
    )(r2, jnp.asarray(onehot), jnp.asarray(col_mask))
    cols = cols.reshape(n_heads, n_drow, GRID_W, GRID_W)
    drow_tables = []
    for i in (0, 1, n_blk - 1):
        r0 = i * NA_Q_ROWS
        start = min(max(r0 - WIN_R // 2, 0), n_rows - NA_WIN_ROWS)
        table = []
        for a in range(NA_Q_ROWS):
            r = r0 + a
            rs = min(max(r - kr // 2, 0), n_rows - kr)
            table.append(tuple(start + j - r + WIN_R - 1 if rs <= start + j < rs + kr else None
                               for j in range(NA_WIN_ROWS)))
        drow_tables.append(tuple(table))
    tq, tk = NA_Q_ROWS * GRID_W, NA_WIN_ROWS * GRID_W
    return pl.pallas_call(
        functools.partial(_na_bias_kernel, drow_tables=tuple(drow_tables)),
        grid=(n_heads // 2, len(drow_tables)),
        in_specs=[pl.BlockSpec((2, n_drow, GRID_W, GRID_W), lambda hp, t: (hp, 0, 0, 0))],
        out_specs=pl.BlockSpec((1, 1, tk, 2 * tq), lambda hp, t: (hp, t, 0, 0)),
        out_shape=jax.ShapeDtypeStruct((n_heads // 2, len(drow_tables), tk, 2 * tq), F32),
        compiler_params=_cparams(("arbitrary", "arbitrary")),
        name="na_bias",
    )(cols)


def _na_attention(q, k, v, kc, vc, rpb):
    B, S, _ = q.shape
    n_ctx = kc.shape[1]
    n_rows = S // GRID_W
    n_blk = n_rows // NA_Q_ROWS
    assert n_rows % NA_Q_ROWS == 0 and n_rows >= NA_WIN_ROWS
    assert all(x % NA_CHUNK_ROWS == 0 for x in (NA_Q_ROWS, NA_WIN_ROWS, WIN_R // 2, n_rows))
    tq = NA_Q_ROWS * GRID_W
    tk = NA_WIN_ROWS * GRID_W
    chunk = NA_CHUNK_ROWS * GRID_W
    bias = _na_bias_table(rpb, n_rows)

    def bias_map(b, hp, i):
        return (hp, jnp.where(i == 0, 0, jnp.where(i == n_blk - 1, 2, 1)), 0, 0)

    return pl.pallas_call(
        functools.partial(_na_kernel, n_rows=n_rows),
        grid=(B, NA_WIDTH // LANES, n_blk),
        in_specs=[pl.BlockSpec((1, tq, LANES), lambda b, hp, i: (b, i, hp)),
                  pl.BlockSpec((1, S, LANES), lambda b, hp, i: (b, 0, hp)),
                  pl.BlockSpec((1, S, LANES), lambda b, hp, i: (b, 0, hp)),
                  pl.BlockSpec((1, n_ctx, LANES), lambda b, hp, i: (b, 0, hp)),
                  pl.BlockSpec((1, n_ctx, LANES), lambda b, hp, i: (b, 0, hp)),
                  pl.BlockSpec((1, 1, tk, 2 * tq), bias_map)],
        out_specs=pl.BlockSpec((1, tq, LANES), lambda b, hp, i: (b, i, hp)),
        out_shape=jax.ShapeDtypeStruct((B, S, NA_WIDTH), BF16),
        scratch_shapes=[pltpu.VMEM((n, 2 * tq), dt) for dt in (F32, BF16)
                        for n in [chunk] * (NA_WIN_ROWS // NA_CHUNK_ROWS) + [n_ctx]],
        compiler_params=_cparams(("arbitrary", "arbitrary", "arbitrary")),
        name="na_attention",
    )(q, k, v, kc, vc, bias)


def _ctx_attn_kernel(q_ref, k_ref, v_ref, o_ref):
    q = q_ref[0]
    k = k_ref[0]
    v = v_ref[0]
    outs = []
    for sel in _head_masks(q.shape):
        qh = jnp.where(sel, q, jnp.zeros_like(q))
        s = _dot_nt(qh, k)
        p = jnp.exp2(s - jnp.max(s, axis=-1, keepdims=True))
        outs.append(_dot(p.astype(BF16), v) / jnp.sum(p, axis=-1, keepdims=True))
    lo_half, _ = _head_masks(outs[0].shape)
    o_ref[0] = jnp.where(lo_half, outs[0], outs[1]).astype(o_ref.dtype)


def _ctx_attention(q, k, v):
    B, n, width = q.shape
    spec = pl.BlockSpec((1, n, LANES), lambda b, hp: (b, 0, hp))
    return pl.pallas_call(
        _ctx_attn_kernel,
        grid=(B, width // LANES),
        in_specs=[spec, spec, spec], out_specs=spec,
        out_shape=jax.ShapeDtypeStruct((B, n, width), BF16),
        compiler_params=_cparams(("arbitrary", "arbitrary")),
        name="ctx_attention",
    )(q, k, v)


def _diff_kernel(*refs, tk, sk, se, lam_init):
    lq1_ref, lk1_ref, lq2_ref, lk2_ref, sg_ref, q_ref, k_ref, v_ref = refs[:8]
    pos = 8
    if se:
        ke_ref, ve_ref = refs[8:10]
        pos = 10
    o_ref = refs[pos]
    scratch = refs[pos + 1:]
    s_bufs = scratch[:DIFF_BUFS]
    p_bufs = scratch[DIFF_BUFS:2 * DIFF_BUFS]
    acc_ref = scratch[2 * DIFF_BUFS]
    tq = q_ref.shape[1]
    q2 = _split_heads_t(q_ref[0].T)
    chunks = [(k_ref, v_ref, r, min(tk, sk - r)) for r in range(0, sk, tk)]
    if se:
        chunks += [(ke_ref, ve_ref, r, min(tk, se - r)) for r in range(0, se, tk)]

    def scores(j):
        kr, _, r, n = chunks[j]
        s = _dot(kr[0, r:r + n, :], q2)
        s_bufs[j % DIFF_BUFS][0:n, :] = s
        return jnp.max(s, axis=0, keepdims=True)

    ahead = DIFF_BUFS - 1
    cmax = [scores(j) for j in range(min(ahead, len(chunks)))]
    m = None
    for j, (_, vr, r, n) in enumerate(chunks):
        if j + ahead < len(chunks):
            cmax.append(scores(j + ahead))
        m_new = cmax[j] if m is None else jnp.maximum(m, cmax[j])
        p_buf = p_bufs[j % DIFF_BUFS]
        p_buf[0:n, :] = jnp.exp2((s_bufs[j % DIFF_BUFS][0:n, :] - m_new).astype(BF16))
        vt = jnp.concatenate([vr[0, r:r + n, :].T, jnp.ones((ONES_ROWS, n), BF16)], axis=0)
        pv = _dot(vt, p_buf[0:n, :])
        acc_ref[...] = pv if m is None else jnp.exp2(m - m_new) * acc_ref[...] + pv
        m = m_new
    l = acc_ref[LANES:LANES + 1, :]
    acc = acc_ref[:LANES, :]

    lam = (jnp.exp(jnp.sum(lq1_ref[...] * lk1_ref[...], axis=-1, keepdims=True))
           - jnp.exp(jnp.sum(lq2_ref[...] * lk2_ref[...], axis=-1, keepdims=True)) + lam_init)
    o = acc / l
    o = o[:, :tq] - lam * o[:, tq:]
    o = o * lax.rsqrt(jnp.mean(o * o, axis=0, keepdims=True) + EPS) * sg_ref[...]
    o_ref[0] = (o * (1.0 - lam_init)).T.astype(o_ref.dtype)


def _diff_attention(q, k, v, lam_vecs, subln_g, lam_init, k_extra=None, v_extra=None):
    B, sq, width = q.shape
    sk = k.shape[1]
    tq = min(sq, DIFF_TQ)
    tk = min(sk, DIFF_TK)
    se = 0 if k_extra is None else k_extra.shape[1]
    small = pl.BlockSpec((1, HEAD_DIM), lambda b, h, i: (0, 0))
    in_specs = [small] * 4 + [
        pl.BlockSpec((LANES, 1), lambda b, h, i: (0, 0)),
        pl.BlockSpec((1, tq, LANES), lambda b, h, i: (b, i, h)),
        pl.BlockSpec((1, sk, LANES), lambda b, h, i: (b, 0, h)),
        pl.BlockSpec((1, sk, LANES), lambda b, h, i: (b, 0, h))]
    args = [a.astype(F32)[None] for a in lam_vecs] + [subln_g.astype(F32)[:, None], q, k, v]
    if se:
        in_specs += [pl.BlockSpec((1, se, LANES), lambda b, h, i: (b, 0, h))] * 2
        args += [k_extra, v_extra]
    return pl.pallas_call(
        functools.partial(_diff_kernel, tk=tk, sk=sk, se=se, lam_init=lam_init),
        grid=(B, width // LANES, sq // tq),
        in_specs=in_specs,
        out_specs=pl.BlockSpec((1, tq, LANES), lambda b, h, i: (b, i, h)),
        out_shape=jax.ShapeDtypeStruct((B, sq, width), BF16),
        scratch_shapes=[pltpu.VMEM((tk, 2 * tq), F32)] * DIFF_BUFS + [pltpu.VMEM((tk, 2 * tq), BF16)] * DIFF_BUFS
        + [pltpu.VMEM((LANES + ONES_ROWS, 2 * tq), F32)],
        compiler_params=_cparams(("arbitrary", "arbitrary", "arbitrary")),
        name="diff_attention",
    )(*args)


def _filter_kernel(z_ref, w1_ref, b1_ref, fr_ref, w2_ref, b2_ref, w3_ref, b3_ref, dec_ref, h_ref, ss_ref):
    hdot = functools.partial(jnp.dot, precision=HIGHEST, preferred_element_type=F32)
    fr = fr_ref[...]
    h = jnp.sin(fr * (hdot(z_ref[...], w1_ref[...]) + b1_ref[...]))
    h = jnp.sin(fr * (hdot(h, w2_ref[...]) + b2_ref[...]))
    h = hdot(h, w3_ref[...]) + b3_ref[...]
    dec = dec_ref[...]
    h = h * jnp.concatenate([dec] * (2 * HY_ORDER), axis=1)
    h_ref[...] = h

    @pl.when(pl.program_id(0) == 0)
    def _():
        ss_ref[...] = jnp.zeros(ss_ref.shape, F32)

    ss_ref[...] += jnp.sum(h * h, axis=0, keepdims=True)


def _hyena_filters(L, w1, b1, freq, w2, b2, w3, b3):
    t = jnp.linspace(0.0, 1.0, L, dtype=F32)[:, None]
    w = (2.0 * math.pi / L) * jnp.arange(L, dtype=F32)[:, None]
    bands = (HY_EMB - 1) // 2
    fb = jnp.linspace(1e-4, bands - 1, bands, dtype=F32)[None, :]
    z = jnp.concatenate([t, jnp.cos(fb * w), -jnp.sin(fb * w)], axis=-1)
    emb_pad = HY_HIDDEN - HY_EMB
    z = jnp.pad(z, ((0, 0), (0, emb_pad)))
    w1p = jnp.pad(w1.astype(F32), ((0, emb_pad), (0, 0)))
    min_decay = math.log(HY_TARGET) / HY_SLOW_DECAY
    max_decay = math.log(HY_TARGET) / HY_FAST_DECAY
    deltas = jnp.abs(jnp.linspace(min_decay, max_decay, HY_WIDTH, dtype=F32))
    decay = jnp.exp(-t * deltas[None, :])
    tl = min(L, 512)
    width = HY_ORDER * 2 * HY_WIDTH
    const = lambda i: (0, 0)
    return pl.pallas_call(
        _filter_kernel,
        grid=(L // tl,),
        in_specs=[pl.BlockSpec((tl, HY_HIDDEN), lambda i: (i, 0)),
                  pl.BlockSpec((HY_HIDDEN, HY_HIDDEN), const), pl.BlockSpec((1, HY_HIDDEN), const),
                  pl.BlockSpec((1, HY_HIDDEN), const),
                  pl.BlockSpec((HY_HIDDEN, HY_HIDDEN), const), pl.BlockSpec((1, HY_HIDDEN), const),
                  pl.BlockSpec((HY_HIDDEN, width), const), pl.BlockSpec((1, width), const),
                  pl.BlockSpec((tl, HY_WIDTH), lambda i: (i, 0))],
        out_specs=[pl.BlockSpec((tl, width), lambda i: (i, 0)), pl.BlockSpec((1, width), const)],
        out_shape=[jax.ShapeDtypeStruct((L, width), F32), jax.ShapeDtypeStruct((1, width), F32)],
        compiler_params=_cparams(("arbitrary",)),
        name="hyena_filters",
    )(z, w1p, b1.astype(F32)[None], freq.astype(F32)[None], w2.astype(F32), b2.astype(F32)[None],
      w3.astype(F32), b3.astype(F32)[None], decay)


def _fft_sizes(L):
    n2 = 128 if L >= 1024 else 32
    n1 = 2 * L // n2
    return n1, n2


@functools.lru_cache(maxsize=None)
def _dft_tables(L):
    n1s, n2s = _fft_sizes(L)
    N = 2 * L
    k1 = np.arange(n1s)
    n1 = np.arange(n1s // 2)
    n2 = np.arange(n2s)
    th1 = 2.0 * np.pi * ((k1[:, None] * n1[None, :]) % n1s) / n1s
    c, s = np.cos(th1), np.sin(th1)
    eye = np.eye(HY_ROWS)
    lk = np.kron(np.block([[c, s], [-s, c]]), eye)
    lki = np.kron(np.block([[c.T, -s.T], [s.T, c.T]]), eye)
    tht = 2.0 * np.pi * ((k1[:, None] * n2[None, :]) % N) / N
    tw = tht.reshape(n1s, n2s // HY_ROWS, HY_ROWS).transpose(1, 0, 2).reshape(n2s // HY_ROWS, n1s * HY_ROWS, 1)
    twc = np.broadcast_to(np.cos(tw), tw.shape[:2] + (LANES,))
    tws = np.broadcast_to(np.sin(tw), tw.shape[:2] + (LANES,))
    th3 = 2.0 * np.pi * ((n2[:, None] * n2[None, :]) % n2s) / n2s
    c3, s3 = np.cos(th3), np.sin(th3)
    l3 = np.block([[c3, s3], [-s3, c3]])
    l3i = np.block([[c3, -s3], [s3, c3]])
    return tuple(np.asarray(a, np.float32) for a in (lk, twc, tws, l3, l3i, lki))


def _tiles(buf, n, g):
    rows = pl.ds(pl.multiple_of(g * HY_ROWS, HY_ROWS), HY_ROWS)
    return jnp.concatenate(
        [jnp.concatenate([buf[h, i, rows, :] for i in range(n)], axis=0) for h in range(2)], axis=1)


def _put_tiles(buf, n, g, val):
    rows = pl.ds(pl.multiple_of(g * HY_ROWS, HY_ROWS), HY_ROWS)
    for h in range(2):
        for i in range(n):
            buf[h, i, rows, :] = val[i * HY_ROWS:(i + 1) * HY_ROWS, h * LANES:(h + 1) * LANES]


def _twiddle(twc_ref, tws_ref, g):
    c = twc_ref[g].astype(F32)
    s = tws_ref[g].astype(F32)
    return jnp.concatenate([c, c], axis=1), jnp.concatenate([s, s], axis=1)


def _fft_stage1(zr, zi, ar, ai, lk_ref, twc_ref, tws_ref, n1s, n2s):
    half = n1s // 2
    m = n1s * HY_ROWS

    def body(g, carry):
        d = jnp.concatenate([_tiles(zr, half, g), _tiles(zi, half, g)], axis=0)
        out = _dot(lk_ref[...], d.astype(BF16))
        c, s = _twiddle(twc_ref, tws_ref, g)
        o_r, o_i = out[:m], out[m:]
        _put_tiles(ar, n1s, g, o_r * c + o_i * s)
        _put_tiles(ai, n1s, g, o_i * c - o_r * s)
        return carry

    lax.fori_loop(0, n2s // HY_ROWS, body, 0, unroll=4)


def _fft_stage3(ar, ai, l3_ref, k1):
    d = jnp.concatenate([jnp.concatenate([ar[0, k1], ar[1, k1]], axis=1),
                         jnp.concatenate([ai[0, k1], ai[1, k1]], axis=1)], axis=0)
    return _dot(l3_ref[...], d.astype(BF16))


def _put_rows(buf, k1, val):
    buf[0, k1] = val[:, :LANES]
    buf[1, k1] = val[:, LANES:]


def _spectrum_kernel(hf_ref, hb_ref, ssf_ref, ssb_ref, skip_ref, lk_ref, twc_ref, tws_ref, l3_ref, gr_ref, gi_ref,
                     zr, zi, ar, ai, *, n1s, n2s):
    nrm = lax.rsqrt(ssf_ref[...] + ssb_ref[...] + EPS)
    hf = hf_ref[...] * nrm
    hb = hb_ref[...] * nrm
    shape = zr.shape[1:]
    zr[0] = (hf + hb).reshape(shape)
    zr[1] = (hf - hb).reshape(shape)
    zi[...] = jnp.zeros(zi.shape, F32)
    _fft_stage1(zr, zi, ar, ai, lk_ref, twc_ref, tws_ref, n1s, n2s)
    inv_n = 1.0 / (n1s * n2s)

    def body(k1, carry):
        z = _fft_stage3(ar, ai, l3_ref, k1)
        r = pl.multiple_of(k1 * n2s, n2s)
        gr_ref[0, pl.ds(r, n2s), :] = (z[:n2s, :LANES] + skip_ref[0]) * inv_n
        gi_ref[0, pl.ds(r, n2s), :] = z[n2s:, LANES:] * inv_n
        return carry

    lax.fori_loop(0, n1s, body, 0, unroll=HY_UNROLL)


def _hyena_scratch(L):
    n1s, n2s = _fft_sizes(L)
    return [pltpu.VMEM((2, n1s // 2, n2s, LANES), F32), pltpu.VMEM((2, n1s // 2, n2s, LANES), F32),
            pltpu.VMEM((2, n1s, n2s, LANES), F32), pltpu.VMEM((2, n1s, n2s, LANES), F32)]


def _table_specs(tables):
    zeros = {2: (lambda *_: (0, 0)), 3: (lambda *_: (0, 0, 0))}
    return [_single(t.shape, zeros[t.ndim]) for t in tables]


def _hyena_spectra(h, ss, skip):
    L = h.shape[0]
    n1s, n2s = _fft_sizes(L)
    N = 2 * L
    tables = _dft_tables(L)[:4]
    n_ct = HY_WIDTH // LANES
    fwd = lambda o, ct: (0, o * 2 * n_ct + ct)
    bwd = lambda o, ct: (0, o * 2 * n_ct + n_ct + ct)
    out_spec = _single((1, N, LANES), lambda o, ct: (o, 0, ct))
    return pl.pallas_call(
        functools.partial(_spectrum_kernel, n1s=n1s, n2s=n2s),
        grid=(HY_ORDER, n_ct),
        in_specs=[_single((L, LANES), fwd), _single((L, LANES), bwd),
                  pl.BlockSpec((1, LANES), fwd), pl.BlockSpec((1, LANES), bwd),
                  pl.BlockSpec((1, 1, LANES), lambda o, ct: (o, 0, ct))] + _table_specs(tables),
        out_specs=[out_spec, out_spec],
        out_shape=[jax.ShapeDtypeStruct((HY_ORDER, N, HY_WIDTH), F32)] * 2,
        scratch_shapes=_hyena_scratch(L),
        compiler_params=_cparams(("arbitrary", "arbitrary")),
        name="hyena_spectra",
    )(h, h, ss, ss, skip[:, None], *[jnp.asarray(t, BF16) for t in tables])


def _short_conv(u, w_ref, b_ref):
    L, C = u.shape
    t = u.reshape(L // HY_ROWS, HY_ROWS, C)
    sub = lax.broadcasted_iota(jnp.int32, (1, HY_ROWS, C), 1)
    zero_tile = jnp.zeros((1, HY_ROWS, C), u.dtype)
    down = pltpu.roll(t, 1, axis=1)
    prev = jnp.where(sub == 0, jnp.concatenate([zero_tile, down[:-1]], axis=0), down).reshape(L, C)
    up = pltpu.roll(t, HY_ROWS - 1, axis=1)
    nxt = jnp.where(sub == HY_ROWS - 1, jnp.concatenate([up[1:], zero_tile], axis=0), up).reshape(L, C)
    return b_ref[...] + prev * w_ref[0:1, :] + u * w_ref[1:2, :] + nxt * w_ref[2:3, :]


def _conv_kernel(a_ref, x_ref, wa_ref, ba_ref, wx_ref, bx_ref, gr_ref, gi_ref,
                 lk_ref, twc_ref, tws_ref, l3_ref, l3i_ref, lki_ref, o_ref, zr, zi, ar, ai, *, conv_a, n1s, n2s):
    half = n1s // 2
    shape = zr.shape[1:]
    slots = ((zr, 0), (zr, 1), (zi, 0), (zi, 1))
    for s, (buf, hi) in enumerate(slots):
        u = a_ref[s].astype(F32)
        buf[hi] = (_short_conv(u, wa_ref, ba_ref) if conv_a else u).reshape(shape)
    _fft_stage1(zr, zi, ar, ai, lk_ref, twc_ref, tws_ref, n1s, n2s)

    def mid(k1, carry):
        z = _fft_stage3(ar, ai, l3_ref, k1)
        r = pl.multiple_of(k1 * n2s, n2s)
        g_r = gr_ref[0, pl.ds(r, n2s), :]
        g_i = gi_ref[0, pl.ds(r, n2s), :]
        g_r = jnp.concatenate([g_r, g_r], axis=1)
        g_i = jnp.concatenate([g_i, g_i], axis=1)
        z_r, z_i = z[:n2s], z[n2s:]
        p = jnp.concatenate([z_r * g_r - z_i * g_i, z_r * g_i + z_i * g_r], axis=0)
        b = _dot(l3i_ref[...], p.astype(BF16))
        _put_rows(ar, k1, b[:n2s])
        _put_rows(ai, k1, b[n2s:])
        return carry

    lax.fori_loop(0, n1s, mid, 0, unroll=HY_UNROLL)
    m = n1s * HY_ROWS

    def last(g, carry):
        c, s = _twiddle(twc_ref, tws_ref, g)
        b_r, b_i = _tiles(ar, n1s, g), _tiles(ai, n1s, g)
        d = jnp.concatenate([b_r * c - b_i * s, b_i * c + b_r * s], axis=0)
        y = _dot(lki_ref[...], d.astype(BF16))
        _put_tiles(zr, half, g, y[:m // 2])
        _put_tiles(zi, half, g, y[m // 2:])
        return carry

    lax.fori_loop(0, n2s // HY_ROWS, last, 0, unroll=4)
    for s, (buf, hi) in enumerate(slots):
        xg = _short_conv(x_ref[s].astype(F32), wx_ref, bx_ref)
        o_ref[s] = (xg * buf[hi].reshape(xg.shape)).astype(o_ref.dtype)


def _hyena_conv(a, a_col, x, x_col, conv_w, conv_b, wa_col, wx_col, g_r, g_i, order, *, conv_a, out_dtype):
    B, L, _ = a.shape
    n1s, n2s = _fft_sizes(L)
    N = 2 * L
    n_ct = HY_WIDTH // LANES
    seqs = 4
    assert B % seqs == 0
    tables = _dft_tables(L)
    return pl.pallas_call(
        functools.partial(_conv_kernel, conv_a=conv_a, n1s=n1s, n2s=n2s),
        grid=(n_ct, B // seqs),
        in_specs=[_single((seqs, L, LANES), lambda ct, b: (b, 0, a_col + ct)),
                  _single((seqs, L, LANES), lambda ct, b: (b, 0, x_col + ct)),
                  pl.BlockSpec((3, LANES), lambda ct, b: (0, wa_col + ct)),
                  pl.BlockSpec((1, LANES), lambda ct, b: (0, wa_col + ct)),
                  pl.BlockSpec((3, LANES), lambda ct, b: (0, wx_col + ct)),
                  pl.BlockSpec((1, LANES), lambda ct, b: (0, wx_col + ct)),
                  _single((1, N, LANES), lambda ct, b: (order, 0, ct)),
                  _single((1, N, LANES), lambda ct, b: (order, 0, ct))] + _table_specs(tables),
        out_specs=_single((seqs, L, LANES), lambda ct, b: (b, 0, ct)),
        out_shape=jax.ShapeDtypeStruct((B, L, HY_WIDTH), out_dtype),
        scratch_shapes=_hyena_scratch(L),
        compiler_params=_cparams(("arbitrary", "arbitrary")),
        name="hyena_conv",
    )(a, x, conv_w, conv_b[None], conv_w, conv_b[None], g_r, g_i, *[jnp.asarray(t, BF16) for t in tables])


def _hyena(hy, conv_w, conv_b, filt, skip):
    L = hy.shape[1]
    n_ct = HY_WIDTH // LANES
    h, ss = _hyena_filters(L, *filt)
    g_r, g_i = _hyena_spectra(h, ss, skip.astype(F32))
    conv_w = conv_w.astype(F32)
    conv_b = conv_b.astype(F32)
    z1 = _hyena_conv(hy, 0, hy, n_ct, conv_w, conv_b, 0, n_ct, g_r, g_i, 0, conv_a=True, out_dtype=BF16)
    return _hyena_conv(z1, 0, hy, 2 * n_ct, conv_w, conv_b, 0, 2 * n_ct, g_r, g_i, 1, conv_a=False, out_dtype=BF16)


def kernel(x, c, ctx, c_ctx, norm_g, w_mod, b_mod, w_in, w_out, q_norm_g, k_norm_g, na_rpb, hy_conv_w, hy_conv_b, hy_filt_w1, hy_filt_b1, hy_filt_freq, hy_filt_w2, hy_filt_b2, hy_filt_w3, hy_filt_b3, hy_skip, diff_lam_q1, diff_lam_k1, diff_lam_q2, diff_lam_k2, diff_subln_g):
    B, S, D = x.shape
    mod_rows = -(-(B + 1) // 8) * 8
    vecs = jnp.concatenate([c, c_ctx[None], jnp.zeros((mod_rows - B - 1, D), F32)], axis=0)
    mods = _modulation(vecs, w_mod, b_mod)
    w_in_b = w_in.astype(BF16)
    w_out_b = w_out.astype(BF16)
    rope_tabs = _rope_tables(S)
    xc = ctx
    for l in range(DEPTH):
        ctx_out = l < DEPTH - 1
        shift, scale, gate = (mods[l, :B, j * D:(j + 1) * D] for j in range(3))
        c_shift, c_scale, c_gate = (jnp.broadcast_to(mods[l, B:B + 1, j * D:(j + 1) * D], (B, D)) for j in range(3))
        odd = l % 2 == 1
        lat = _inproj(x, shift, scale, norm_g[l], w_in_b[l], q_norm_g[l], k_norm_g[l], odd=odd,
                      rope_tabs=rope_tabs if odd else None)
        cx = _inproj(xc, c_shift, c_scale, norm_g[l], w_in_b[l], q_norm_g[l], k_norm_g[l], odd=odd)
        if not odd:
            e = l // 2
            q, k, v, hy, g = lat
            qc, kc, vc, hyc, gc = cx
            filt = (hy_filt_w1[e], hy_filt_b1[e], hy_filt_freq[e], hy_filt_w2[e], hy_filt_b2[e],
                    hy_filt_w3[e], hy_filt_b3[e])
            o_na = _na_attention(q, k, v, kc, vc, na_rpb[e])
            o_hy = _hyena(hy, hy_conv_w[e], hy_conv_b[e], filt, hy_skip[e])
            parts = [o_na, o_hy]
            if ctx_out:
                c_parts = [_ctx_attention(qc, kc, vc), _hyena(hyc, hy_conv_w[e], hy_conv_b[e], filt, hy_skip[e])]
        else:
            o_i = l // 2
            lam_init = 0.8 - 0.6 * math.exp(-0.3 * l)
            lam_vecs = (diff_lam_q1[o_i], diff_lam_k1[o_i], diff_lam_q2[o_i], diff_lam_k2[o_i])
            q, k, v, g = lat
            qc, kc, vc, gc = cx
            parts = [_diff_attention(q, k, v, lam_vecs, diff_subln_g[o_i], lam_init, kc, vc)]
            if ctx_out:
                c_parts = [_diff_attention(qc, kc, vc, lam_vecs, diff_subln_g[o_i], lam_init)]
        x = _outproj(x, gate, g, w_out_b[l], parts)
        if ctx_out:
            xc = _outproj(xc, c_gate, gc, w_out_b[l], c_parts)
    return x
```

```python
import functools
import math

import numpy as np
import jax
import jax.numpy as jnp
from jax import lax
from jax.experimental import pallas as pl
from jax.experimental.pallas import tpu as pltpu

F32 = jnp.float32
BF16 = jnp.bfloat16
HIGHEST = lax.Precision.HIGHEST

D_MODEL = 1024
DEPTH = 4
GRID_W = 64
HEAD_DIM = 64
N_HEADS_NA = 8
NA_WIDTH = N_HEADS_NA * HEAD_DIM
HY_WIDTH = D_MODEL - NA_WIDTH
HY_ORDER = 2
HY_EMB = 33
HY_HIDDEN = 64
HY_FAST_DECAY = 0.3
HY_SLOW_DECAY = 1.5
HY_TARGET = 1e-2
WIN_R = 8
WIN_C = 16
N_HEADS_DIFF = D_MODEL // (2 * HEAD_DIM)
DIFF_QK = N_HEADS_DIFF * 2 * HEAD_DIM
IN_WIDTH = 4 * D_MODEL
EPS = 1e-6
ROPE_BASE = 10000.0

LANES = 128
MXU_DIM = 256
VMEM_LIMIT = 56 * 1024 * 1024
NEG_BIG = -1e30
LOG2E = math.log2(math.e)
ONES_ROWS = 16

NA_Q_ROWS = 8
NA_WIN_ROWS = 16
NA_CHUNK_ROWS = 4
DIFF_BUFS = 4
DIFF_TK = 256
DIFF_TQ = 512
HY_UNROLL = 16
HY_ROWS = 8


def _cparams(sem):
    return pltpu.CompilerParams(dimension_semantics=sem, vmem_limit_bytes=VMEM_LIMIT)


def _single(shape, index_map):
    return pl.BlockSpec(shape, index_map, pipeline_mode=pl.Buffered(1))


def _dot(a, b):
    return jnp.dot(a, b, preferred_element_type=F32)


def _dot_nt(a, b):
    return lax.dot_general(a, b, (((1,), (1,)), ((), ())), preferred_element_type=F32)


def _mod_kernel(v_ref, w_ref, b_ref, o_ref):
    v = v_ref[...]
    a = v * jax.nn.sigmoid(v)
    o_ref[0] = jnp.dot(a, w_ref[0], precision=HIGHEST, preferred_element_type=F32) + b_ref[0]


def _modulation(vecs, w_mod, b_mod):
    rows = vecs.shape[0]
    tn = 1024
    return pl.pallas_call(
        _mod_kernel,
        grid=(DEPTH, 3 * D_MODEL // tn),
        in_specs=[pl.BlockSpec((rows, D_MODEL), lambda l, j: (0, 0)),
                  pl.BlockSpec((1, D_MODEL, tn), lambda l, j: (l, 0, j)),
                  pl.BlockSpec((1, 1, tn), lambda l, j: (l, 0, j))],
        out_specs=pl.BlockSpec((1, rows, tn), lambda l, j: (l, 0, j)),
        out_shape=jax.ShapeDtypeStruct((DEPTH, rows, 3 * D_MODEL), F32),
        compiler_params=_cparams(("arbitrary", "arbitrary")),
        name="modulation",
    )(vecs, w_mod, b_mod.reshape(DEPTH, 1, 3 * D_MODEL))


_EVEN_SECTIONS = (("q", 0, NA_WIDTH, "qnorm"), ("k", NA_WIDTH, 2 * NA_WIDTH, "knorm"),
                  ("v", 2 * NA_WIDTH, 3 * NA_WIDTH, "copy"),
                  ("hy", 3 * NA_WIDTH, 3 * NA_WIDTH + 3 * HY_WIDTH, "copy"),
                  ("gate", IN_WIDTH - D_MODEL, IN_WIDTH, "silu"))
_ODD_SECTIONS = (("q", 0, DIFF_QK, "qnorm"), ("k", DIFF_QK, 2 * DIFF_QK, "knorm"),
                 ("v", 2 * DIFF_QK, 3 * DIFF_QK, "copy"),
                 ("gate", IN_WIDTH - D_MODEL, IN_WIDTH, "silu"))


def _rope_chunk(a, cos, sin_signed, low_half):
    up = pltpu.roll(a, LANES - HEAD_DIM // 2, axis=1)
    dn = pltpu.roll(a, HEAD_DIM // 2, axis=1)
    return a * cos + jnp.where(low_half, up, dn) * sin_signed


def _inproj_kernel(*refs, sections, rope):
    x_ref, shift_ref, scale_ref, g_ref, w_ref, qg_ref, kg_ref, e_ref = refs[:8]
    pos = 8
    if rope:
        cos_ref, sin_ref = refs[8:10]
        pos = 10
    out_refs = refs[pos:]
    x = x_ref[0]
    ms = jnp.mean(x * x, axis=-1, keepdims=True)
    h = x * lax.rsqrt(ms + EPS) * g_ref[...] * (1.0 + scale_ref[0]) + shift_ref[0]
    hb = h.astype(BF16)
    if rope:
        cos = cos_ref[...]
        sin_signed = sin_ref[...]
        lane = lax.broadcasted_iota(jnp.int32, cos.shape, 1)
        low_half = (lane % HEAD_DIM) < HEAD_DIM // 2
    def finish(acc, o_ref, off, kind):
        if kind in ("qnorm", "knorm"):
            gain = qg_ref[...] if kind == "qnorm" else kg_ref[...]
            ss = _dot((acc * acc).astype(BF16), e_ref[...])
            acc = acc * lax.rsqrt(ss * (1.0 / HEAD_DIM) + EPS) * gain
            if rope:
                acc = jnp.concatenate(
                    [_rope_chunk(acc[:, j:j + LANES], cos, sin_signed, low_half)
                     for j in range(0, MXU_DIM, LANES)], axis=1)
        elif kind == "silu":
            acc = acc * jax.nn.sigmoid(acc)
        o_ref[0, :, off:off + MXU_DIM] = acc.astype(o_ref.dtype)

    pending = None
    for o_ref, (_, lo, hi, kind) in zip(out_refs, sections):
        for c0 in range(lo, hi, MXU_DIM):
            acc = _dot(hb, w_ref[:, c0:c0 + MXU_DIM])
            if pending is not None:
                finish(*pending)
            pending = (acc, o_ref, c0 - lo, kind)
    finish(*pending)


def _rope_tables(n_tokens):
    t = jnp.arange(n_tokens, dtype=jnp.int32)
    row = (t // GRID_W).astype(F32)
    col = (t % GRID_W).astype(F32)
    n_freq = HEAD_DIM // 4
    inv = ROPE_BASE ** (-jnp.arange(n_freq, dtype=F32) / n_freq)
    ang = jnp.concatenate([row[:, None] * inv, col[:, None] * inv], axis=-1)
    cos, sin = jnp.cos(ang), jnp.sin(ang)
    cos_t = jnp.concatenate([cos, cos, cos, cos], axis=-1)
    sin_t = jnp.concatenate([-sin, sin, -sin, sin], axis=-1)
    return cos_t, sin_t


def _inproj(x, shift, scale, g, w, qg, kg, *, odd, rope_tabs=None):
    B, S, _ = x.shape
    tm = min(S, 512)
    sections = _ODD_SECTIONS if odd else _EVEN_SECTIONS
    rope = rope_tabs is not None
    head = jnp.arange(MXU_DIM) // HEAD_DIM
    e = (head[:, None] == head[None, :]).astype(BF16)
    q_scale = HEAD_DIM ** -0.5 * LOG2E
    qg_t = jnp.tile(qg.astype(F32), MXU_DIM // HEAD_DIM)[None] * q_scale
    kg_t = jnp.tile(kg.astype(F32), MXU_DIM // HEAD_DIM)[None]
    const = lambda b, i: (0, 0)
    in_specs = [pl.BlockSpec((1, tm, D_MODEL), lambda b, i: (b, i, 0)),
                pl.BlockSpec((1, 1, D_MODEL), lambda b, i: (b, 0, 0)),
                pl.BlockSpec((1, 1, D_MODEL), lambda b, i: (b, 0, 0)),
                pl.BlockSpec((1, D_MODEL), const),
                _single((D_MODEL, IN_WIDTH), const),
                pl.BlockSpec((1, MXU_DIM), const),
                pl.BlockSpec((1, MXU_DIM), const),
                pl.BlockSpec((MXU_DIM, MXU_DIM), const)]
    args = [x, shift[:, None], scale[:, None], g[None], w, qg_t, kg_t, e]
    if rope:
        in_specs += [pl.BlockSpec((tm, LANES), lambda b, i: (i, 0))] * 2
        args += list(rope_tabs)
    out_shape = [jax.ShapeDtypeStruct((B, S, hi - lo), BF16) for _, lo, hi, _ in sections]
    out_specs = [pl.BlockSpec((1, tm, hi - lo), lambda b, i: (b, i, 0)) for _, lo, hi, _ in sections]
    return pl.pallas_call(
        functools.partial(_inproj_kernel, sections=sections, rope=rope),
        grid=(B, S // tm),
        in_specs=in_specs, out_specs=out_specs, out_shape=out_shape,
        compiler_params=_cparams(("arbitrary", "arbitrary")),
        name="inproj_odd" if odd else "inproj_even",
    )(*args)


def _outproj_kernel(*refs, n_parts):
    x_ref, gm_ref, gate_ref, w_ref = refs[:4]
    parts = refs[4:4 + n_parts]
    o_ref = refs[4 + n_parts]
    y = jnp.concatenate([p[0].astype(F32) for p in parts], axis=1) * gate_ref[0].astype(F32)
    o_ref[0] = x_ref[0] + gm_ref[0] * _dot(y.astype(BF16), w_ref[...])


def _outproj(x, gate_mod, gate, w, parts):
    B, S, _ = x.shape
    tm = min(S, 512)
    in_specs = [pl.BlockSpec((1, tm, D_MODEL), lambda b, i: (b, i, 0)),
                pl.BlockSpec((1, 1, D_MODEL), lambda b, i: (b, 0, 0)),
                pl.BlockSpec((1, tm, D_MODEL), lambda b, i: (b, i, 0)),
                _single((D_MODEL, D_MODEL), lambda b, i: (0, 0))]
    in_specs += [pl.BlockSpec((1, tm, p.shape[-1]), lambda b, i: (b, i, 0)) for p in parts]
    return pl.pallas_call(
        functools.partial(_outproj_kernel, n_parts=len(parts)),
        grid=(B, S // tm),
        in_specs=in_specs,
        out_specs=pl.BlockSpec((1, tm, D_MODEL), lambda b, i: (b, i, 0)),
        out_shape=jax.ShapeDtypeStruct((B, S, D_MODEL), F32),
        compiler_params=_cparams(("arbitrary", "arbitrary")),
        name="outproj",
    )(x, gate_mod[:, None], gate, w, *parts)


def _head_masks(shape):
    lane = lax.broadcasted_iota(jnp.int32, shape, 1)
    return lane < HEAD_DIM, lane >= HEAD_DIM


def _split_heads_t(qt):
    row = lax.broadcasted_iota(jnp.int32, qt.shape, 0)
    zero = jnp.zeros_like(qt)
    return jnp.concatenate([jnp.where(row < HEAD_DIM, qt, zero), jnp.where(row >= HEAD_DIM, qt, zero)], axis=1)


def _na_kernel(q_ref, k_ref, v_ref, kc_ref, vc_ref, bias_ref, o_ref, *scratch, n_rows):
    s_bufs = scratch[:len(scratch) // 2]
    p_bufs = scratch[len(scratch) // 2:]
    i = pl.program_id(2)
    start = jnp.clip(i * NA_Q_ROWS - WIN_R // 2, 0, n_rows - NA_WIN_ROWS)
    chunk = NA_CHUNK_ROWS * GRID_W
    off = pl.multiple_of(start * GRID_W, chunk)
    tq = q_ref.shape[1]
    q2 = _split_heads_t(q_ref[0].T)
    n_loc = NA_WIN_ROWS // NA_CHUNK_ROWS
    cmax = []
    for t in range(n_loc):
        s = _dot(k_ref[0, pl.ds(off + t * chunk, chunk), :], q2) + bias_ref[0, 0, t * chunk:(t + 1) * chunk, :]
        s_bufs[t][...] = s
        cmax.append(jnp.max(s, axis=0, keepdims=True))
    s = _dot(kc_ref[0], q2)
    s_bufs[n_loc][...] = s
    cmax.append(jnp.max(s, axis=0, keepdims=True))
    m = acc = None
    for t in range(n_loc + 1):
        m_new = cmax[t] if m is None else jnp.maximum(m, cmax[t])
        p_bufs[t][...] = jnp.exp2((s_bufs[t][...] - m_new).astype(BF16))
        v_t = v_ref[0, pl.ds(off + t * chunk, chunk), :] if t < n_loc else vc_ref[0]
        ones = jnp.ones((ONES_ROWS, v_t.shape[0]), v_t.dtype)
        pv = _dot(jnp.concatenate([v_t.T, ones], axis=0), p_bufs[t][...])
        acc = pv if acc is None else jnp.exp2(m - m_new) * acc + pv
        m = m_new
    o = jnp.concatenate([acc[:HEAD_DIM, :tq] / acc[LANES:LANES + 1, :tq],
                         acc[HEAD_DIM:LANES, tq:] / acc[LANES:LANES + 1, tq:]], axis=0)
    o_ref[0] = o.T.astype(o_ref.dtype)


def _rpb_cols_kernel(r_ref, p_ref, m_ref, o_ref):
    gathered = jnp.dot(r_ref[...], p_ref[...], precision=HIGHEST, preferred_element_type=F32)
    o_ref[...] = gathered * LOG2E + m_ref[...]


def _na_bias_kernel(cols_ref, o_ref, *, drow_tables):
    masked = jnp.full((GRID_W, GRID_W), NEG_BIG, F32)
    for t, drow in enumerate(drow_tables):
        @pl.when(pl.program_id(1) == t)
        def _(drow=drow):
            for j in range(NA_WIN_ROWS):
                for h2 in range(2):
                    for a in range(0, NA_Q_ROWS, 2):
                        pair = [masked if drow[b][j] is None else cols_ref[h2, drow[b][j]] for b in (a, a + 1)]
                        lane0 = (h2 * NA_Q_ROWS + a) * GRID_W
                        o_ref[0, 0, j * GRID_W:(j + 1) * GRID_W, lane0:lane0 + 2 * GRID_W] = (
                            jnp.concatenate(pair, axis=1))


def _na_bias_table(rpb, n_rows):
    n_heads, n_drow, n_dcol = rpb.shape
    n_blk = n_rows // NA_Q_ROWS
    kr = min(WIN_R, n_rows)
    c = np.arange(GRID_W)
    cs = np.clip(c - WIN_C // 2, 0, GRID_W - WIN_C)
    col_ok = (c[None, :] >= cs[:, None]) & (c[None, :] < cs[:, None] + WIN_C)
    dcol = np.clip(c[None, :] - c[:, None] + WIN_C - 1, 0, 2 * WIN_C - 2)
    pad = -n_dcol % 8
    onehot = (np.arange(n_dcol + pad)[:, None] == dcol.T.reshape(1, -1)).astype(np.float32)
    col_mask = np.where(col_ok.T, 0.0, NEG_BIG).astype(np.float32).reshape(1, -1)
    r2 = jnp.pad(rpb.astype(F32).reshape(n_heads * n_drow, n_dcol), ((0, 0), (0, pad)))
    cols = pl.pallas_call(
        _rpb_cols_kernel,
        out_shape=jax.ShapeDtypeStruct((n_heads * n_drow, GRID_W * GRID_W), F32),
        name="rpb_cols",
    )(r2, jnp.asarray(onehot), jnp.asarray(col_mask))
    cols = cols.reshape(n_heads, n_drow, GRID_W, GRID_W)
    drow_tables = []
    for i in (0, 1, n_blk - 1):
        r0 = i * NA_Q_ROWS
        start = min(max(r0 - WIN_R // 2, 0), n_rows - NA_WIN_ROWS)
        table = []
        for a in range(NA_Q_ROWS):
            r = r0 + a
            rs = min(max(r - kr // 2, 0), n_rows - kr)
            table.append(tuple(start + j - r + WIN_R - 1 if rs <= start + j < rs + kr else None
                               for j in range(NA_WIN_ROWS)))
        drow_tables.append(tuple(table))
    tq, tk = NA_Q_ROWS * GRID_W, NA_WIN_ROWS * GRID_W
    return pl.pallas_call(
        functools.partial(_na_bias_kernel, drow_tables=tuple(drow_tables)),
        grid=(n_heads // 2, len(drow_tables)),
        in_specs=[pl.BlockSpec((2, n_drow, GRID_W, GRID_W), lambda hp, t: (hp, 0, 0, 0))],
        out_specs=pl.BlockSpec((1, 1, tk, 2 * tq), lambda hp, t: (hp, t, 0, 0)),
        out_shape=jax.ShapeDtypeStruct((n_heads // 2, len(drow_tables), tk, 2 * tq), F32),
        compiler_params=_cparams(("arbitrary", "arbitrary")),
        name="na_bias",
    )(cols)


def _na_attention(q, k, v, kc, vc, rpb):
    B, S, _ = q.shape
    n_ctx = kc.shape[1]
    n_rows = S // GRID_W
    n_blk = n_rows // NA_Q_ROWS
    assert n_rows % NA_Q_ROWS == 0 and n_rows >= NA_WIN_ROWS
    assert all(x % NA_CHUNK_ROWS == 0 for x in (NA_Q_ROWS, NA_WIN_ROWS, WIN_R // 2, n_rows))
    tq = NA_Q_ROWS * GRID_W
    tk = NA_WIN_ROWS * GRID_W
    chunk = NA_CHUNK_ROWS * GRID_W
    bias = _na_bias_table(rpb, n_rows)

    def bias_map(b, hp, i):
        return (hp, jnp.where(i == 0, 0, jnp.where(i == n_blk - 1, 2, 1)), 0, 0)

    return pl.pallas_call(
        functools.partial(_na_kernel, n_rows=n_rows),
        grid=(B, NA_WIDTH // LANES, n_blk),
        in_specs=[pl.BlockSpec((1, tq, LANES), lambda b, hp, i: (b, i, hp)),
                  pl.BlockSpec((1, S, LANES), lambda b, hp, i: (b, 0, hp)),
                  pl.BlockSpec((1, S, LANES), lambda b, hp, i: (b, 0, hp)),
                  pl.BlockSpec((1, n_ctx, LANES), lambda b, hp, i: (b, 0, hp)),
                  pl.BlockSpec((1, n_ctx, LANES), lambda b, hp, i: (b, 0, hp)),
                  pl.BlockSpec((1, 1, tk, 2 * tq), bias_map)],
        out_specs=pl.BlockSpec((1, tq, LANES), lambda b, hp, i: (b, i, hp)),
        out_shape=jax.ShapeDtypeStruct((B, S, NA_WIDTH), BF16),
        scratch_shapes=[pltpu.VMEM((n, 2 * tq), dt) for dt in (F32, BF16)
                        for n in [chunk] * (NA_WIN_ROWS // NA_CHUNK_ROWS) + [n_ctx]],
        compiler_params=_cparams(("arbitrary", "arbitrary", "arbitrary")),
        name="na_attention",
    )(q, k, v, kc, vc, bias)


def _ctx_attn_kernel(q_ref, k_ref, v_ref, o_ref):
    q = q_ref[0]
    k = k_ref[0]
    v = v_ref[0]
    outs = []
    for sel in _head_masks(q.shape):
        qh = jnp.where(sel, q, jnp.zeros_like(q))
        s = _dot_nt(qh, k)
        p = jnp.exp2(s - jnp.max(s, axis=-1, keepdims=True))
        outs.append(_dot(p.astype(BF16), v) / jnp.sum(p, axis=-1, keepdims=True))
    lo_half, _ = _head_masks(outs[0].shape)
    o_ref[0] = jnp.where(lo_half, outs[0], outs[1]).astype(o_ref.dtype)


def _ctx_attention(q, k, v):
    B, n, width = q.shape
    spec = pl.BlockSpec((1, n, LANES), lambda b, hp: (b, 0, hp))
    return pl.pallas_call(
        _ctx_attn_kernel,
        grid=(B, width // LANES),
        in_specs=[spec, spec, spec], out_specs=spec,
        out_shape=jax.ShapeDtypeStruct((B, n, width), BF16),
        compiler_params=_cparams(("arbitrary", "arbitrary")),
        name="ctx_attention",
    )(q, k, v)


def _diff_kernel(*refs, tk, sk, se, lam_init):
    lq1_ref, lk1_ref, lq2_ref, lk2_ref, sg_ref, q_ref, k_ref, v_ref = refs[:8]
    pos = 8
    if se:
        ke_ref, ve_ref = refs[8:10]
        pos = 10
    o_ref = refs[pos]
    scratch = refs[pos + 1:]
    s_bufs = scratch[:DIFF_BUFS]
    p_bufs = scratch[DIFF_BUFS:2 * DIFF_BUFS]
    acc_ref = scratch[2 * DIFF_BUFS]
    tq = q_ref.shape[1]
    q2 = _split_heads_t(q_ref[0].T)
    chunks = [(k_ref, v_ref, r, min(tk, sk - r)) for r in range(0, sk, tk)]
    if se:
        chunks += [(ke_ref, ve_ref, r, min(tk, se - r)) for r in range(0, se, tk)]

    def scores(j):
        kr, _, r, n = chunks[j]
        s = _dot(kr[0, r:r + n, :], q2)
        s_bufs[j % DIFF_BUFS][0:n, :] = s
        return jnp.max(s, axis=0, keepdims=True)

    ahead = DIFF_BUFS - 1
    cmax = [scores(j) for j in range(min(ahead, len(chunks)))]
    m = None
    for j, (_, vr, r, n) in enumerate(chunks):
        if j + ahead < len(chunks):
            cmax.append(scores(j + ahead))
        m_new = cmax[j] if m is None else jnp.maximum(m, cmax[j])
        p_buf = p_bufs[j % DIFF_BUFS]
        p_buf[0:n, :] = jnp.exp2((s_bufs[j % DIFF_BUFS][0:n, :] - m_new).astype(BF16))
        vt = jnp.concatenate([vr[0, r:r + n, :].T, jnp.ones((ONES_ROWS, n), BF16)], axis=0)
        pv = _dot(vt, p_buf[0:n, :])
        acc_ref[...] = pv if m is None else jnp.exp2(m - m_new) * acc_ref[...] + pv
        m = m_new
    l = acc_ref[LANES:LANES + 1, :]
    acc = acc_ref[:LANES, :]

    lam = (jnp.exp(jnp.sum(lq1_ref[...] * lk1_ref[...], axis=-1, keepdims=True))
           - jnp.exp(jnp.sum(lq2_ref[...] * lk2_ref[...], axis=-1, keepdims=True)) + lam_init)
    o = acc / l
    o = o[:, :tq] - lam * o[:, tq:]
    o = o * lax.rsqrt(jnp.mean(o * o, axis=0, keepdims=True) + EPS) * sg_ref[...]
    o_ref[0] = (o * (1.0 - lam_init)).T.astype(o_ref.dtype)


def _diff_attention(q, k, v, lam_vecs, subln_g, lam_init, k_extra=None, v_extra=None):
    B, sq, width = q.shape
    sk = k.shape[1]
    tq = min(sq, DIFF_TQ)
    tk = min(sk, DIFF_TK)
    se = 0 if k_extra is None else k_extra.shape[1]
    small = pl.BlockSpec((1, HEAD_DIM), lambda b, h, i: (0, 0))
    in_specs = [small] * 4 + [
        pl.BlockSpec((LANES, 1), lambda b, h, i: (0, 0)),
        pl.BlockSpec((1, tq, LANES), lambda b, h, i: (b, i, h)),
        pl.BlockSpec((1, sk, LANES), lambda b, h, i: (b, 0, h)),
        pl.BlockSpec((1, sk, LANES), lambda b, h, i: (b, 0, h))]
    args = [a.astype(F32)[None] for a in lam_vecs] + [subln_g.astype(F32)[:, None], q, k, v]
    if se:
        in_specs += [pl.BlockSpec((1, se, LANES), lambda b, h, i: (b, 0, h))] * 2
        args += [k_extra, v_extra]
    return pl.pallas_call(
        functools.partial(_diff_kernel, tk=tk, sk=sk, se=se, lam_init=lam_init),
        grid=(B, width // LANES, sq // tq),
        in_specs=in_specs,
        out_specs=pl.BlockSpec((1, tq, LANES), lambda b, h, i: (b, i, h)),
        out_shape=jax.ShapeDtypeStruct((B, sq, width), BF16),
        scratch_shapes=[pltpu.VMEM((tk, 2 * tq), F32)] * DIFF_BUFS + [pltpu.VMEM((tk, 2 * tq), BF16)] * DIFF_BUFS
        + [pltpu.VMEM((LANES + ONES_ROWS, 2 * tq), F32)],
        compiler_params=_cparams(("arbitrary", "arbitrary", "arbitrary")),
        name="diff_attention",
    )(*args)


def _filter_kernel(z_ref, w1_ref, b1_ref, fr_ref, w2_ref, b2_ref, w3_ref, b3_ref, dec_ref, h_ref, ss_ref):
    hdot = functools.partial(jnp.dot, precision=HIGHEST, preferred_element_type=F32)
    fr = fr_ref[...]
    h = jnp.sin(fr * (hdot(z_ref[...], w1_ref[...]) + b1_ref[...]))
    h = jnp.sin(fr * (hdot(h, w2_ref[...]) + b2_ref[...]))
    h = hdot(h, w3_ref[...]) + b3_ref[...]
    dec = dec_ref[...]
    h = h * jnp.concatenate([dec] * (2 * HY_ORDER), axis=1)
    h_ref[...] = h

    @pl.when(pl.program_id(0) == 0)
    def _():
        ss_ref[...] = jnp.zeros(ss_ref.shape, F32)

    ss_ref[...] += jnp.sum(h * h, axis=0, keepdims=True)


def _hyena_filters(L, w1, b1, freq, w2, b2, w3, b3):
    t = jnp.linspace(0.0, 1.0, L, dtype=F32)[:, None]
    w = (2.0 * math.pi / L) * jnp.arange(L, dtype=F32)[:, None]
    bands = (HY_EMB - 1) // 2
    fb = jnp.linspace(1e-4, bands - 1, bands, dtype=F32)[None, :]
    z = jnp.concatenate([t, jnp.cos(fb * w), -jnp.sin(fb * w)], axis=-1)
    emb_pad = HY_HIDDEN - HY_EMB
    z = jnp.pad(z, ((0, 0), (0, emb_pad)))
    w1p = jnp.pad(w1.astype(F32), ((0, emb_pad), (0, 0)))
    min_decay = math.log(HY_TARGET) / HY_SLOW_DECAY
    max_decay = math.log(HY_TARGET) / HY_FAST_DECAY
    deltas = jnp.abs(jnp.linspace(min_decay, max_decay, HY_WIDTH, dtype=F32))
    decay = jnp.exp(-t * deltas[None, :])
    tl = min(L, 512)
    width = HY_ORDER * 2 * HY_WIDTH
    const = lambda i: (0, 0)
    return pl.pallas_call(
        _filter_kernel,
        grid=(L // tl,),
        in_specs=[pl.BlockSpec((tl, HY_HIDDEN), lambda i: (i, 0)),
                  pl.BlockSpec((HY_HIDDEN, HY_HIDDEN), const), pl.BlockSpec((1, HY_HIDDEN), const),
                  pl.BlockSpec((1, HY_HIDDEN), const),
                  pl.BlockSpec((HY_HIDDEN, HY_HIDDEN), const), pl.BlockSpec((1, HY_HIDDEN), const),
                  pl.BlockSpec((HY_HIDDEN, width), const), pl.BlockSpec((1, width), const),
                  pl.BlockSpec((tl, HY_WIDTH), lambda i: (i, 0))],
        out_specs=[pl.BlockSpec((tl, width), lambda i: (i, 0)), pl.BlockSpec((1, width), const)],
        out_shape=[jax.ShapeDtypeStruct((L, width), F32), jax.ShapeDtypeStruct((1, width), F32)],
        compiler_params=_cparams(("arbitrary",)),
        name="hyena_filters",
    )(z, w1p, b1.astype(F32)[None], freq.astype(F32)[None], w2.astype(F32), b2.astype(F32)[None],
      w3.astype(F32), b3.astype(F32)[None], decay)


def _fft_sizes(L):
    n2 = 128 if L >= 1024 else 32
    n1 = 2 * L // n2
    return n1, n2


@functools.lru_cache(maxsize=None)
def _dft_tables(L):
    n1s, n2s = _fft_sizes(L)
    N = 2 * L
    k1 = np.arange(n1s)
    n1 = np.arange(n1s // 2)
    n2 = np.arange(n2s)
    th1 = 2.0 * np.pi * ((k1[:, None] * n1[None, :]) % n1s) / n1s
    c, s = np.cos(th1), np.sin(th1)
    eye = np.eye(HY_ROWS)
    lk = np.kron(np.block([[c, s], [-s, c]]), eye)
    lki = np.kron(np.block([[c.T, -s.T], [s.T, c.T]]), eye)
    tht = 2.0 * np.pi * ((k1[:, None] * n2[None, :]) % N) / N
    tw = tht.reshape(n1s, n2s // HY_ROWS, HY_ROWS).transpose(1, 0, 2).reshape(n2s // HY_ROWS, n1s * HY_ROWS, 1)
    twc = np.broadcast_to(np.cos(tw), tw.shape[:2] + (LANES,))
    tws = np.broadcast_to(np.sin(tw), tw.shape[:2] + (LANES,))
    th3 = 2.0 * np.pi * ((n2[:, None] * n2[None, :]) % n2s) / n2s
    c3, s3 = np.cos(th3), np.sin(th3)
    l3 = np.block([[c3, s3], [-s3, c3]])
    l3i = np.block([[c3, -s3], [s3, c3]])
    return tuple(np.asarray(a, np.float32) for a in (lk, twc, tws, l3, l3i, lki))


def _tiles(buf, n, g):
    rows = pl.ds(pl.multiple_of(g * HY_ROWS, HY_ROWS), HY_ROWS)
    return jnp.concatenate(
        [jnp.concatenate([buf[h, i, rows, :] for i in range(n)], axis=0) for h in range(2)], axis=1)


def _put_tiles(buf, n, g, val):
    rows = pl.ds(pl.multiple_of(g * HY_ROWS, HY_ROWS), HY_ROWS)
    for h in range(2):
        for i in range(n):
            buf[h, i, rows, :] = val[i * HY_ROWS:(i + 1) * HY_ROWS, h * LANES:(h + 1) * LANES]


def _twiddle(twc_ref, tws_ref, g):
    c = twc_ref[g].astype(F32)
    s = tws_ref[g].astype(F32)
    return jnp.concatenate([c, c], axis=1), jnp.concatenate([s, s], axis=1)


def _fft_stage1(zr, zi, ar, ai, lk_ref, twc_ref, tws_ref, n1s, n2s):
    half = n1s // 2
    m = n1s * HY_ROWS

    def body(g, carry):
        d = jnp.concatenate([_tiles(zr, half, g), _tiles(zi, half, g)], axis=0)
        out = _dot(lk_ref[...], d.astype(BF16))
        c, s = _twiddle(twc_ref, tws_ref, g)
        o_r, o_i = out[:m], out[m:]
        _put_tiles(ar, n1s, g, o_r * c + o_i * s)
        _put_tiles(ai, n1s, g, o_i * c - o_r * s)
        return carry

    lax.fori_loop(0, n2s // HY_ROWS, body, 0, unroll=4)


def _fft_stage3(ar, ai, l3_ref, k1):
    d = jnp.concatenate([jnp.concatenate([ar[0, k1], ar[1, k1]], axis=1),
                         jnp.concatenate([ai[0, k1], ai[1, k1]], axis=1)], axis=0)
    return _dot(l3_ref[...], d.astype(BF16))


def _put_rows(buf, k1, val):
    buf[0, k1] = val[:, :LANES]
    buf[1, k1] = val[:, LANES:]


def _spectrum_kernel(hf_ref, hb_ref, ssf_ref, ssb_ref, skip_ref, lk_ref, twc_ref, tws_ref, l3_ref, gr_ref, gi_ref,
                     zr, zi, ar, ai, *, n1s, n2s):
    nrm = lax.rsqrt(ssf_ref[...] + ssb_ref[...] + EPS)
    hf = hf_ref[...] * nrm
    hb = hb_ref[...] * nrm
    shape = zr.shape[1:]
    zr[0] = (hf + hb).reshape(shape)
    zr[1] = (hf - hb).reshape(shape)
    zi[...] = jnp.zeros(zi.shape, F32)
    _fft_stage1(zr, zi, ar, ai, lk_ref, twc_ref, tws_ref, n1s, n2s)
    inv_n = 1.0 / (n1s * n2s)

    def body(k1, carry):
        z = _fft_stage3(ar, ai, l3_ref, k1)
        r = pl.multiple_of(k1 * n2s, n2s)
        gr_ref[0, pl.ds(r, n2s), :] = (z[:n2s, :LANES] + skip_ref[0]) * inv_n
        gi_ref[0, pl.ds(r, n2s), :] = z[n2s:, LANES:] * inv_n
        return carry

    lax.fori_loop(0, n1s, body, 0, unroll=HY_UNROLL)


def _hyena_scratch(L):
    n1s, n2s = _fft_sizes(L)
    return [pltpu.VMEM((2, n1s // 2, n2s, LANES), F32), pltpu.VMEM((2, n1s // 2, n2s, LANES), F32),
            pltpu.VMEM((2, n1s, n2s, LANES), F32), pltpu.VMEM((2, n1s, n2s, LANES), F32)]


def _table_specs(tables):
    zeros = {2: (lambda *_: (0, 0)), 3: (lambda *_: (0, 0, 0))}
    return [_single(t.shape, zeros[t.ndim]) for t in tables]


def _hyena_spectra(h, ss, skip):
    L = h.shape[0]
    n1s, n2s = _fft_sizes(L)
    N = 2 * L
    tables = _dft_tables(L)[:4]
    n_ct = HY_WIDTH // LANES
    fwd = lambda o, ct: (0, o * 2 * n_ct + ct)
    bwd = lambda o, ct: (0, o * 2 * n_ct + n_ct + ct)
    out_spec = _single((1, N, LANES), lambda o, ct: (o, 0, ct))
    return pl.pallas_call(
        functools.partial(_spectrum_kernel, n1s=n1s, n2s=n2s),
        grid=(HY_ORDER, n_ct),
        in_specs=[_single((L, LANES), fwd), _single((L, LANES), bwd),
                  pl.BlockSpec((1, LANES), fwd), pl.BlockSpec((1, LANES), bwd),
                  pl.BlockSpec((1, 1, LANES), lambda o, ct: (o, 0, ct))] + _table_specs(tables),
        out_specs=[out_spec, out_spec],
        out_shape=[jax.ShapeDtypeStruct((HY_ORDER, N, HY_WIDTH), F32)] * 2,
        scratch_shapes=_hyena_scratch(L),
        compiler_params=_cparams(("arbitrary", "arbitrary")),
        name="hyena_spectra",
    )(h, h, ss, ss, skip[:, None], *[jnp.asarray(t, BF16) for t in tables])


def _short_conv(u, w_ref, b_ref):
    L, C = u.shape
    t = u.reshape(L // HY_ROWS, HY_ROWS, C)
    sub = lax.broadcasted_iota(jnp.int32, (1, HY_ROWS, C), 1)
    zero_tile = jnp.zeros((1, HY_ROWS, C), u.dtype)
    down = pltpu.roll(t, 1, axis=1)
    prev = jnp.where(sub == 0, jnp.concatenate([zero_tile, down[:-1]], axis=0), down).reshape(L, C)
    up = pltpu.roll(t, HY_ROWS - 1, axis=1)
    nxt = jnp.where(sub == HY_ROWS - 1, jnp.concatenate([up[1:], zero_tile], axis=0), up).reshape(L, C)
    return b_ref[...] + prev * w_ref[0:1, :] + u * w_ref[1:2, :] + nxt * w_ref[2:3, :]


def _conv_kernel(a_ref, x_ref, wa_ref, ba_ref, wx_ref, bx_ref, gr_ref, gi_ref,
                 lk_ref, twc_ref, tws_ref, l3_ref, l3i_ref, lki_ref, o_ref, zr, zi, ar, ai, *, conv_a, n1s, n2s):
    half = n1s // 2
    shape = zr.shape[1:]
    slots = ((zr, 0), (zr, 1), (zi, 0), (zi, 1))
    for s, (buf, hi) in enumerate(slots):
        u = a_ref[s].astype(F32)
        buf[hi] = (_short_conv(u, wa_ref, ba_ref) if conv_a else u).reshape(shape)
    _fft_stage1(zr, zi, ar, ai, lk_ref, twc_ref, tws_ref, n1s, n2s)

    def mid(k1, carry):
        z = _fft_stage3(ar, ai, l3_ref, k1)
        r = pl.multiple_of(k1 * n2s, n2s)
        g_r = gr_ref[0, pl.ds(r, n2s), :]
        g_i = gi_ref[0, pl.ds(r, n2s), :]
        g_r = jnp.concatenate([g_r, g_r], axis=1)
        g_i = jnp.concatenate([g_i, g_i], axis=1)
        z_r, z_i = z[:n2s], z[n2s:]
        p = jnp.concatenate([z_r * g_r - z_i * g_i, z_r * g_i + z_i * g_r], axis=0)
        b = _dot(l3i_ref[...], p.astype(BF16))
        _put_rows(ar, k1, b[:n2s])
        _put_rows(ai, k1, b[n2s:])
        return carry

    lax.fori_loop(0, n1s, mid, 0, unroll=HY_UNROLL)
    m = n1s * HY_ROWS

    def last(g, carry):
        c, s = _twiddle(twc_ref, tws_ref, g)
        b_r, b_i = _tiles(ar, n1s, g), _tiles(ai, n1s, g)
        d = jnp.concatenate([b_r * c - b_i * s, b_i * c + b_r * s], axis=0)
        y = _dot(lki_ref[...], d.astype(BF16))
        _put_tiles(zr, half, g, y[:m // 2])
        _put_tiles(zi, half, g, y[m // 2:])
        return carry

    lax.fori_loop(0, n2s // HY_ROWS, last, 0, unroll=4)
    for s, (buf, hi) in enumerate(slots):
        xg = _short_conv(x_ref[s].astype(F32), wx_ref, bx_ref)
        o_ref[s] = (xg * buf[hi].reshape(xg.shape)).astype(o_ref.dtype)


def _hyena_conv(a, a_col, x, x_col, conv_w, conv_b, wa_col, wx_col, g_r, g_i, order, *, conv_a, out_dtype):
    B, L, _ = a.shape
    n1s, n2s = _fft_sizes(L)
    N = 2 * L
    n_ct = HY_WIDTH // LANES
    seqs = 4
    assert B % seqs == 0
    tables = _dft_tables(L)
    return pl.pallas_call(
        functools.partial(_conv_kernel, conv_a=conv_a, n1s=n1s, n2s=n2s),
        grid=(n_ct, B // seqs),
        in_specs=[_single((seqs, L, LANES), lambda ct, b: (b, 0, a_col + ct)),
                  _single((seqs, L, LANES), lambda ct, b: (b, 0, x_col + ct)),
                  pl.BlockSpec((3, LANES), lambda ct, b: (0, wa_col + ct)),
                  pl.BlockSpec((1, LANES), lambda ct, b: (0, wa_col + ct)),
                  pl.BlockSpec((3, LANES), lambda ct, b: (0, wx_col + ct)),
                  pl.BlockSpec((1, LANES), lambda ct, b: (0, wx_col + ct)),
                  _single((1, N, LANES), lambda ct, b: (order, 0, ct)),
                  _single((1, N, LANES), lambda ct, b: (order, 0, ct))] + _table_specs(tables),
        out_specs=_single((seqs, L, LANES), lambda ct, b: (b, 0, ct)),
        out_shape=jax.ShapeDtypeStruct((B, L, HY_WIDTH), out_dtype),
        scratch_shapes=_hyena_scratch(L),
        compiler_params=_cparams(("arbitrary", "arbitrary")),
        name="hyena_conv",
    )(a, x, conv_w, conv_b[None], conv_w, conv_b[None], g_r, g_i, *[jnp.asarray(t, BF16) for t in tables])


def _hyena(hy, conv_w, conv_b, filt, skip):
    L = hy.shape[1]
    n_ct = HY_WIDTH // LANES
    h, ss = _hyena_filters(L, *filt)
    g_r, g_i = _hyena_spectra(h, ss, skip.astype(F32))
    conv_w = conv_w.astype(F32)
    conv_b = conv_b.astype(F32)
    z1 = _hyena_conv(hy, 0, hy, n_ct, conv_w, conv_b, 0, n_ct, g_r, g_i, 0, conv_a=True, out_dtype=BF16)
    return _hyena_conv(z1, 0, hy, 2 * n_ct, conv_w, conv_b, 0, 2 * n_ct, g_r, g_i, 1, conv_a=False, out_dtype=BF16)


def kernel(x, c, ctx, c_ctx, norm_g, w_mod, b_mod, w_in, w_out, q_norm_g, k_norm_g, na_rpb, hy_conv_w, hy_conv_b, hy_filt_w1, hy_filt_b1, hy_filt_freq, hy_filt_w2, hy_filt_b2, hy_filt_w3, hy_filt_b3, hy_skip, diff_lam_q1, diff_lam_k1, diff_lam_q2, diff_lam_k2, diff_subln_g):
    B, S, D = x.shape
    mod_rows = -(-(B + 1) // 8) * 8
    vecs = jnp.concatenate([c, c_ctx[None], jnp.zeros((mod_rows - B - 1, D), F32)], axis=0)
    mods = _modulation(vecs, w_mod, b_mod)
    w_in_b = w_in.astype(BF16)
    w_out_b = w_out.astype(BF16)
    rope_tabs = _rope_tables(S)
    xc = ctx
    for l in range(DEPTH):
        ctx_out = l < DEPTH - 1
        shift, scale, gate = (mods[l, :B, j * D:(j + 1) * D] for j in range(3))
        c_shift, c_scale, c_gate = (jnp.broadcast_to(mods[l, B:B + 1, j * D:(j + 1) * D], (B, D)) for j in range(3))
        odd = l % 2 == 1
        lat = _inproj(x, shift, scale, norm_g[l], w_in_b[l], q_norm_g[l], k_norm_g[l], odd=odd,
                      rope_tabs=rope_tabs if odd else None)
        cx = _inproj(xc, c_shift, c_scale, norm_g[l], w_in_b[l], q_norm_g[l], k_norm_g[l], odd=odd)
        if not odd:
            e = l // 2
            q, k, v, hy, g = lat
            qc, kc, vc, hyc, gc = cx
            filt = (hy_filt_w1[e], hy_filt_b1[e], hy_filt_freq[e], hy_filt_w2[e], hy_filt_b2[e],
                    hy_filt_w3[e], hy_filt_b3[e])
            o_na = _na_attention(q, k, v, kc, vc, na_rpb[e])
            o_hy = _hyena(hy, hy_conv_w[e], hy_conv_b[e], filt, hy_skip[e])
            parts = [o_na, o_hy]
            if ctx_out:
                c_parts = [_ctx_attention(qc, kc, vc), _hyena(hyc, hy_conv_w[e], hy_conv_b[e], filt, hy_skip[e])]
        else:
            o_i = l // 2
            lam_init = 0.8 - 0.6 * math.exp(-0.3 * l)
            lam_vecs = (diff_lam_q1[o_i], diff_lam_k1[o_i], diff_lam_q2[o_i], diff_lam_k2[o_i])
            q, k, v, g = lat
            qc, kc, vc, gc = cx
            parts = [_diff_attention(q, k, v, lam_vecs, diff_subln_g[o_i], lam_init, kc, vc)]
            if ctx_out:
                c_parts = [_diff_attention(qc, kc, vc, lam_vecs, diff_subln_g[o_i], lam_init)]
        x = _outproj(x, gate, g, w_out_b[l], parts)
        if ctx_out:
            xc = _outproj(xc, c_gate, gc, w_out_b[l], c_parts)
    return x
```

```python
import functools
import math

import numpy as np
import jax
import jax.numpy as jnp
from jax import lax
from jax.experimental import pallas as pl
from jax.experimental.pallas import tpu as pltpu

F32 = jnp.float32
BF16 = jnp.bfloat16
HIGHEST = lax.Precision.HIGHEST

D_MODEL = 1024
DEPTH = 4
GRID_W = 64
HEAD_DIM = 64
N_HEADS_NA = 8
NA_WIDTH = N_HEADS_NA * HEAD_DIM
HY_WIDTH = D_MODEL - NA_WIDTH
HY_ORDER = 2
HY_EMB = 33
HY_HIDDEN = 64
HY_FAST_DECAY = 0.3
HY_SLOW_DECAY = 1.5
HY_TARGET = 1e-2
WIN_R = 8
WIN_C = 16
N_HEADS_DIFF = D_MODEL // (2 * HEAD_DIM)
DIFF_QK = N_HEADS_DIFF * 2 * HEAD_DIM
IN_WIDTH = 4 * D_MODEL
EPS = 1e-6
ROPE_BASE = 10000.0

LANES = 128
MXU_DIM = 256
VMEM_LIMIT = 56 * 1024 * 1024
NEG_BIG = -1e30
LOG2E = math.log2(math.e)
ONES_ROWS = 16

NA_Q_ROWS = 8
NA_WIN_ROWS = 16
NA_CHUNK_ROWS = 4
DIFF_BUFS = 3
DIFF_AHEAD = 1
DIFF_TK = 256
DIFF_TQ = 512
HY_UNROLL = 16
HY_ROWS = 8


def _cparams(sem):
    return pltpu.CompilerParams(dimension_semantics=sem, vmem_limit_bytes=VMEM_LIMIT)


def _single(shape, index_map):
    return pl.BlockSpec(shape, index_map, pipeline_mode=pl.Buffered(1))


def _dot(a, b):
    return jnp.dot(a, b, preferred_element_type=F32)


def _dot_nt(a, b):
    return lax.dot_general(a, b, (((1,), (1,)), ((), ())), preferred_element_type=F32)


def _mod_kernel(v_ref, w_ref, b_ref, o_ref):
    v = v_ref[...]
    a = v * jax.nn.sigmoid(v)
    o_ref[0] = jnp.dot(a, w_ref[0], precision=HIGHEST, preferred_element_type=F32) + b_ref[0]


def _modulation(vecs, w_mod, b_mod):
    rows = vecs.shape[0]
    tn = 1024
    return pl.pallas_call(
        _mod_kernel,
        grid=(DEPTH, 3 * D_MODEL // tn),
        in_specs=[pl.BlockSpec((rows, D_MODEL), lambda l, j: (0, 0)),
                  pl.BlockSpec((1, D_MODEL, tn), lambda l, j: (l, 0, j)),
                  pl.BlockSpec((1, 1, tn), lambda l, j: (l, 0, j))],
        out_specs=pl.BlockSpec((1, rows, tn), lambda l, j: (l, 0, j)),
        out_shape=jax.ShapeDtypeStruct((DEPTH, rows, 3 * D_MODEL), F32),
        compiler_params=_cparams(("arbitrary", "arbitrary")),
        name="modulation",
    )(vecs, w_mod, b_mod.reshape(DEPTH, 1, 3 * D_MODEL))


_EVEN_SECTIONS = (("q", 0, NA_WIDTH, "qnorm"), ("k", NA_WIDTH, 2 * NA_WIDTH, "knorm"),
                  ("v", 2 * NA_WIDTH, 3 * NA_WIDTH, "copy"),
                  ("hy", 3 * NA_WIDTH, 3 * NA_WIDTH + 3 * HY_WIDTH, "copy"),
                  ("gate", IN_WIDTH - D_MODEL, IN_WIDTH, "silu"))
_ODD_SECTIONS = (("q", 0, DIFF_QK, "qnorm"), ("k", DIFF_QK, 2 * DIFF_QK, "knorm"),
                 ("v", 2 * DIFF_QK, 3 * DIFF_QK, "copy"),
                 ("gate", IN_WIDTH - D_MODEL, IN_WIDTH, "silu"))


def _rope_chunk(a, cos, sin_signed, low_half):
    up = pltpu.roll(a, LANES - HEAD_DIM // 2, axis=1)
    dn = pltpu.roll(a, HEAD_DIM // 2, axis=1)
    return a * cos + jnp.where(low_half, up, dn) * sin_signed


def _inproj_kernel(*refs, sections, rope):
    x_ref, shift_ref, scale_ref, g_ref, w_ref, qg_ref, kg_ref, e_ref = refs[:8]
    pos = 8
    if rope:
        cos_ref, sin_ref = refs[8:10]
        pos = 10
    out_refs = refs[pos:]
    x = x_ref[0]
    ms = jnp.mean(x * x, axis=-1, keepdims=True)
    h = x * lax.rsqrt(ms + EPS) * g_ref[...] * (1.0 + scale_ref[0]) + shift_ref[0]
    hb = h.astype(BF16)
    if rope:
        cos = cos_ref[...]
        sin_signed = sin_ref[...]
        lane = lax.broadcasted_iota(jnp.int32, cos.shape, 1)
        low_half = (lane % HEAD_DIM) < HEAD_DIM // 2
    def finish(acc, o_ref, off, kind):
        if kind in ("qnorm", "knorm"):
            gain = qg_ref[...] if kind == "qnorm" else kg_ref[...]
            ss = _dot((acc * acc).astype(BF16), e_ref[...])
            acc = acc * lax.rsqrt(ss * (1.0 / HEAD_DIM) + EPS) * gain
            if rope:
                acc = jnp.concatenate(
                    [_rope_chunk(acc[:, j:j + LANES], cos, sin_signed, low_half)
                     for j in range(0, MXU_DIM, LANES)], axis=1)
        elif kind == "silu":
            acc = acc * jax.nn.sigmoid(acc)
        o_ref[0, :, off:off + MXU_DIM] = acc.astype(o_ref.dtype)

    pending = None
    for o_ref, (_, lo, hi, kind) in zip(out_refs, sections):
        for c0 in range(lo, hi, MXU_DIM):
            acc = _dot(hb, w_ref[:, c0:c0 + MXU_DIM])
            if pending is not None:
                finish(*pending)
            pending = (acc, o_ref, c0 - lo, kind)
    finish(*pending)


def _rope_tables(n_tokens):
    t = jnp.arange(n_tokens, dtype=jnp.int32)
    row = (t // GRID_W).astype(F32)
    col = (t % GRID_W).astype(F32)
    n_freq = HEAD_DIM // 4
    inv = ROPE_BASE ** (-jnp.arange(n_freq, dtype=F32) / n_freq)
    ang = jnp.concatenate([row[:, None] * inv, col[:, None] * inv], axis=-1)
    cos, sin = jnp.cos(ang), jnp.sin(ang)
    cos_t = jnp.concatenate([cos, cos, cos, cos], axis=-1)
    sin_t = jnp.concatenate([-sin, sin, -sin, sin], axis=-1)
    return cos_t, sin_t


def _inproj(x, shift, scale, g, w, qg, kg, *, odd, rope_tabs=None):
    B, S, _ = x.shape
    tm = min(S, 512)
    sections = _ODD_SECTIONS if odd else _EVEN_SECTIONS
    rope = rope_tabs is not None
    head = jnp.arange(MXU_DIM) // HEAD_DIM
    e = (head[:, None] == head[None, :]).astype(BF16)
    q_scale = HEAD_DIM ** -0.5 * LOG2E
    qg_t = jnp.tile(qg.astype(F32), MXU_DIM // HEAD_DIM)[None] * q_scale
    kg_t = jnp.tile(kg.astype(F32), MXU_DIM // HEAD_DIM)[None]
    const = lambda b, i: (0, 0)
    in_specs = [pl.BlockSpec((1, tm, D_MODEL), lambda b, i: (b, i, 0)),
                pl.BlockSpec((1, 1, D_MODEL), lambda b, i: (b, 0, 0)),
                pl.BlockSpec((1, 1, D_MODEL), lambda b, i: (b, 0, 0)),
                pl.BlockSpec((1, D_MODEL), const),
                _single((D_MODEL, IN_WIDTH), const),
                pl.BlockSpec((1, MXU_DIM), const),
                pl.BlockSpec((1, MXU_DIM), const),
                pl.BlockSpec((MXU_DIM, MXU_DIM), const)]
    args = [x, shift[:, None], scale[:, None], g[None], w, qg_t, kg_t, e]
    if rope:
        in_specs += [pl.BlockSpec((tm, LANES), lambda b, i: (i, 0))] * 2
        args += list(rope_tabs)
    out_shape = [jax.ShapeDtypeStruct((B, S, hi - lo), BF16) for _, lo, hi, _ in sections]
    out_specs = [pl.BlockSpec((1, tm, hi - lo), lambda b, i: (b, i, 0)) for _, lo, hi, _ in sections]
    return pl.pallas_call(
        functools.partial(_inproj_kernel, sections=sections, rope=rope),
        grid=(B, S // tm),
        in_specs=in_specs, out_specs=out_specs, out_shape=out_shape,
        compiler_params=_cparams(("arbitrary", "arbitrary")),
        name="inproj_odd" if odd else "inproj_even",
    )(*args)


def _outproj_kernel(*refs, n_parts):
    x_ref, gm_ref, gate_ref, w_ref = refs[:4]
    parts = refs[4:4 + n_parts]
    o_ref = refs[4 + n_parts]
    y = jnp.concatenate([p[0].astype(F32) for p in parts], axis=1) * gate_ref[0].astype(F32)
    o_ref[0] = x_ref[0] + gm_ref[0] * _dot(y.astype(BF16), w_ref[...])


def _outproj(x, gate_mod, gate, w, parts):
    B, S, _ = x.shape
    tm = min(S, 512)
    in_specs = [pl.BlockSpec((1, tm, D_MODEL), lambda b, i: (b, i, 0)),
                pl.BlockSpec((1, 1, D_MODEL), lambda b, i: (b, 0, 0)),
                pl.BlockSpec((1, tm, D_MODEL), lambda b, i: (b, i, 0)),
                _single((D_MODEL, D_MODEL), lambda b, i: (0, 0))]
    in_specs += [pl.BlockSpec((1, tm, p.shape[-1]), lambda b, i: (b, i, 0)) for p in parts]
    return pl.pallas_call(
        functools.partial(_outproj_kernel, n_parts=len(parts)),
        grid=(B, S // tm),
        in_specs=in_specs,
        out_specs=pl.BlockSpec((1, tm, D_MODEL), lambda b, i: (b, i, 0)),
        out_shape=jax.ShapeDtypeStruct((B, S, D_MODEL), F32),
        compiler_params=_cparams(("arbitrary", "arbitrary")),
        name="outproj",
    )(x, gate_mod[:, None], gate, w, *parts)


def _head_masks(shape):
    lane = lax.broadcasted_iota(jnp.int32, shape, 1)
    return lane < HEAD_DIM, lane >= HEAD_DIM


def _split_heads_t(qt):
    row = lax.broadcasted_iota(jnp.int32, qt.shape, 0)
    zero = jnp.zeros_like(qt)
    return jnp.concatenate([jnp.where(row < HEAD_DIM, qt, zero), jnp.where(row >= HEAD_DIM, qt, zero)], axis=1)


def _na_kernel(q_ref, k_ref, v_ref, kc_ref, vc_ref, bias_ref, o_ref, *scratch, n_rows):
    s_bufs = scratch[:len(scratch) // 2]
    p_bufs = scratch[len(scratch) // 2:]
    i = pl.program_id(2)
    start = jnp.clip(i * NA_Q_ROWS - WIN_R // 2, 0, n_rows - NA_WIN_ROWS)
    chunk = NA_CHUNK_ROWS * GRID_W
    off = pl.multiple_of(start * GRID_W, chunk)
    tq = q_ref.shape[1]
    q2 = _split_heads_t(q_ref[0].T)
    n_loc = NA_WIN_ROWS // NA_CHUNK_ROWS
    def scores(t):
        if t < n_loc:
            s = _dot(k_ref[0, pl.ds(off + t * chunk, chunk), :], q2) + bias_ref[0, 0, t * chunk:(t + 1) * chunk, :]
        else:
            s = _dot(kc_ref[0], q2)
        s_bufs[t][...] = s
        return jnp.max(s, axis=0, keepdims=True)

    ahead = 1
    cmax = [scores(t) for t in range(ahead)]
    m = acc = None
    for t in range(n_loc + 1):
        if t + ahead <= n_loc:
            cmax.append(scores(t + ahead))
        m_new = cmax[t] if m is None else jnp.maximum(m, cmax[t])
        p_bufs[t][...] = jnp.exp2((s_bufs[t][...] - m_new).astype(BF16))
        v_t = v_ref[0, pl.ds(off + t * chunk, chunk), :] if t < n_loc else vc_ref[0]
        ones = jnp.ones((ONES_ROWS, v_t.shape[0]), v_t.dtype)
        pv = _dot(jnp.concatenate([v_t.T, ones], axis=0), p_bufs[t][...])
        acc = pv if acc is None else jnp.exp2(m - m_new) * acc + pv
        m = m_new
    o = jnp.concatenate([acc[:HEAD_DIM, :tq] / acc[LANES:LANES + 1, :tq],
                         acc[HEAD_DIM:LANES, tq:] / acc[LANES:LANES + 1, tq:]], axis=0)
    o_ref[0] = o.T.astype(o_ref.dtype)


def _rpb_cols_kernel(r_ref, p_ref, m_ref, o_ref):
    gathered = jnp.dot(r_ref[...], p_ref[...], precision=HIGHEST, preferred_element_type=F32)
    o_ref[...] = gathered * LOG2E + m_ref[...]


def _na_bias_kernel(cols_ref, o_ref, *, drow_tables):
    masked = jnp.full((GRID_W, GRID_W), NEG_BIG, F32)
    for t, drow in enumerate(drow_tables):
        @pl.when(pl.program_id(1) == t)
        def _(drow=drow):
            for j in range(NA_WIN_ROWS):
                for h2 in range(2):
                    for a in range(0, NA_Q_ROWS, 2):
                        pair = [masked if drow[b][j] is None else cols_ref[h2, drow[b][j]] for b in (a, a + 1)]
                        lane0 = (h2 * NA_Q_ROWS + a) * GRID_W
                        o_ref[0, 0, j * GRID_W:(j + 1) * GRID_W, lane0:lane0 + 2 * GRID_W] = (
                            jnp.concatenate(pair, axis=1))


def _na_bias_table(rpb, n_rows):
    n_heads, n_drow, n_dcol = rpb.shape
    n_blk = n_rows // NA_Q_ROWS
    kr = min(WIN_R, n_rows)
    c = np.arange(GRID_W)
    cs = np.clip(c - WIN_C // 2, 0, GRID_W - WIN_C)
    col_ok = (c[None, :] >= cs[:, None]) & (c[None, :] < cs[:, None] + WIN_C)
    dcol = np.clip(c[None, :] - c[:, None] + WIN_C - 1, 0, 2 * WIN_C - 2)
    pad = -n_dcol % 8
    onehot = (np.arange(n_dcol + pad)[:, None] == dcol.T.reshape(1, -1)).astype(np.float32)
    col_mask = np.where(col_ok.T, 0.0, NEG_BIG).astype(np.float32).reshape(1, -1)
    r2 = jnp.pad(rpb.astype(F32).reshape(n_heads * n_drow, n_dcol), ((0, 0), (0, pad)))
    cols = pl.pallas_call(
        _rpb_cols_kernel,
        out_shape=jax.ShapeDtypeStruct((n_heads * n_drow, GRID_W * GRID_W), F32),
        name="rpb_cols",
    )(r2, jnp.asarray(onehot), jnp.asarray(col_mask))
    cols = cols.reshape(n_heads, n_drow, GRID_W, GRID_W)
    drow_tables = []
    for i in (0, 1, n_blk - 1):
        r0 = i * NA_Q_ROWS
        start = min(max(r0 - WIN_R // 2, 0), n_rows - NA_WIN_ROWS)
        table = []
        for a in range(NA_Q_ROWS):
            r = r0 + a
            rs = min(max(r - kr // 2, 0), n_rows - kr)
            table.append(tuple(start + j - r + WIN_R - 1 if rs <= start + j < rs + kr else None
                               for j in range(NA_WIN_ROWS)))
        drow_tables.append(tuple(table))
    tq, tk = NA_Q_ROWS * GRID_W, NA_WIN_ROWS * GRID_W
    return pl.pallas_call(
        functools.partial(_na_bias_kernel, drow_tables=tuple(drow_tables)),
        grid=(n_heads // 2, len(drow_tables)),
        in_specs=[pl.BlockSpec((2, n_drow, GRID_W, GRID_W), lambda hp, t: (hp, 0, 0, 0))],
        out_specs=pl.BlockSpec((1, 1, tk, 2 * tq), lambda hp, t: (hp, t, 0, 0)),
        out_shape=jax.ShapeDtypeStruct((n_heads // 2, len(drow_tables), tk, 2 * tq), F32),
        compiler_params=_cparams(("arbitrary", "arbitrary")),
        name="na_bias",
    )(cols)


def _na_attention(q, k, v, kc, vc, rpb):
    B, S, _ = q.shape
    n_ctx = kc.shape[1]
    n_rows = S // GRID_W
    n_blk = n_rows // NA_Q_ROWS
    assert n_rows % NA_Q_ROWS == 0 and n_rows >= NA_WIN_ROWS
    assert all(x % NA_CHUNK_ROWS == 0 for x in (NA_Q_ROWS, NA_WIN_ROWS, WIN_R // 2, n_rows))
    tq = NA_Q_ROWS * GRID_W
    tk = NA_WIN_ROWS * GRID_W
    chunk = NA_CHUNK_ROWS * GRID_W
    bias = _na_bias_table(rpb, n_rows)

    def bias_map(b, hp, i):
        return (hp, jnp.where(i == 0, 0, jnp.where(i == n_blk - 1, 2, 1)), 0, 0)

    return pl.pallas_call(
        functools.partial(_na_kernel, n_rows=n_rows),
        grid=(B, NA_WIDTH // LANES, n_blk),
        in_specs=[pl.BlockSpec((1, tq, LANES), lambda b, hp, i: (b, i, hp)),
                  pl.BlockSpec((1, S, LANES), lambda b, hp, i: (b, 0, hp)),
                  pl.BlockSpec((1, S, LANES), lambda b, hp, i: (b, 0, hp)),
                  pl.BlockSpec((1, n_ctx, LANES), lambda b, hp, i: (b, 0, hp)),
                  pl.BlockSpec((1, n_ctx, LANES), lambda b, hp, i: (b, 0, hp)),
                  pl.BlockSpec((1, 1, tk, 2 * tq), bias_map)],
        out_specs=pl.BlockSpec((1, tq, LANES), lambda b, hp, i: (b, i, hp)),
        out_shape=jax.ShapeDtypeStruct((B, S, NA_WIDTH), BF16),
        scratch_shapes=[pltpu.VMEM((n, 2 * tq), dt) for dt in (F32, BF16)
                        for n in [chunk] * (NA_WIN_ROWS // NA_CHUNK_ROWS) + [n_ctx]],
        compiler_params=_cparams(("arbitrary", "arbitrary", "arbitrary")),
        name="na_attention",
    )(q, k, v, kc, vc, bias)


def _ctx_attn_kernel(q_ref, k_ref, v_ref, o_ref):
    q = q_ref[0]
    k = k_ref[0]
    v = v_ref[0]
    outs = []
    for sel in _head_masks(q.shape):
        qh = jnp.where(sel, q, jnp.zeros_like(q))
        s = _dot_nt(qh, k)
        p = jnp.exp2(s - jnp.max(s, axis=-1, keepdims=True))
        outs.append(_dot(p.astype(BF16), v) / jnp.sum(p, axis=-1, keepdims=True))
    lo_half, _ = _head_masks(outs[0].shape)
    o_ref[0] = jnp.where(lo_half, outs[0], outs[1]).astype(o_ref.dtype)


def _ctx_attention(q, k, v):
    B, n, width = q.shape
    spec = pl.BlockSpec((1, n, LANES), lambda b, hp: (b, 0, hp))
    return pl.pallas_call(
        _ctx_attn_kernel,
        grid=(B, width // LANES),
        in_specs=[spec, spec, spec], out_specs=spec,
        out_shape=jax.ShapeDtypeStruct((B, n, width), BF16),
        compiler_params=_cparams(("arbitrary", "arbitrary")),
        name="ctx_attention",
    )(q, k, v)


def _diff_kernel(*refs, tk, sk, se, lam_init):
    lq1_ref, lk1_ref, lq2_ref, lk2_ref, sg_ref, q_ref, k_ref, v_ref = refs[:8]
    pos = 8
    if se:
        ke_ref, ve_ref = refs[8:10]
        pos = 10
    o_ref = refs[pos]
    scratch = refs[pos + 1:]
    s_bufs = scratch[:DIFF_BUFS]
    p_bufs = scratch[DIFF_BUFS:2 * DIFF_BUFS]
    acc_ref = scratch[2 * DIFF_BUFS]
    tq = q_ref.shape[1]
    q2 = _split_heads_t(q_ref[0].T)
    chunks = [(k_ref, v_ref, r, min(tk, sk - r)) for r in range(0, sk, tk)]
    if se:
        chunks += [(ke_ref, ve_ref, r, min(tk, se - r)) for r in range(0, se, tk)]

    def scores(j):
        kr, _, r, n = chunks[j]
        s = _dot(kr[0, r:r + n, :], q2)
        s_bufs[j % DIFF_BUFS][0:n, :] = s
        return jnp.max(s, axis=0, keepdims=True)

    ahead = DIFF_AHEAD
    cmax = [scores(j) for j in range(min(ahead, len(chunks)))]
    m = None
    for j, (_, vr, r, n) in enumerate(chunks):
        if j + ahead < len(chunks):
            cmax.append(scores(j + ahead))
        m_new = cmax[j] if m is None else jnp.maximum(m, cmax[j])
        p_buf = p_bufs[j % DIFF_BUFS]
        p_buf[0:n, :] = jnp.exp2((s_bufs[j % DIFF_BUFS][0:n, :] - m_new).astype(BF16))
        vt = jnp.concatenate([vr[0, r:r + n, :].T, jnp.ones((ONES_ROWS, n), BF16)], axis=0)
        pv = _dot(vt, p_buf[0:n, :])
        acc_ref[...] = pv if m is None else jnp.exp2(m - m_new) * acc_ref[...] + pv
        m = m_new
    l = acc_ref[LANES:LANES + 1, :]
    acc = acc_ref[:LANES, :]

    lam = (jnp.exp(jnp.sum(lq1_ref[...] * lk1_ref[...], axis=-1, keepdims=True))
           - jnp.exp(jnp.sum(lq2_ref[...] * lk2_ref[...], axis=-1, keepdims=True)) + lam_init)
    o = acc / l
    o = o[:, :tq] - lam * o[:, tq:]
    o = o * lax.rsqrt(jnp.mean(o * o, axis=0, keepdims=True) + EPS) * sg_ref[...]
    o_ref[0] = (o * (1.0 - lam_init)).T.astype(o_ref.dtype)


def _diff_attention(q, k, v, lam_vecs, subln_g, lam_init, k_extra=None, v_extra=None):
    B, sq, width = q.shape
    sk = k.shape[1]
    tq = min(sq, DIFF_TQ)
    tk = min(sk, DIFF_TK)
    se = 0 if k_extra is None else k_extra.shape[1]
    small = pl.BlockSpec((1, HEAD_DIM), lambda b, h, i: (0, 0))
    in_specs = [small] * 4 + [
        pl.BlockSpec((LANES, 1), lambda b, h, i: (0, 0)),
        pl.BlockSpec((1, tq, LANES), lambda b, h, i: (b, i, h)),
        pl.BlockSpec((1, sk, LANES), lambda b, h, i: (b, 0, h)),
        pl.BlockSpec((1, sk, LANES), lambda b, h, i: (b, 0, h))]
    args = [a.astype(F32)[None] for a in lam_vecs] + [subln_g.astype(F32)[:, None], q, k, v]
    if se:
        in_specs += [pl.BlockSpec((1, se, LANES), lambda b, h, i: (b, 0, h))] * 2
        args += [k_extra, v_extra]
    return pl.pallas_call(
        functools.partial(_diff_kernel, tk=tk, sk=sk, se=se, lam_init=lam_init),
        grid=(B, width // LANES, sq // tq),
        in_specs=in_specs,
        out_specs=pl.BlockSpec((1, tq, LANES), lambda b, h, i: (b, i, h)),
        out_shape=jax.ShapeDtypeStruct((B, sq, width), BF16),
        scratch_shapes=[pltpu.VMEM((tk, 2 * tq), F32)] * DIFF_BUFS + [pltpu.VMEM((tk, 2 * tq), BF16)] * DIFF_BUFS
        + [pltpu.VMEM((LANES + ONES_ROWS, 2 * tq), F32)],
        compiler_params=_cparams(("arbitrary", "arbitrary", "arbitrary")),
        name="diff_attention",
    )(*args)


def _filter_kernel(z_ref, w1_ref, b1_ref, fr_ref, w2_ref, b2_ref, w3_ref, b3_ref, dec_ref, h_ref, ss_ref):
    hdot = functools.partial(jnp.dot, precision=HIGHEST, preferred_element_type=F32)
    fr = fr_ref[...]
    h = jnp.sin(fr * (hdot(z_ref[...], w1_ref[...]) + b1_ref[...]))
    h = jnp.sin(fr * (hdot(h, w2_ref[...]) + b2_ref[...]))
    h = hdot(h, w3_ref[...]) + b3_ref[...]
    dec = dec_ref[...]
    h = h * jnp.concatenate([dec] * (2 * HY_ORDER), axis=1)
    h_ref[...] = h

    @pl.when(pl.program_id(0) == 0)
    def _():
        ss_ref[...] = jnp.zeros(ss_ref.shape, F32)

    ss_ref[...] += jnp.sum(h * h, axis=0, keepdims=True)


def _hyena_filters(L, w1, b1, freq, w2, b2, w3, b3):
    t = jnp.linspace(0.0, 1.0, L, dtype=F32)[:, None]
    w = (2.0 * math.pi / L) * jnp.arange(L, dtype=F32)[:, None]
    bands = (HY_EMB - 1) // 2
    fb = jnp.linspace(1e-4, bands - 1, bands, dtype=F32)[None, :]
    z = jnp.concatenate([t, jnp.cos(fb * w), -jnp.sin(fb * w)], axis=-1)
    emb_pad = HY_HIDDEN - HY_EMB
    z = jnp.pad(z, ((0, 0), (0, emb_pad)))
    w1p = jnp.pad(w1.astype(F32), ((0, emb_pad), (0, 0)))
    min_decay = math.log(HY_TARGET) / HY_SLOW_DECAY
    max_decay = math.log(HY_TARGET) / HY_FAST_DECAY
    deltas = jnp.abs(jnp.linspace(min_decay, max_decay, HY_WIDTH, dtype=F32))
    decay = jnp.exp(-t * deltas[None, :])
    tl = min(L, 512)
    width = HY_ORDER * 2 * HY_WIDTH
    const = lambda i: (0, 0)
    return pl.pallas_call(
        _filter_kernel,
        grid=(L // tl,),
        in_specs=[pl.BlockSpec((tl, HY_HIDDEN), lambda i: (i, 0)),
                  pl.BlockSpec((HY_HIDDEN, HY_HIDDEN), const), pl.BlockSpec((1, HY_HIDDEN), const),
                  pl.BlockSpec((1, HY_HIDDEN), const),
                  pl.BlockSpec((HY_HIDDEN, HY_HIDDEN), const), pl.BlockSpec((1, HY_HIDDEN), const),
                  pl.BlockSpec((HY_HIDDEN, width), const), pl.BlockSpec((1, width), const),
                  pl.BlockSpec((tl, HY_WIDTH), lambda i: (i, 0))],
        out_specs=[pl.BlockSpec((tl, width), lambda i: (i, 0)), pl.BlockSpec((1, width), const)],
        out_shape=[jax.ShapeDtypeStruct((L, width), F32), jax.ShapeDtypeStruct((1, width), F32)],
        compiler_params=_cparams(("arbitrary",)),
        name="hyena_filters",
    )(z, w1p, b1.astype(F32)[None], freq.astype(F32)[None], w2.astype(F32), b2.astype(F32)[None],
      w3.astype(F32), b3.astype(F32)[None], decay)


def _fft_sizes(L):
    n2 = 128 if L >= 1024 else 32
    n1 = 2 * L // n2
    return n1, n2


@functools.lru_cache(maxsize=None)
def _dft_tables(L):
    n1s, n2s = _fft_sizes(L)
    N = 2 * L
    k1 = np.arange(n1s)
    n1 = np.arange(n1s // 2)
    n2 = np.arange(n2s)
    th1 = 2.0 * np.pi * ((k1[:, None] * n1[None, :]) % n1s) / n1s
    c, s = np.cos(th1), np.sin(th1)
    eye = np.eye(HY_ROWS)
    lk = np.kron(np.block([[c, s], [-s, c]]), eye)
    lki = np.kron(np.block([[c.T, -s.T], [s.T, c.T]]), eye)
    tht = 2.0 * np.pi * ((k1[:, None] * n2[None, :]) % N) / N
    tw = tht.reshape(n1s, n2s // HY_ROWS, HY_ROWS).transpose(1, 0, 2).reshape(n2s // HY_ROWS, n1s * HY_ROWS, 1)
    twc = np.broadcast_to(np.cos(tw), tw.shape[:2] + (LANES,))
    tws = np.broadcast_to(np.sin(tw), tw.shape[:2] + (LANES,))
    th3 = 2.0 * np.pi * ((n2[:, None] * n2[None, :]) % n2s) / n2s
    c3, s3 = np.cos(th3), np.sin(th3)
    l3 = np.block([[c3, s3], [-s3, c3]])
    l3i = np.block([[c3, -s3], [s3, c3]])
    return tuple(np.asarray(a, np.float32) for a in (lk, twc, tws, l3, l3i, lki))


def _tiles(buf, n, g):
    rows = pl.ds(pl.multiple_of(g * HY_ROWS, HY_ROWS), HY_ROWS)
    return jnp.concatenate(
        [jnp.concatenate([buf[h, i, rows, :] for i in range(n)], axis=0) for h in range(2)], axis=1)


def _put_tiles(buf, n, g, val):
    rows = pl.ds(pl.multiple_of(g * HY_ROWS, HY_ROWS), HY_ROWS)
    for h in range(2):
        for i in range(n):
            buf[h, i, rows, :] = val[i * HY_ROWS:(i + 1) * HY_ROWS, h * LANES:(h + 1) * LANES]


def _twiddle(twc_ref, tws_ref, g):
    c = twc_ref[g].astype(F32)
    s = tws_ref[g].astype(F32)
    return jnp.concatenate([c, c], axis=1), jnp.concatenate([s, s], axis=1)


def _fft_stage1(zr, zi, ar, ai, lk_ref, twc_ref, tws_ref, n1s, n2s):
    half = n1s // 2
    m = n1s * HY_ROWS

    def body(g, carry):
        d = jnp.concatenate([_tiles(zr, half, g), _tiles(zi, half, g)], axis=0)
        out = _dot(lk_ref[...], d.astype(BF16))
        c, s = _twiddle(twc_ref, tws_ref, g)
        o_r, o_i = out[:m], out[m:]
        _put_tiles(ar, n1s, g, o_r * c + o_i * s)
        _put_tiles(ai, n1s, g, o_i * c - o_r * s)
        return carry

    lax.fori_loop(0, n2s // HY_ROWS, body, 0, unroll=4)


def _fft_stage3(ar, ai, l3_ref, k1):
    d = jnp.concatenate([jnp.concatenate([ar[0, k1], ar[1, k1]], axis=1),
                         jnp.concatenate([ai[0, k1], ai[1, k1]], axis=1)], axis=0)
    return _dot(l3_ref[...], d.astype(BF16))


def _put_rows(buf, k1, val):
    buf[0, k1] = val[:, :LANES]
    buf[1, k1] = val[:, LANES:]


def _spectrum_kernel(hf_ref, hb_ref, ssf_ref, ssb_ref, skip_ref, lk_ref, twc_ref, tws_ref, l3_ref, gr_ref, gi_ref,
                     zr, zi, ar, ai, *, n1s, n2s):
    nrm = lax.rsqrt(ssf_ref[...] + ssb_ref[...] + EPS)
    hf = hf_ref[...] * nrm
    hb = hb_ref[...] * nrm
    shape = zr.shape[1:]
    zr[0] = (hf + hb).reshape(shape)
    zr[1] = (hf - hb).reshape(shape)
    zi[...] = jnp.zeros(zi.shape, F32)
    _fft_stage1(zr, zi, ar, ai, lk_ref, twc_ref, tws_ref, n1s, n2s)
    inv_n = 1.0 / (n1s * n2s)

    def body(k1, carry):
        z = _fft_stage3(ar, ai, l3_ref, k1)
        r = pl.multiple_of(k1 * n2s, n2s)
        gr_ref[0, pl.ds(r, n2s), :] = (z[:n2s, :LANES] + skip_ref[0]) * inv_n
        gi_ref[0, pl.ds(r, n2s), :] = z[n2s:, LANES:] * inv_n
        return carry

    lax.fori_loop(0, n1s, body, 0, unroll=HY_UNROLL)


def _hyena_scratch(L):
    n1s, n2s = _fft_sizes(L)
    return [pltpu.VMEM((2, n1s // 2, n2s, LANES), F32), pltpu.VMEM((2, n1s // 2, n2s, LANES), F32),
            pltpu.VMEM((2, n1s, n2s, LANES), F32), pltpu.VMEM((2, n1s, n2s, LANES), F32)]


def _table_specs(tables):
    zeros = {2: (lambda *_: (0, 0)), 3: (lambda *_: (0, 0, 0))}
    return [_single(t.shape, zeros[t.ndim]) for t in tables]


def _hyena_spectra(h, ss, skip):
    L = h.shape[0]
    n1s, n2s = _fft_sizes(L)
    N = 2 * L
    tables = _dft_tables(L)[:4]
    n_ct = HY_WIDTH // LANES
    fwd = lambda o, ct: (0, o * 2 * n_ct + ct)
    bwd = lambda o, ct: (0, o * 2 * n_ct + n_ct + ct)
    out_spec = _single((1, N, LANES), lambda o, ct: (o, 0, ct))
    return pl.pallas_call(
        functools.partial(_spectrum_kernel, n1s=n1s, n2s=n2s),
        grid=(HY_ORDER, n_ct),
        in_specs=[_single((L, LANES), fwd), _single((L, LANES), bwd),
                  pl.BlockSpec((1, LANES), fwd), pl.BlockSpec((1, LANES), bwd),
                  pl.BlockSpec((1, 1, LANES), lambda o, ct: (o, 0, ct))] + _table_specs(tables),
        out_specs=[out_spec, out_spec],
        out_shape=[jax.ShapeDtypeStruct((HY_ORDER, N, HY_WIDTH), F32)] * 2,
        scratch_shapes=_hyena_scratch(L),
        compiler_params=_cparams(("arbitrary", "arbitrary")),
        name="hyena_spectra",
    )(h, h, ss, ss, skip[:, None], *[jnp.asarray(t, BF16) for t in tables])


def _short_conv(u, w_ref, b_ref):
    L, C = u.shape
    t = u.reshape(L // HY_ROWS, HY_ROWS, C)
    sub = lax.broadcasted_iota(jnp.int32, (1, HY_ROWS, C), 1)
    zero_tile = jnp.zeros((1, HY_ROWS, C), u.dtype)
    down = pltpu.roll(t, 1, axis=1)
    prev = jnp.where(sub == 0, jnp.concatenate([zero_tile, down[:-1]], axis=0), down).reshape(L, C)
    up = pltpu.roll(t, HY_ROWS - 1, axis=1)
    nxt = jnp.where(sub == HY_ROWS - 1, jnp.concatenate([up[1:], zero_tile], axis=0), up).reshape(L, C)
    return b_ref[...] + prev * w_ref[0:1, :] + u * w_ref[1:2, :] + nxt * w_ref[2:3, :]


def _conv_kernel(a_ref, x_ref, wa_ref, ba_ref, wx_ref, bx_ref, gr_ref, gi_ref,
                 lk_ref, twc_ref, tws_ref, l3_ref, l3i_ref, lki_ref, o_ref, zr, zi, ar, ai, *, conv_a, n1s, n2s):
    half = n1s // 2
    shape = zr.shape[1:]
    slots = ((zr, 0), (zr, 1), (zi, 0), (zi, 1))
    for s, (buf, hi) in enumerate(slots):
        u = a_ref[s].astype(F32)
        buf[hi] = (_short_conv(u, wa_ref, ba_ref) if conv_a else u).reshape(shape)
    _fft_stage1(zr, zi, ar, ai, lk_ref, twc_ref, tws_ref, n1s, n2s)

    def mid(k1, carry):
        z = _fft_stage3(ar, ai, l3_ref, k1)
        r = pl.multiple_of(k1 * n2s, n2s)
        g_r = gr_ref[0, pl.ds(r, n2s), :]
        g_i = gi_ref[0, pl.ds(r, n2s), :]
        g_r = jnp.concatenate([g_r, g_r], axis=1)
        g_i = jnp.concatenate([g_i, g_i], axis=1)
        z_r, z_i = z[:n2s], z[n2s:]
        p = jnp.concatenate([z_r * g_r - z_i * g_i, z_r * g_i + z_i * g_r], axis=0)
        b = _dot(l3i_ref[...], p.astype(BF16))
        _put_rows(ar, k1, b[:n2s])
        _put_rows(ai, k1, b[n2s:])
        return carry

    lax.fori_loop(0, n1s, mid, 0, unroll=HY_UNROLL)
    m = n1s * HY_ROWS

    def last(g, carry):
        c, s = _twiddle(twc_ref, tws_ref, g)
        b_r, b_i = _tiles(ar, n1s, g), _tiles(ai, n1s, g)
        d = jnp.concatenate([b_r * c - b_i * s, b_i * c + b_r * s], axis=0)
        y = _dot(lki_ref[...], d.astype(BF16))
        _put_tiles(zr, half, g, y[:m // 2])
        _put_tiles(zi, half, g, y[m // 2:])
        return carry

    lax.fori_loop(0, n2s // HY_ROWS, last, 0, unroll=4)
    for s, (buf, hi) in enumerate(slots):
        xg = _short_conv(x_ref[s].astype(F32), wx_ref, bx_ref)
        o_ref[s] = (xg * buf[hi].reshape(xg.shape)).astype(o_ref.dtype)


def _hyena_conv(a, a_col, x, x_col, conv_w, conv_b, wa_col, wx_col, g_r, g_i, order, *, conv_a, out_dtype):
    B, L, _ = a.shape
    n1s, n2s = _fft_sizes(L)
    N = 2 * L
    n_ct = HY_WIDTH // LANES
    seqs = 4
    assert B % seqs == 0
    tables = _dft_tables(L)
    return pl.pallas_call(
        functools.partial(_conv_kernel, conv_a=conv_a, n1s=n1s, n2s=n2s),
        grid=(n_ct, B // seqs),
        in_specs=[_single((seqs, L, LANES), lambda ct, b: (b, 0, a_col + ct)),
                  _single((seqs, L, LANES), lambda ct, b: (b, 0, x_col + ct)),
                  pl.BlockSpec((3, LANES), lambda ct, b: (0, wa_col + ct)),
                  pl.BlockSpec((1, LANES), lambda ct, b: (0, wa_col + ct)),
                  pl.BlockSpec((3, LANES), lambda ct, b: (0, wx_col + ct)),
                  pl.BlockSpec((1, LANES), lambda ct, b: (0, wx_col + ct)),
                  _single((1, N, LANES), lambda ct, b: (order, 0, ct)),
                  _single((1, N, LANES), lambda ct, b: (order, 0, ct))] + _table_specs(tables),
        out_specs=_single((seqs, L, LANES), lambda ct, b: (b, 0, ct)),
        out_shape=jax.ShapeDtypeStruct((B, L, HY_WIDTH), out_dtype),
        scratch_shapes=_hyena_scratch(L),
        compiler_params=_cparams(("arbitrary", "arbitrary")),
        name="hyena_conv",
    )(a, x, conv_w, conv_b[None], conv_w, conv_b[None], g_r, g_i, *[jnp.asarray(t, BF16) for t in tables])


def _hyena(hy, conv_w, conv_b, filt, skip):
    L = hy.shape[1]
    n_ct = HY_WIDTH // LANES
    h, ss = _hyena_filters(L, *filt)
    g_r, g_i = _hyena_spectra(h, ss, skip.astype(F32))
    conv_w = conv_w.astype(F32)
    conv_b = conv_b.astype(F32)
    z1 = _hyena_conv(hy, 0, hy, n_ct, conv_w, conv_b, 0, n_ct, g_r, g_i, 0, conv_a=True, out_dtype=BF16)
    return _hyena_conv(z1, 0, hy, 2 * n_ct, conv_w, conv_b, 0, 2 * n_ct, g_r, g_i, 1, conv_a=False, out_dtype=BF16)


def kernel(x, c, ctx, c_ctx, norm_g, w_mod, b_mod, w_in, w_out, q_norm_g, k_norm_g, na_rpb, hy_conv_w, hy_conv_b, hy_filt_w1, hy_filt_b1, hy_filt_freq, hy_filt_w2, hy_filt_b2, hy_filt_w3, hy_filt_b3, hy_skip, diff_lam_q1, diff_lam_k1, diff_lam_q2, diff_lam_k2, diff_subln_g):
    B, S, D = x.shape
    mod_rows = -(-(B + 1) // 8) * 8
    vecs = jnp.concatenate([c, c_ctx[None], jnp.zeros((mod_rows - B - 1, D), F32)], axis=0)
    mods = _modulation(vecs, w_mod, b_mod)
    w_in_b = w_in.astype(BF16)
    w_out_b = w_out.astype(BF16)
    rope_tabs = _rope_tables(S)
    xc = ctx
    for l in range(DEPTH):
        ctx_out = l < DEPTH - 1
        shift, scale, gate = (mods[l, :B, j * D:(j + 1) * D] for j in range(3))
        c_shift, c_scale, c_gate = (jnp.broadcast_to(mods[l, B:B + 1, j * D:(j + 1) * D], (B, D)) for j in range(3))
        odd = l % 2 == 1
        lat = _inproj(x, shift, scale, norm_g[l], w_in_b[l], q_norm_g[l], k_norm_g[l], odd=odd,
                      rope_tabs=rope_tabs if odd else None)
        cx = _inproj(xc, c_shift, c_scale, norm_g[l], w_in_b[l], q_norm_g[l], k_norm_g[l], odd=odd)
        if not odd:
            e = l // 2
            q, k, v, hy, g = lat
            qc, kc, vc, hyc, gc = cx
            filt = (hy_filt_w1[e], hy_filt_b1[e], hy_filt_freq[e], hy_filt_w2[e], hy_filt_b2[e],
                    hy_filt_w3[e], hy_filt_b3[e])
            o_na = _na_attention(q, k, v, kc, vc, na_rpb[e])
            o_hy = _hyena(hy, hy_conv_w[e], hy_conv_b[e], filt, hy_skip[e])
            parts = [o_na, o_hy]
            if ctx_out:
                c_parts = [_ctx_attention(qc, kc, vc), _hyena(hyc, hy_conv_w[e], hy_conv_b[e], filt, hy_skip[e])]
        else:
            o_i = l // 2
            lam_init = 0.8 - 0.6 * math.exp(-0.3 * l)
            lam_vecs = (diff_lam_q1[o_i], diff_lam_k1[o_i], diff_lam_q2[o_i], diff_lam_k2[o_i])
            q, k, v, g = lat
            qc, kc, vc, gc = cx
            parts = [_diff_attention(q, k, v, lam_vecs, diff_subln_g[o_i], lam_init, kc, vc)]
            if ctx_out:
                c_parts = [_diff_attention(qc, kc, vc, lam_vecs, diff_subln_g[o_i], lam_init)]
        x = _outproj(x, gate, g, w_out_b[l], parts)
        if ctx_out:
            xc = _outproj(xc, c_gate, gc, w_out_b[l], c_parts)
    return x
```
